```python
import jax, jax.numpy as jnp
from jax import lax
import numpy as np

D_MODEL = 2048
BATCH = 2
SEQ = 4096
DEPTH = 4

N_EVEN = (DEPTH + 1) // 2
N_ODD = DEPTH // 2

GLA_HEADS = 4
GLA_DK = 128
GLA_DV = 256
GLA_GATE_RANK = 16
GLA_TAU = 16.0
GLA_CHUNK = 64

DIL_PAIRS = ((128, 1), (512, 4), (2048, 16))
DIL_GROUPS = len(DIL_PAIRS)
DIL_HEADS = 4
DIL_HEAD_DIM = 128
DIL_BLOCK = 128

MLSTM_HEADS = 4
MLSTM_DQK = 128
MLSTM_DV = 256
MLSTM_CONV = 4
MLSTM_CHUNK = 64

MLA_HEADS = 8
MLA_Q_RANK = 512
MLA_KV_RANK = 512
MLA_NOPE = 128
MLA_ROPE = 64
MLA_DV = 128
ROPE_THETA = 10000.0
ATTN_BLOCK = 128

FFN_HIDDEN = -((-8 * D_MODEL) // (3 * 256)) * 256

DEEPNORM_ALPHA = (2.0 * DEPTH) ** 0.25
DEEPNORM_BETA = (8.0 * DEPTH) ** -0.25

GLA_QK_W = GLA_HEADS * GLA_DK
GLA_V_W = GLA_HEADS * GLA_DV
DIL_W = DIL_GROUPS * DIL_HEADS * DIL_HEAD_DIM
EVEN_SPLITS = (GLA_QK_W, GLA_QK_W, GLA_V_W, GLA_GATE_RANK, GLA_V_W, DIL_W, DIL_W, DIL_W)
EVEN_IN = sum(EVEN_SPLITS)
EVEN_OUT = GLA_V_W + DIL_HEADS * DIL_HEAD_DIM

MQK_W = MLSTM_HEADS * MLSTM_DQK
MV_W = MLSTM_HEADS * MLSTM_DV
ODD_SPLITS = (MQK_W, MQK_W, MV_W, MLSTM_HEADS, MLSTM_HEADS, MV_W, MLA_Q_RANK, MLA_KV_RANK, MLA_ROPE)
ODD_IN = sum(ODD_SPLITS)
ODD_OUT = MV_W + MLA_HEADS * MLA_DV

kernel_name = 'hybrid_gla_dilated_mlstm_mla_deepnorm'

F32 = jnp.float32


def _split(y, sizes):
    return jnp.split(y, np.cumsum(sizes)[:-1].tolist(), axis=-1)


def layer_norm(x, g, b, eps=1e-5):
    xf = x.astype(F32)
    mu = jnp.mean(xf, -1, keepdims=True)
    var = jnp.mean(jnp.square(xf - mu), -1, keepdims=True)
    return ((xf - mu) * lax.rsqrt(var + eps) * g.astype(F32) + b.astype(F32)).astype(x.dtype)


def head_layer_norm(x, g, eps=1e-5):
    xf = x.astype(F32)
    mu = jnp.mean(xf, -1, keepdims=True)
    var = jnp.mean(jnp.square(xf - mu), -1, keepdims=True)
    return (xf - mu) * lax.rsqrt(var + eps) * g.astype(F32)


def rms_norm(x, g, eps=1e-6):
    xf = x.astype(F32)
    return (xf * lax.rsqrt(jnp.mean(xf * xf, -1, keepdims=True) + eps) * g.astype(F32)).astype(x.dtype)


def swiglu(x, w_gu, w_d):
    gate, up = jnp.split(x @ w_gu, 2, axis=-1)
    return (jax.nn.silu(gate) * up) @ w_d


def gla(q, k, v, g1, w_g2, b_g):
    B_, S_, H, dk = q.shape
    dv = v.shape[-1]
    L = GLA_CHUNK
    N = S_ // L
    log_a = jax.nn.log_sigmoid((g1 @ w_g2 + b_g).astype(F32)) / GLA_TAU
    log_a = log_a.reshape(B_, N, L, H, dk)
    q = (q.astype(F32) * dk ** -0.5).reshape(B_, N, L, H, dk)
    k = k.astype(F32).reshape(B_, N, L, H, dk)
    v = v.astype(F32).reshape(B_, N, L, H, dv)
    b = jnp.cumsum(log_a, axis=2)
    b_last = b[:, :, -1]
    q_dec = q * jnp.exp(b)
    k_inv = k * jnp.exp(-b)
    k_end = k * jnp.exp(b_last[:, :, None] - b)
    tril = jnp.tril(jnp.ones((L, L), dtype=bool))
    scores = jnp.where(tril, jnp.einsum('bnihd,bnjhd->bnhij', q_dec, k_inv), 0.0)
    o_intra = jnp.einsum('bnhij,bnjhe->bnihe', scores, v)
    s_loc = jnp.einsum('bnjhd,bnjhe->bnhde', k_end, v)

    def step(state, inp):
        decay, s_c = inp
        return state * decay[..., None] + s_c, state

    init = jnp.zeros((B_, H, dk, dv), F32)
    _, s_prev = lax.scan(step, init, (jnp.moveaxis(jnp.exp(b_last), 1, 0), jnp.moveaxis(s_loc, 1, 0)))
    s_prev = jnp.moveaxis(s_prev, 0, 1)
    o_inter = jnp.einsum('bnihd,bnhde->bnihe', q_dec, s_prev)
    return (o_intra + o_inter).reshape(B_, S_, H, dv)


def dilated_branch(q, k, v, window, dilation):
    B_, S_, H, Dh = q.shape
    span = window // dilation
    blk = DIL_BLOCK
    unit = dilation * blk
    S_pad = -((-S_) // unit) * unit
    nb = S_pad // unit

    def regroup(t):
        t = jnp.pad(t, ((0, 0), (0, S_pad - S_), (0, 0), (0, 0)))
        t = jnp.moveaxis(t.reshape(B_, S_pad // dilation, dilation, H, Dh), 2, 1)
        return t.reshape(B_, dilation, nb, blk, H, Dh)

    def with_prev(t):
        prev = jnp.pad(t, ((0, 0), (0, 0), (1, 0), (0, 0), (0, 0), (0, 0)))[:, :, :-1]
        return jnp.concatenate([prev, t], axis=3)

    qg = regroup(q)
    kb = with_prev(regroup(k))
    vb = with_prev(regroup(v))
    s = jnp.einsum('bdnihe,bdnjhe->bdnhij', qg, kb).astype(F32) * Dh ** -0.5
    qi = jnp.arange(blk)[:, None] + blk
    kj = jnp.arange(2 * blk)[None, :]
    rel = qi - kj
    band = (rel >= 0) & (rel <= span)
    no_prev = (jnp.arange(nb) == 0)[:, None, None] & (kj < blk)[None]
    valid = band[None] & ~no_prev
    s = jnp.where(valid[:, None], s, -jnp.inf)
    m = jnp.max(s, -1, keepdims=True)
    p = jnp.exp(s - m)
    den = jnp.sum(p, -1, keepdims=True)
    o = jnp.einsum('bdnhij,bdnjhe->bdnihe', p / den, vb.astype(F32))
    lse = jnp.swapaxes((m + jnp.log(den))[..., 0], 3, 4)

    def ungroup(t):
        t = t.reshape((B_, dilation, S_pad // dilation) + t.shape[4:])
        t = jnp.moveaxis(t, 1, 2)
        return t.reshape((B_, S_pad) + t.shape[3:])[:, :S_]

    return ungroup(o), ungroup(lse)


def causal_conv(u, w, b):
    C = u.shape[-1]
    K = w.shape[0]
    out = lax.conv_general_dilated(u, w[:, None, :].astype(u.dtype), window_strides=(1,),
                                   padding=[(K - 1, 0)], dimension_numbers=('NWC', 'WIO', 'NWC'),
                                   feature_group_count=C)
    return out + b


def mlstm(q, k, v, i_pre, f_pre):
    B_, S_, H, dk = q.shape
    dv = v.shape[-1]
    L = MLSTM_CHUNK
    N = S_ // L
    q = q.astype(F32).reshape(B_, N, L, H, dk)
    k = (k.astype(F32) * dk ** -0.5).reshape(B_, N, L, H, dk)
    v = v.astype(F32).reshape(B_, N, L, H, dv)
    log_f = jax.nn.log_sigmoid(f_pre.astype(F32)).reshape(B_, N, L, H)
    log_i = i_pre.astype(F32).reshape(B_, N, L, H)
    b = jnp.cumsum(log_f, axis=2)
    b_last = b[:, :, -1]
    a = b_last[:, :, None] - b + log_i
    m_loc = jnp.max(a, axis=2)
    w = jnp.exp(a - m_loc[:, :, None])
    c_loc = jnp.einsum('bnjh,bnjhd,bnjhe->bnhde', w, k, v)
    n_loc = jnp.einsum('bnjh,bnjhd->bnhd', w, k)

    def step(carry, inp):
        c, n, m = carry
        g, cl, nl, ml = inp
        m_new = jnp.maximum(g + m, ml)
        s_old = jnp.exp(g + m - m_new)
        s_new = jnp.exp(ml - m_new)
        c_new = c * s_old[..., None, None] + cl * s_new[..., None, None]
        n_new = n * s_old[..., None] + nl * s_new[..., None]
        return (c_new, n_new, m_new), (c, n, m)

    init = (jnp.zeros((B_, H, dk, dv), F32), jnp.zeros((B_, H, dk), F32), jnp.zeros((B_, H), F32))
    xs = (jnp.moveaxis(b_last, 1, 0), jnp.moveaxis(c_loc, 1, 0), jnp.moveaxis(n_loc, 1, 0), jnp.moveaxis(m_loc, 1, 0))
    _, (c_prev, n_prev, m_prev) = lax.scan(step, init, xs)
    c_prev = jnp.moveaxis(c_prev, 0, 1)
    n_prev = jnp.moveaxis(n_prev, 0, 1)
    m_prev = jnp.moveaxis(m_prev, 0, 1)

    inter_log = b + m_prev[:, :, None, :]
    d_log = b[:, :, :, None, :] - b[:, :, None, :, :] + log_i[:, :, None, :, :]
    tril = jnp.tril(jnp.ones((L, L), dtype=bool))
    d_log = jnp.where(tril[:, :, None], d_log, -jnp.inf)
    m_t = jnp.maximum(inter_log, jnp.max(d_log, axis=3))
    dmat = jnp.exp(d_log - m_t[:, :, :, None, :])
    qk = jnp.einsum('bnihd,bnjhd->bnijh', q, k) * dmat
    inter_scale = jnp.exp(inter_log - m_t)
    num = (jnp.einsum('bnijh,bnjhe->bnihe', qk, v)
           + inter_scale[..., None] * jnp.einsum('bnihd,bnhde->bnihe', q, c_prev))
    nq = jnp.sum(qk, axis=3) + inter_scale * jnp.einsum('bnihd,bnhd->bnih', q, n_prev)
    den = jnp.maximum(jnp.abs(nq), jnp.exp(-m_t))
    return (num / den[..., None]).reshape(B_, S_, H, dv)


def rope(x, cos, sin):
    half = x.shape[-1] // 2
    x1, x2 = x[..., :half], x[..., half:]
    return jnp.concatenate([x1 * cos - x2 * sin, x1 * sin + x2 * cos], axis=-1)


def causal_latent_attention(q_nope, q_rope, k_nope, k_rope, v):
    B_, S_, H, _ = q_nope.shape
    nb = S_ // ATTN_BLOCK
    scale = (MLA_NOPE + MLA_ROPE) ** -0.5
    k_pos = jnp.arange(S_)

    def blocks(t):
        return jnp.moveaxis(t.reshape((B_, nb, ATTN_BLOCK) + t.shape[2:]), 1, 0)

    def one_block(args):
        qn, qr, i = args
        s = (jnp.einsum('bihd,bjhd->bhij', qn, k_nope)
             + jnp.einsum('bihr,bjr->bhij', qr, k_rope)).astype(F32) * scale
        q_pos = i * ATTN_BLOCK + jnp.arange(ATTN_BLOCK)
        s = jnp.where(k_pos[None, :] <= q_pos[:, None], s, -jnp.inf)
        p = jax.nn.softmax(s, axis=-1)
        return jnp.einsum('bhij,bjhe->bihe', p.astype(v.dtype), v)

    o = lax.map(one_block, (blocks(q_nope), blocks(q_rope), jnp.arange(nb)))
    return jnp.moveaxis(o, 0, 1).reshape(B_, S_, H, v.shape[-1])


def mixer_even(x, w_in, gla_wg2, gla_bg, gla_norm_g, w_o):
    B_, S_, _ = x.shape
    gq, gk, gv, gg, gr, dq, dk, dv = _split(x @ w_in, EVEN_SPLITS)
    o_a = gla(gq.reshape(B_, S_, GLA_HEADS, GLA_DK), gk.reshape(B_, S_, GLA_HEADS, GLA_DK),
              gv.reshape(B_, S_, GLA_HEADS, GLA_DV), gg, gla_wg2, gla_bg)
    o_a = rms_norm(o_a, gla_norm_g) * jax.nn.silu(gr.reshape(B_, S_, GLA_HEADS, GLA_DV).astype(F32))
    grp = (B_, S_, DIL_GROUPS, DIL_HEADS, DIL_HEAD_DIM)
    dq, dk, dv = dq.reshape(grp), dk.reshape(grp), dv.reshape(grp)
    outs = []
    lses = []
    for g, (window, dilation) in enumerate(DIL_PAIRS):
        o_g, lse_g = dilated_branch(dq[:, :, g], dk[:, :, g], dv[:, :, g], window, dilation)
        outs.append(o_g)
        lses.append(lse_g)
    wts = jax.nn.softmax(jnp.stack(lses, 0), axis=0)
    o_b = jnp.sum(wts[..., None] * jnp.stack(outs, 0), axis=0)
    o = jnp.concatenate([o_a.reshape(B_, S_, -1), o_b.reshape(B_, S_, -1)], axis=-1)
    return o.astype(x.dtype) @ w_o


def mixer_odd(x, cos, sin, w_in, conv_w, conv_b, mlstm_bi, mlstm_bf, mlstm_norm_g,
              mla_qnorm_g, mla_kvnorm_g, mla_wuq, mla_wukv, w_o):
    B_, S_, _ = x.shape
    cq, ck, cv, ci, cf, co, mq, mkv, mkr = _split(x @ w_in, ODD_SPLITS)
    qk = jax.nn.silu(causal_conv(jnp.concatenate([cq, ck], -1), conv_w, conv_b))
    cq, ck = jnp.split(qk, 2, axis=-1)
    h = mlstm(cq.reshape(B_, S_, MLSTM_HEADS, MLSTM_DQK), ck.reshape(B_, S_, MLSTM_HEADS, MLSTM_DQK),
              cv.reshape(B_, S_, MLSTM_HEADS, MLSTM_DV), ci + mlstm_bi, cf + mlstm_bf)
    o_c = jax.nn.sigmoid(co.reshape(B_, S_, MLSTM_HEADS, MLSTM_DV).astype(F32)) * head_layer_norm(h, mlstm_norm_g)
    q = (rms_norm(mq, mla_qnorm_g) @ mla_wuq).reshape(B_, S_, MLA_HEADS, MLA_NOPE + MLA_ROPE)
    q_nope, q_rope = q[..., :MLA_NOPE], q[..., MLA_NOPE:]
    q_rope = rope(q_rope, cos[:, :, None, :], sin[:, :, None, :]).astype(q.dtype)
    kv = (rms_norm(mkv, mla_kvnorm_g) @ mla_wukv).reshape(B_, S_, MLA_HEADS, MLA_NOPE + MLA_DV)
    k_nope, v = kv[..., :MLA_NOPE], kv[..., MLA_NOPE:]
    k_rope = rope(mkr, cos, sin).astype(mkr.dtype)
    o_d = causal_latent_attention(q_nope, q_rope, k_nope, k_rope, v)
    o = jnp.concatenate([o_c.reshape(B_, S_, -1).astype(x.dtype), o_d.reshape(B_, S_, -1).astype(x.dtype)], axis=-1)
    return o @ w_o


def setup_inputs(seed: int = 0) -> dict:
    key = jax.random.key(seed)
    ks = iter(jax.random.split(key, 32))

    def nrm(shape, fan_in, scale=1.0):
        return jax.random.normal(next(ks), shape, F32) * (scale * fan_in ** -0.5)

    def gain(shape):
        return 1.0 + 0.02 * jax.random.normal(next(ks), shape, F32)

    def small(shape, s=0.02):
        return s * jax.random.normal(next(ks), shape, F32)

    ne, no = N_EVEN, N_ODD
    x = jax.random.normal(next(ks), (BATCH, SEQ, D_MODEL), F32)
    positions = (jax.random.randint(next(ks), (BATCH, 1), 0, 1024, dtype=jnp.int32)
                 + jnp.arange(SEQ, dtype=jnp.int32)[None, :])
    return {
        'x': x,
        'positions': positions,
        'even_w_in': nrm((ne, D_MODEL, EVEN_IN), D_MODEL),
        'even_gla_wg2': nrm((ne, GLA_GATE_RANK, GLA_QK_W), GLA_GATE_RANK),
        'even_gla_bg': small((ne, GLA_QK_W), 0.1),
        'even_gla_norm_g': gain((ne, GLA_DV)),
        'even_w_o': nrm((ne, EVEN_OUT, D_MODEL), EVEN_OUT, DEEPNORM_BETA),
        'odd_w_in': nrm((no, D_MODEL, ODD_IN), D_MODEL),
        'odd_conv_w': nrm((no, MLSTM_CONV, 2 * MQK_W), MLSTM_CONV),
        'odd_conv_b': small((no, 2 * MQK_W)),
        'odd_mlstm_bi': small((no, MLSTM_HEADS), 0.1),
        'odd_mlstm_bf': jnp.linspace(3.0, 6.0, MLSTM_HEADS, dtype=F32)[None, :] + small((no, MLSTM_HEADS), 0.01),
        'odd_mlstm_norm_g': gain((no, MLSTM_DV)),
        'odd_mla_qnorm_g': gain((no, MLA_Q_RANK)),
        'odd_mla_kvnorm_g': gain((no, MLA_KV_RANK)),
        'odd_mla_wuq': nrm((no, MLA_Q_RANK, MLA_HEADS * (MLA_NOPE + MLA_ROPE)), MLA_Q_RANK),
        'odd_mla_wukv': nrm((no, MLA_KV_RANK, MLA_HEADS * (MLA_NOPE + MLA_DV)), MLA_KV_RANK),
        'odd_w_o': nrm((no, ODD_OUT, D_MODEL), ODD_OUT, DEEPNORM_BETA),
        'ln1_g': gain((DEPTH, D_MODEL)),
        'ln1_b': small((DEPTH, D_MODEL)),
        'ffn_wgu': nrm((DEPTH, D_MODEL, 2 * FFN_HIDDEN), D_MODEL),
        'ffn_wd': nrm((DEPTH, FFN_HIDDEN, D_MODEL), FFN_HIDDEN, DEEPNORM_BETA),
        'ln2_g': gain((DEPTH, D_MODEL)),
        'ln2_b': small((DEPTH, D_MODEL)),
    }


def reference(x, positions, even_w_in, even_gla_wg2, even_gla_bg, even_gla_norm_g, even_w_o,
              odd_w_in, odd_conv_w, odd_conv_b, odd_mlstm_bi, odd_mlstm_bf, odd_mlstm_norm_g,
              odd_mla_qnorm_g, odd_mla_kvnorm_g, odd_mla_wuq, odd_mla_wukv, odd_w_o,
              ln1_g, ln1_b, ffn_wgu, ffn_wd, ln2_g, ln2_b):
    inv_freq = ROPE_THETA ** (-jnp.arange(0, MLA_ROPE, 2, dtype=F32) / MLA_ROPE)
    angles = positions.astype(F32)[..., None] * inv_freq
    cos, sin = jnp.cos(angles), jnp.sin(angles)
    for l in range(DEPTH):
        j = l // 2
        if l % 2 == 0:
            h = mixer_even(x, even_w_in[j], even_gla_wg2[j], even_gla_bg[j], even_gla_norm_g[j], even_w_o[j])
        else:
            h = mixer_odd(x, cos, sin, odd_w_in[j], odd_conv_w[j], odd_conv_b[j], odd_mlstm_bi[j],
                          odd_mlstm_bf[j], odd_mlstm_norm_g[j], odd_mla_qnorm_g[j], odd_mla_kvnorm_g[j],
                          odd_mla_wuq[j], odd_mla_wukv[j], odd_w_o[j])
        x = layer_norm(DEEPNORM_ALPHA * x + h, ln1_g[l], ln1_b[l])
        x = layer_norm(DEEPNORM_ALPHA * x + swiglu(x, ffn_wgu[l], ffn_wd[l]), ln2_g[l], ln2_b[l])
    return x
```

```python
import functools

import jax
import jax.numpy as jnp
from jax import lax
from jax.experimental import pallas as pl
from jax.experimental.pallas import tpu as pltpu

F32 = jnp.float32
BF16 = jnp.bfloat16

D_MODEL = 2048
DEPTH = 4
GLA_HEADS, GLA_DK, GLA_DV, GLA_GATE_RANK, GLA_TAU, GLA_CHUNK = 4, 128, 256, 16, 16.0, 64
DIL_PAIRS = ((128, 1), (512, 4), (2048, 16))
DIL_HEADS, DIL_HEAD_DIM, DIL_BLOCK = 4, 128, 128
MLSTM_HEADS, MLSTM_DQK, MLSTM_DV, MLSTM_CONV, MLSTM_CHUNK = 4, 128, 256, 4, 64
MLA_HEADS, MLA_Q_RANK, MLA_KV_RANK, MLA_NOPE, MLA_ROPE, MLA_DV = 8, 512, 512, 128, 64, 128
ROPE_THETA = 10000.0
FFN_HIDDEN = 5632
DEEPNORM_ALPHA = (2.0 * DEPTH) ** 0.25

LANES = 128
VMEM_LIMIT = 56 * 1024 * 1024
NEG_INF = float("-inf")

_NT = (((1,), (1,)), ((), ()))


def _cparams(*sem):
    return pltpu.CompilerParams(dimension_semantics=sem, vmem_limit_bytes=VMEM_LIMIT)


def _bdot(a, b):
    return jnp.dot(a.astype(BF16), b.astype(BF16), preferred_element_type=F32)


def _bdot_nt(a, b):
    return lax.dot_general(a.astype(BF16), b.astype(BF16), _NT, preferred_element_type=F32)


def _bdot_tn(a, b):
    return jnp.dot(a.astype(F32).T.astype(BF16), b.astype(BF16), preferred_element_type=F32)


def _split2(a):
    hi = a.astype(BF16)
    lo = (a - hi.astype(F32)).astype(BF16)
    return hi, lo


def _dot3(a, b):
    ah, al = _split2(a)
    bh, bl = _split2(b)
    f = functools.partial(jnp.dot, preferred_element_type=F32)
    return f(ah, bh) + (f(ah, bl) + f(al, bh))


def _chunk_cumsum(tril_b, x):
    hi, lo = _split2(x)
    lo2 = (x - hi.astype(F32) - lo.astype(F32)).astype(BF16)
    f = functools.partial(jnp.dot, preferred_element_type=F32)
    return f(tril_b, hi) + (f(tril_b, lo) + f(tril_b, lo2))


def _tril(n):
    r = lax.broadcasted_iota(jnp.int32, (n, n), 0)
    c = lax.broadcasted_iota(jnp.int32, (n, n), 1)
    return c <= r


def _log_sigmoid(x):
    return jnp.minimum(x, 0.0) - jnp.log1p(jnp.exp(-jnp.abs(x)))


def _sigmoid(x):
    return 1.0 / (1.0 + jnp.exp(-x))


def _silu(x):
    return x * _sigmoid(x)


def _mm_kernel(a_ref, w_ref, o_ref):
    o_ref[...] = jnp.dot(a_ref[...], w_ref[...], preferred_element_type=F32).astype(o_ref.dtype)


def _matmul(a, w, *, tm, tn, out_dtype):
    m, k = a.shape
    n = w.shape[1]
    return pl.pallas_call(
        _mm_kernel,
        grid=(m // tm, n // tn),
        in_specs=[pl.BlockSpec((tm, k), lambda i, j: (i, 0)),
                  pl.BlockSpec((k, tn), lambda i, j: (0, j))],
        out_specs=pl.BlockSpec((tm, tn), lambda i, j: (i, j)),
        out_shape=jax.ShapeDtypeStruct((m, n), out_dtype),
        compiler_params=_cparams("parallel", "arbitrary"),
    )(a, w)


def _ffn_up_kernel(x_ref, wg_ref, wu_ref, o_ref):
    x = x_ref[...]
    g = jnp.dot(x, wg_ref[...], preferred_element_type=F32)
    u = jnp.dot(x, wu_ref[...], preferred_element_type=F32)
    o_ref[...] = (_silu(g) * u).astype(o_ref.dtype)


def _ffn_up(x_bf, w_gu, *, tm, tn):
    m, k = x_bf.shape
    hidden = w_gu.shape[1] // 2
    nb = hidden // tn
    return pl.pallas_call(
        _ffn_up_kernel,
        grid=(m // tm, nb),
        in_specs=[pl.BlockSpec((tm, k), lambda i, j: (i, 0)),
                  pl.BlockSpec((k, tn), lambda i, j: (0, j)),
                  pl.BlockSpec((k, tn), lambda i, j: (0, j + nb))],
        out_specs=pl.BlockSpec((tm, tn), lambda i, j: (i, j)),
        out_shape=jax.ShapeDtypeStruct((m, hidden), BF16),
        compiler_params=_cparams("parallel", "arbitrary"),
    )(x_bf, w_gu, w_gu)


def _mm_ln_kernel(a_ref, w_ref, res_ref, g_ref, b_ref, of_ref, ob_ref, acc_ref, *, nk):
    k = pl.program_id(1)

    @pl.when(k == 0)
    def _():
        acc_ref[...] = jnp.zeros_like(acc_ref)

    acc_ref[...] += jnp.dot(a_ref[...], w_ref[...], preferred_element_type=F32)

    @pl.when(k == nk - 1)
    def _():
        y = DEEPNORM_ALPHA * res_ref[...] + acc_ref[...]
        mu = jnp.mean(y, axis=-1, keepdims=True)
        yc = y - mu
        var = jnp.mean(yc * yc, axis=-1, keepdims=True)
        out = yc * lax.rsqrt(var + 1e-5) * g_ref[...] + b_ref[...]
        of_ref[...] = out
        ob_ref[...] = out.astype(BF16)


def _mm_ln(a, w, res, g, b, *, tm, tk):
    m, kdim = a.shape
    n = w.shape[1]
    nk = kdim // tk
    return pl.pallas_call(
        functools.partial(_mm_ln_kernel, nk=nk),
        grid=(m // tm, nk),
        in_specs=[pl.BlockSpec((tm, tk), lambda i, k: (i, k)),
                  pl.BlockSpec((tk, n), lambda i, k: (k, 0)),
                  pl.BlockSpec((tm, n), lambda i, k: (i, 0)),
                  pl.BlockSpec((1, n), lambda i, k: (0, 0)),
                  pl.BlockSpec((1, n), lambda i, k: (0, 0))],
        out_specs=[pl.BlockSpec((tm, n), lambda i, k: (i, 0)),
                   pl.BlockSpec((tm, n), lambda i, k: (i, 0))],
        out_shape=[jax.ShapeDtypeStruct((m, n), F32), jax.ShapeDtypeStruct((m, n), BF16)],
        scratch_shapes=[pltpu.VMEM((tm, n), F32)],
        compiler_params=_cparams("parallel", "arbitrary"),
    )(a, w, res, g, b)


def _gla_kernel(q_ref, k_ref, v_ref, r_ref, g_ref, wg2_ref, bg_ref, ng_ref, o_ref, st_ref, *, tile):
    L, H, DK, DV = GLA_CHUNK, GLA_HEADS, GLA_DK, GLA_DV

    @pl.when(pl.program_id(1) == 0)
    def _():
        st_ref[...] = jnp.zeros_like(st_ref)

    tril = _tril(L)
    tril_b = tril.astype(BF16)
    g16 = g_ref[:, :GLA_GATE_RANK]
    for h in range(H):
        ks = slice(h * DK, (h + 1) * DK)
        vs = slice(h * DV, (h + 1) * DV)
        log_a = _log_sigmoid(_dot3(g16, wg2_ref[:, ks]) + bg_ref[:, ks]) / GLA_TAU
        q = q_ref[:, ks] * DK ** -0.5
        k = k_ref[:, ks]
        v = v_ref[:, vs]
        st = st_ref[h]
        outs = []
        for c in range(tile // L):
            rows = slice(c * L, (c + 1) * L)
            b = _chunk_cumsum(tril_b, log_a[rows])
            b_last = b[L - 1:L]
            q_dec = (q[rows] * jnp.exp(b)).astype(BF16)
            k_inv = k[rows] * jnp.exp(-b)
            k_end = k[rows] * jnp.exp(b_last - b)
            vb = v[rows]
            scores = jnp.where(tril, _bdot_nt(q_dec, k_inv), 0.0)
            outs.append(_bdot(scores, vb) + _bdot_nt(q_dec, st))
            st = st * jnp.exp(b_last) + _bdot_tn(vb, k_end)
        st_ref[h] = st
        o = jnp.concatenate(outs, axis=0)
        o = o * lax.rsqrt(jnp.mean(o * o, axis=-1, keepdims=True) + 1e-6) * ng_ref[...]
        o_ref[:, vs] = (o * _silu(r_ref[:, vs])).astype(o_ref.dtype)


def _gla(y, gates, wg2, bg, ng, *, batch, seq, tile):
    nt = seq // tile
    qk_w, v_w = GLA_HEADS * GLA_DK, GLA_HEADS * GLA_DV
    row = lambda b, t: b * nt + t
    return pl.pallas_call(
        functools.partial(_gla_kernel, tile=tile),
        grid=(batch, nt),
        in_specs=[pl.BlockSpec((tile, qk_w), lambda b, t: (row(b, t), 0)),
                  pl.BlockSpec((tile, qk_w), lambda b, t: (row(b, t), 1)),
                  pl.BlockSpec((tile, v_w), lambda b, t: (row(b, t), 1)),
                  pl.BlockSpec((tile, v_w), lambda b, t: (row(b, t), 2)),
                  pl.BlockSpec((tile, LANES), lambda b, t: (row(b, t), 0)),
                  pl.BlockSpec((GLA_GATE_RANK, qk_w), lambda b, t: (0, 0)),
                  pl.BlockSpec((1, qk_w), lambda b, t: (0, 0)),
                  pl.BlockSpec((1, GLA_DV), lambda b, t: (0, 0))],
        out_specs=pl.BlockSpec((tile, v_w), lambda b, t: (row(b, t), 0)),
        out_shape=jax.ShapeDtypeStruct((batch * seq, v_w), BF16),
        scratch_shapes=[pltpu.VMEM((GLA_HEADS, GLA_DV, GLA_DK), F32)],
        compiler_params=_cparams("parallel", "arbitrary"),
    )(y, y, y, y, gates, wg2, bg, ng)


def _dil_block(q, kp, vp, kc, vc, prev_ok):
    n = DIL_BLOCK
    scale = DIL_HEAD_DIM ** -0.5
    r = lax.broadcasted_iota(jnp.int32, (n, n), 0)
    c = lax.broadcasted_iota(jnp.int32, (n, n), 1)
    s_p = jnp.where((c >= r) & prev_ok, _bdot_nt(q, kp) * scale, NEG_INF)
    s_c = jnp.where(c <= r, _bdot_nt(q, kc) * scale, NEG_INF)
    m = jnp.maximum(jnp.max(s_p, axis=-1, keepdims=True), jnp.max(s_c, axis=-1, keepdims=True))
    p_p = jnp.exp(s_p - m)
    p_c = jnp.exp(s_c - m)
    den = jnp.sum(p_p, axis=-1, keepdims=True) + jnp.sum(p_c, axis=-1, keepdims=True)
    o = (_bdot(p_p, vp) + _bdot(p_c, vc)) / den
    return o, m + jnp.log(den)


def _dil_kernel(*refs, unit):
    ngrp = len(DIL_PAIRS)
    in_refs, o_ref, og_ref, lg_ref = refs[:5 * ngrp], refs[5 * ngrp], refs[5 * ngrp + 1], refs[5 * ngrp + 2]
    u = pl.program_id(1)
    n = DIL_BLOCK
    for g, (_, d) in enumerate(DIL_PAIRS):
        q_ref, k_ref, v_ref, kp_ref, vp_ref = in_refs[5 * g:5 * g + 5]
        sub = n * d
        for s in range(unit // sub):
            for r in range(d):
                cur = pl.ds(s * sub + r, n, stride=d) if d > 1 else pl.ds(s * sub, n)
                if s == 0:
                    prv = pl.ds(r, n, stride=d) if d > 1 else pl.ds(0, n)
                    kp, vp, prev_ok = kp_ref[prv, :], vp_ref[prv, :], u > 0
                else:
                    prv = pl.ds((s - 1) * sub + r, n, stride=d) if d > 1 else pl.ds((s - 1) * sub, n)
                    kp, vp, prev_ok = k_ref[prv, :], v_ref[prv, :], True
                o, lse = _dil_block(q_ref[cur, :], kp, vp, k_ref[cur, :], v_ref[cur, :], prev_ok)
                og_ref[g, cur, :] = o
                lg_ref[g, cur, :] = jnp.broadcast_to(lse, (n, LANES))
    lses = [lg_ref[g] for g in range(ngrp)]
    m = functools.reduce(jnp.maximum, lses)
    ws = [jnp.exp(l - m) for l in lses]
    tot = functools.reduce(lambda a, b: a + b, ws)
    acc = sum(w * og_ref[g] for g, w in enumerate(ws))
    o_ref[...] = (acc / tot).astype(o_ref.dtype)


def _dilated(y, col0, *, batch, seq):
    ngrp, H, n = len(DIL_PAIRS), DIL_HEADS, DIL_BLOCK
    unit = max(d for _, d in DIL_PAIRS) * n
    assert seq % unit == 0
    nu = seq // unit
    cb0 = col0 // LANES
    in_specs, args = [], []
    for g, (_, d) in enumerate(DIL_PAIRS):
        sub = n * d
        per = unit // sub
        for part in range(3):
            cb = cb0 + part * ngrp * H + g * H
            in_specs.append(pl.BlockSpec((unit, LANES), lambda b, u, h, cb=cb: (b * nu + u, cb + h)))
            args.append(y)
        for part in (1, 2):
            cb = cb0 + part * ngrp * H + g * H
            in_specs.append(pl.BlockSpec(
                (sub, LANES),
                lambda b, u, h, cb=cb, per=per: (jnp.maximum(b * nu * per + u * per - 1, 0), cb + h)))
            args.append(y)
    return pl.pallas_call(
        functools.partial(_dil_kernel, unit=unit),
        grid=(batch, nu, H),
        in_specs=in_specs,
        out_specs=pl.BlockSpec((unit, LANES), lambda b, u, h: (b * nu + u, h)),
        out_shape=jax.ShapeDtypeStruct((batch * seq, H * LANES), BF16),
        scratch_shapes=[pltpu.VMEM((ngrp, unit, LANES), F32), pltpu.VMEM((ngrp, unit, LANES), F32)],
        compiler_params=_cparams("parallel", "parallel", "arbitrary"),
    )(*args)


I_LANE, F_LANE = 64, 68


def _mlstm_kernel(qk_ref, v_ref, co_ref, gt_ref, cw_ref, cb_ref, gb_ref, ng_ref, o_ref,
                  c_ref, m_ref, tail_ref, xbuf_ref, *, tile):
    L, H, DK, DV = MLSTM_CHUNK, MLSTM_HEADS, MLSTM_DQK, MLSTM_DV
    KC = MLSTM_CONV
    PAD = 8

    @pl.when(pl.program_id(1) == 0)
    def _():
        c_ref[...] = jnp.zeros_like(c_ref)
        m_ref[...] = jnp.zeros_like(m_ref)
        tail_ref[...] = jnp.zeros_like(tail_ref)

    xbuf_ref[0:PAD, :] = tail_ref[...]
    xbuf_ref[PAD:PAD + tile, :] = qk_ref[...]
    tail_ref[...] = qk_ref[tile - PAD:tile, :]
    acc = cb_ref[...] + cw_ref[KC - 1:KC, :] * xbuf_ref[PAD:PAD + tile, :]
    for j in range(KC - 1):
        off = PAD - (KC - 1) + j
        acc = acc + cw_ref[j:j + 1, :] * xbuf_ref[off:off + tile, :]
    qk = _silu(acc)

    gates = gt_ref[...] + gb_ref[...]
    lane = lax.broadcasted_iota(jnp.int32, gates.shape, 1)
    z = jnp.where(lane >= F_LANE, _log_sigmoid(gates), gates)
    tril = _tril(L)
    tril_b = tril.astype(BF16)
    nchunk = tile // L
    lane_c = lax.broadcasted_iota(jnp.int32, (L, LANES), 1)
    zs, bs, wts = [], [], []
    for c in range(nchunk):
        zc = z[c * L:(c + 1) * L]
        bc = _chunk_cumsum(tril_b, zc)
        zs.append(zc)
        bs.append(bc)
        wts.append(jnp.where(lane_c >= F_LANE, bc, zc).T)
    ones_col = (lane_c == 0).astype(F32)

    for h in range(H):
        q = qk[:, h * DK:(h + 1) * DK]
        k = qk[:, (H + h) * DK:(H + h + 1) * DK] * DK ** -0.5
        v = v_ref[:, h * DV:(h + 1) * DV]
        ct = c_ref[h]
        m = m_ref[h:h + 1, 0:1]
        outs = []
        for c in range(nchunk):
            rows = slice(c * L, (c + 1) * L)
            b_col = bs[c][:, F_LANE + h:F_LANE + h + 1]
            li_col = zs[c][:, I_LANE + h:I_LANE + h + 1]
            b_row = wts[c][F_LANE + h:F_LANE + h + 1, :]
            li_row = wts[c][I_LANE + h:I_LANE + h + 1, :]
            b_last = bs[c][L - 1:L, F_LANE + h:F_LANE + h + 1]
            qc, kc, vc = q[rows], k[rows], v[rows]
            d_log = jnp.where(tril, b_col - b_row + li_row, NEG_INF)
            inter_log = b_col + m
            m_t = jnp.maximum(inter_log, jnp.max(d_log, axis=-1, keepdims=True))
            qkm = _bdot_nt(qc, kc) * jnp.exp(d_log - m_t)
            inter_scale = jnp.exp(inter_log - m_t)
            inter = _bdot_nt(qc, ct)
            num = _bdot(qkm, vc) + inter_scale * inter[:, :DV]
            nq = jnp.sum(qkm, axis=-1, keepdims=True) + inter_scale * inter[:, DV:DV + 1]
            den = jnp.maximum(jnp.abs(nq), jnp.exp(-m_t))
            outs.append(num / den)
            a_col = b_last - b_col + li_col
            m_loc = jnp.max(a_col, axis=0, keepdims=True)
            kw = kc * jnp.exp(a_col - m_loc)
            v_aug = jnp.concatenate([vc, ones_col], axis=1)
            c_loc = _bdot_tn(v_aug, kw)
            m_new = jnp.maximum(b_last + m, m_loc)
            ct = ct * jnp.exp(b_last + m - m_new) + c_loc * jnp.exp(m_loc - m_new)
            m = m_new
        c_ref[h] = ct
        m_ref[h:h + 1, :] = jnp.broadcast_to(m, (1, LANES))
        hcat = jnp.concatenate(outs, axis=0)
        mu = jnp.mean(hcat, axis=-1, keepdims=True)
        hc = hcat - mu
        var = jnp.mean(hc * hc, axis=-1, keepdims=True)
        hn = hc * lax.rsqrt(var + 1e-5) * ng_ref[...]
        o_ref[:, h * DV:(h + 1) * DV] = (_sigmoid(co_ref[:, h * DV:(h + 1) * DV]) * hn).astype(o_ref.dtype)


def _mlstm(y, gates, conv_w, conv_b, gate_bias, ng, *, batch, seq, tile):
    nt = seq // tile
    H, DK, DV = MLSTM_HEADS, MLSTM_DQK, MLSTM_DV
    qk_w, v_w = 2 * H * DK, H * DV
    row = lambda b, t: b * nt + t
    return pl.pallas_call(
        functools.partial(_mlstm_kernel, tile=tile),
        grid=(batch, nt),
        in_specs=[pl.BlockSpec((tile, qk_w), lambda b, t: (row(b, t), 0)),
                  pl.BlockSpec((tile, v_w), lambda b, t: (row(b, t), 1)),
                  pl.BlockSpec((tile, v_w), lambda b, t: (row(b, t), 2)),
                  pl.BlockSpec((tile, LANES), lambda b, t: (row(b, t), 0)),
                  pl.BlockSpec((MLSTM_CONV, qk_w), lambda b, t: (0, 0)),
                  pl.BlockSpec((1, qk_w), lambda b, t: (0, 0)),
                  pl.BlockSpec((1, LANES), lambda b, t: (0, 0)),
                  pl.BlockSpec((1, DV), lambda b, t: (0, 0))],
        out_specs=pl.BlockSpec((tile, v_w), lambda b, t: (row(b, t), 0)),
        out_shape=jax.ShapeDtypeStruct((batch * seq, v_w), BF16),
        scratch_shapes=[pltpu.VMEM((H, DV + LANES, DK), F32),
                        pltpu.VMEM((8, LANES), F32),
                        pltpu.VMEM((8, qk_w), F32),
                        pltpu.VMEM((tile + 8, qk_w), F32)],
        compiler_params=_cparams("parallel", "arbitrary"),
    )(y, y, y, gates, conv_w, conv_b, gate_bias, ng)


def _rope128(x, c_ref, s1_ref, s2_ref):
    return x * c_ref[...] + pltpu.roll(x, 32, 1) * s1_ref[...] + pltpu.roll(x, LANES - 32, 1) * s2_ref[...]


def _rms(x, g):
    return x * lax.rsqrt(jnp.mean(x * x, axis=-1, keepdims=True) + 1e-6) * g


def _mla_q_kernel(x_ref, g_ref, w_ref, c_ref, s1_ref, s2_ref, o_ref):
    q = _bdot(_rms(x_ref[...], g_ref[...]), w_ref[...])
    for h in range(MLA_HEADS):
        o_ref[:, 256 * h:256 * h + 128] = q[:, 256 * h:256 * h + 128].astype(o_ref.dtype)
        o_ref[:, 256 * h + 128:256 * h + 256] = _rope128(
            q[:, 256 * h + 128:256 * h + 256], c_ref, s1_ref, s2_ref).astype(o_ref.dtype)


def _mla_kv_kernel(x_ref, g_ref, wk_ref, wv_ref, kr_ref, c_ref, s1_ref, s2_ref, k_ref, v_ref):
    xn = _rms(x_ref[...], g_ref[...])
    kn = _bdot(xn, wk_ref[...])
    v_ref[...] = _bdot(xn, wv_ref[...]).astype(v_ref.dtype)
    kr = _rope128(kr_ref[...], c_ref, s1_ref, s2_ref).astype(k_ref.dtype)
    for h in range(MLA_HEADS):
        k_ref[:, 256 * h:256 * h + 128] = kn[:, 128 * h:128 * h + 128].astype(k_ref.dtype)
        k_ref[:, 256 * h + 128:256 * h + 256] = kr


def _mla_proj(y, gates, qg, kvg, wq, wk, wv, tabs, *, tm):
    m = y.shape[0]
    H = MLA_HEADS
    qcb = 3072 // MLA_Q_RANK
    tab_specs = [pl.BlockSpec((tm, LANES), lambda i: (i, 0))] * 3
    full = lambda a: pl.BlockSpec(a.shape, lambda i: (0, 0))
    qf = pl.pallas_call(
        _mla_q_kernel,
        grid=(m // tm,),
        in_specs=[pl.BlockSpec((tm, MLA_Q_RANK), lambda i: (i, qcb)), full(qg), full(wq)] + tab_specs,
        out_specs=pl.BlockSpec((tm, H * 256), lambda i: (i, 0)),
        out_shape=jax.ShapeDtypeStruct((m, H * 256), BF16),
        compiler_params=_cparams("parallel"),
    )(y, qg, wq, *tabs)
    kf, vf = pl.pallas_call(
        _mla_kv_kernel,
        grid=(m // tm,),
        in_specs=[pl.BlockSpec((tm, MLA_KV_RANK), lambda i: (i, qcb + 1)), full(kvg), full(wk), full(wv),
                  pl.BlockSpec((tm, LANES), lambda i: (i, 0))] + tab_specs,
        out_specs=[pl.BlockSpec((tm, H * 256), lambda i: (i, 0)),
                   pl.BlockSpec((tm, H * MLA_DV), lambda i: (i, 0))],
        out_shape=[jax.ShapeDtypeStruct((m, H * 256), BF16), jax.ShapeDtypeStruct((m, H * MLA_DV), BF16)],
        compiler_params=_cparams("parallel"),
    )(y, kvg, wk, wv, gates, *tabs)
    return qf, kf, vf


def _mla_attn_kernel(q_ref, k_ref, v_ref, o_ref, *, tq):
    i = pl.program_id(2)
    scale = (MLA_NOPE + MLA_ROPE) ** -0.5
    q = q_ref[...]

    def step(j, carry, masked):
        m, l, acc = carry
        start = pl.multiple_of(j * tq, tq)
        k = k_ref[pl.ds(start, tq), :]
        v = v_ref[pl.ds(start, tq), :]
        s = lax.dot_general(q, k, _NT, preferred_element_type=F32) * scale
        if masked:
            s = jnp.where(_tril(tq), s, NEG_INF)
        m_new = jnp.maximum(m, jnp.max(s, axis=-1, keepdims=True))
        p = jnp.exp(s - m_new)
        a = jnp.exp(m - m_new)
        l = a * l + jnp.sum(p, axis=-1, keepdims=True)
        acc = a * acc + jnp.dot(p.astype(BF16), v, preferred_element_type=F32)
        return m_new, l, acc

    init = (jnp.full((tq, 1), NEG_INF, F32), jnp.zeros((tq, 1), F32), jnp.zeros((tq, MLA_DV), F32))
    carry = lax.fori_loop(0, i, lambda j, c: step(j, c, False), init)
    _, l, acc = step(i, carry, True)
    o_ref[...] = (acc / l).astype(o_ref.dtype)


def _mla_attn(qf, kf, vf, *, batch, seq, tq):
    H = MLA_HEADS
    nq = seq // tq
    return pl.pallas_call(
        functools.partial(_mla_attn_kernel, tq=tq),
        grid=(batch, H, nq),
        in_specs=[pl.BlockSpec((tq, 256), lambda b, h, i: (b * nq + i, h)),
                  pl.BlockSpec((seq, 256), lambda b, h, i: (b, h)),
                  pl.BlockSpec((seq, MLA_DV), lambda b, h, i: (b, h))],
        out_specs=pl.BlockSpec((tq, MLA_DV), lambda b, h, i: (b * nq + i, h)),
        out_shape=jax.ShapeDtypeStruct((batch * seq, H * MLA_DV), BF16),
        compiler_params=_cparams("parallel", "parallel", "arbitrary"),
    )(qf, kf, vf)


def _tile_for(n, pref):
    t = pref
    while n % t:
        t //= 2
    return t


def _even_mixer(x_bf, w_in, wg2, bg, ng, *, batch, seq):
    m = x_bf.shape[0]
    main = jnp.concatenate([w_in[:, :2048], w_in[:, 2048 + GLA_GATE_RANK:]], axis=1).astype(BF16)
    gate = jnp.pad(w_in[:, 2048:2048 + GLA_GATE_RANK], ((0, 0), (0, LANES - GLA_GATE_RANK))).astype(BF16)
    tm = _tile_for(m, 2048)
    y = _matmul(x_bf, main, tm=tm, tn=512, out_dtype=F32)
    gts = _matmul(x_bf, gate, tm=tm, tn=LANES, out_dtype=F32)
    o_a = _gla(y, gts, wg2, bg.reshape(1, -1), ng.reshape(1, -1), batch=batch, seq=seq, tile=_tile_for(seq, 256))
    o_b = _dilated(y, 3072, batch=batch, seq=seq)
    return jnp.concatenate([o_a, o_b], axis=1)


def _rope_tables(positions):
    inv_freq = ROPE_THETA ** (-jnp.arange(0, MLA_ROPE, 2, dtype=F32) / MLA_ROPE)
    ang = positions.astype(F32).reshape(-1, 1) * inv_freq
    cos, sin = jnp.cos(ang), jnp.sin(ang)
    z32 = jnp.zeros_like(cos)
    z64 = jnp.zeros((cos.shape[0], LANES - MLA_ROPE), F32)
    c = jnp.concatenate([cos, cos, z64], axis=1)
    s1 = jnp.concatenate([z32, sin, z64], axis=1)
    s2 = jnp.concatenate([-sin, z32, z64], axis=1)
    return c, s1, s2


def _odd_mixer(x_bf, tabs, w_in, conv_w, conv_b, bi, bf, ng, qg, kvg, wuq, wukv, *, batch, seq):
    m = x_bf.shape[0]
    H = MLA_HEADS
    main = jnp.concatenate([w_in[:, :2048], w_in[:, 2056:4104]], axis=1).astype(BF16)
    gate = jnp.concatenate([w_in[:, 4104:4168], w_in[:, 2048:2056],
                            jnp.zeros((w_in.shape[0], LANES - 72), F32)], axis=1).astype(BF16)
    gate_bias = jnp.concatenate([jnp.zeros((I_LANE,), F32), bi, bf,
                                 jnp.zeros((LANES - F_LANE - MLSTM_HEADS,), F32)]).reshape(1, LANES)
    wq = jnp.pad(wuq.reshape(-1, H, MLA_NOPE + MLA_ROPE),
                 ((0, 0), (0, 0), (0, 256 - MLA_NOPE - MLA_ROPE))).reshape(-1, H * 256).astype(BF16)
    wkv = wukv.reshape(-1, H, MLA_NOPE + MLA_DV)
    wk = wkv[:, :, :MLA_NOPE].reshape(-1, H * MLA_NOPE).astype(BF16)
    wv = wkv[:, :, MLA_NOPE:].reshape(-1, H * MLA_DV).astype(BF16)
    tm = _tile_for(m, 2048)
    y = _matmul(x_bf, main, tm=tm, tn=512, out_dtype=F32)
    gts = _matmul(x_bf, gate, tm=tm, tn=LANES, out_dtype=F32)
    o_c = _mlstm(y, gts, conv_w, conv_b.reshape(1, -1), gate_bias, ng.reshape(1, -1),
                 batch=batch, seq=seq, tile=_tile_for(seq, 256))
    qf, kf, vf = _mla_proj(y, gts, qg.reshape(1, -1), kvg.reshape(1, -1), wq, wk, wv, tabs, tm=_tile_for(m, 512))
    o_d = _mla_attn(qf, kf, vf, batch=batch, seq=seq, tq=_tile_for(seq, 512))
    return jnp.concatenate([o_c, o_d], axis=1)


def kernel(x, positions, even_w_in, even_gla_wg2, even_gla_bg, even_gla_norm_g, even_w_o, odd_w_in, odd_conv_w, odd_conv_b, odd_mlstm_bi, odd_mlstm_bf, odd_mlstm_norm_g, odd_mla_qnorm_g, odd_mla_kvnorm_g, odd_mla_wuq, odd_mla_wukv, odd_w_o, ln1_g, ln1_b, ffn_wgu, ffn_wd, ln2_g, ln2_b):
    batch, seq, d = x.shape
    m = batch * seq
    xf = x.reshape(m, d)
    xb = xf.astype(BF16)
    tabs = _rope_tables(positions)
    depth = ln1_g.shape[0]
    tm_ln = _tile_for(m, 512)
    for l in range(depth):
        j = l // 2
        if l % 2 == 0:
            o = _even_mixer(xb, even_w_in[j], even_gla_wg2[j], even_gla_bg[j], even_gla_norm_g[j],
                            batch=batch, seq=seq)
            w_o = even_w_o[j]
        else:
            o = _odd_mixer(xb, tabs, odd_w_in[j], odd_conv_w[j], odd_conv_b[j], odd_mlstm_bi[j], odd_mlstm_bf[j],
                           odd_mlstm_norm_g[j], odd_mla_qnorm_g[j], odd_mla_kvnorm_g[j], odd_mla_wuq[j],
                           odd_mla_wukv[j], batch=batch, seq=seq)
            w_o = odd_w_o[j]
        xf, xb = _mm_ln(o, w_o.astype(BF16), xf, ln1_g[l].reshape(1, -1), ln1_b[l].reshape(1, -1),
                        tm=tm_ln, tk=512)
        hid = _ffn_up(xb, ffn_wgu[l].astype(BF16), tm=_tile_for(m, 2048), tn=512)
        xf, xb = _mm_ln(hid, ffn_wd[l].astype(BF16), xf, ln2_g[l].reshape(1, -1), ln2_b[l].reshape(1, -1),
                        tm=tm_ln, tk=512)
    return xf.reshape(batch, seq, d)
```

```python
import functools

import jax
import jax.numpy as jnp
from jax import lax
from jax.experimental import pallas as pl
from jax.experimental.pallas import tpu as pltpu

F32 = jnp.float32
BF16 = jnp.bfloat16

D_MODEL = 2048
DEPTH = 4
GLA_HEADS, GLA_DK, GLA_DV, GLA_GATE_RANK, GLA_TAU, GLA_CHUNK = 4, 128, 256, 16, 16.0, 64
DIL_PAIRS = ((128, 1), (512, 4), (2048, 16))
DIL_HEADS, DIL_HEAD_DIM, DIL_BLOCK = 4, 128, 128
MLSTM_HEADS, MLSTM_DQK, MLSTM_DV, MLSTM_CONV, MLSTM_CHUNK = 4, 128, 256, 4, 64
MLA_HEADS, MLA_Q_RANK, MLA_KV_RANK, MLA_NOPE, MLA_ROPE, MLA_DV = 8, 512, 512, 128, 64, 128
ROPE_THETA = 10000.0
FFN_HIDDEN = 5632
DEEPNORM_ALPHA = (2.0 * DEPTH) ** 0.25

LANES = 128
VMEM_LIMIT = 56 * 1024 * 1024
NEG_INF = float("-inf")

_NT = (((1,), (1,)), ((), ()))


def _cparams(*sem):
    return pltpu.CompilerParams(dimension_semantics=sem, vmem_limit_bytes=VMEM_LIMIT)


def _bdot(a, b):
    return jnp.dot(a.astype(BF16), b.astype(BF16), preferred_element_type=F32)


def _bdot_nt(a, b):
    return lax.dot_general(a.astype(BF16), b.astype(BF16), _NT, preferred_element_type=F32)


def _bdot_tn(a, b):
    return jnp.dot(a.astype(F32).T.astype(BF16), b.astype(BF16), preferred_element_type=F32)


def _split2(a):
    hi = a.astype(BF16)
    lo = (a - hi.astype(F32)).astype(BF16)
    return hi, lo


def _dot3(a, b):
    ah, al = _split2(a)
    bh, bl = _split2(b)
    f = functools.partial(jnp.dot, preferred_element_type=F32)
    return f(ah, bh) + (f(ah, bl) + f(al, bh))


def _chunk_cumsum(tril_b, x):
    hi, lo = _split2(x)
    lo2 = (x - hi.astype(F32) - lo.astype(F32)).astype(BF16)
    f = functools.partial(jnp.dot, preferred_element_type=F32)
    return f(tril_b, hi) + (f(tril_b, lo) + f(tril_b, lo2))


def _tril(n):
    r = lax.broadcasted_iota(jnp.int32, (n, n), 0)
    c = lax.broadcasted_iota(jnp.int32, (n, n), 1)
    return c <= r


def _log_sigmoid(x):
    return jnp.minimum(x, 0.0) - jnp.log1p(jnp.exp(-jnp.abs(x)))


def _sigmoid(x):
    return 1.0 / (1.0 + jnp.exp(-x))


def _silu(x):
    return x * _sigmoid(x)


def _mm_kernel(a_ref, w_ref, o_ref):
    o_ref[...] = jnp.dot(a_ref[...], w_ref[...].astype(BF16), preferred_element_type=F32).astype(o_ref.dtype)


def _wspec(w, layer, rows, cols, index_map):
    if w.ndim == 2:
        return pl.BlockSpec((rows, cols), index_map)
    return pl.BlockSpec((None, rows, cols), lambda *ids: (layer,) + tuple(index_map(*ids)))


def _matmul(a, w, *, tm, tn, out_dtype, layer=None, ncols=None):
    m, k = a.shape
    n = w.shape[-1] if ncols is None else ncols
    return pl.pallas_call(
        _mm_kernel,
        grid=(m // tm, n // tn),
        in_specs=[pl.BlockSpec((tm, k), lambda i, j: (i, 0)),
                  _wspec(w, layer, k, tn, lambda i, j: (0, j))],
        out_specs=pl.BlockSpec((tm, tn), lambda i, j: (i, j)),
        out_shape=jax.ShapeDtypeStruct((m, n), out_dtype),
        compiler_params=_cparams("parallel", "arbitrary"),
        name="mm",
    )(a, w)


def _ffn_up_kernel(x_ref, wg_ref, wu_ref, o_ref):
    x = x_ref[...]
    g = jnp.dot(x, wg_ref[...].astype(BF16), preferred_element_type=F32)
    u = jnp.dot(x, wu_ref[...].astype(BF16), preferred_element_type=F32)
    o_ref[...] = (_silu(g) * u).astype(o_ref.dtype)


def _ffn_up(x_bf, w_gu, *, tm, tn, layer=None):
    m, k = x_bf.shape
    hidden = w_gu.shape[-1] // 2
    nb = hidden // tn
    return pl.pallas_call(
        _ffn_up_kernel,
        grid=(m // tm, nb),
        in_specs=[pl.BlockSpec((tm, k), lambda i, j: (i, 0)),
                  _wspec(w_gu, layer, k, tn, lambda i, j: (0, j)),
                  _wspec(w_gu, layer, k, tn, lambda i, j: (0, j + nb))],
        out_specs=pl.BlockSpec((tm, tn), lambda i, j: (i, j)),
        out_shape=jax.ShapeDtypeStruct((m, hidden), BF16),
        compiler_params=_cparams("parallel", "arbitrary"),
        name="ffn_up",
    )(x_bf, w_gu, w_gu)


def _mm_ln_kernel(*refs, n1, nk):
    a_refs, (w_ref, res_ref, g_ref, b_ref, of_ref, ob_ref) = refs[:-6], refs[-6:]
    k = pl.program_id(1)
    w = w_ref[...].astype(BF16)

    @pl.when(k == 0)
    def _():
        of_ref[...] = DEEPNORM_ALPHA * res_ref[...]

    if len(a_refs) == 1:
        of_ref[...] += jnp.dot(a_refs[0][...], w, preferred_element_type=F32)
    else:
        @pl.when(k < n1)
        def _():
            of_ref[...] += jnp.dot(a_refs[0][...], w, preferred_element_type=F32)

        @pl.when(k >= n1)
        def _():
            of_ref[...] += jnp.dot(a_refs[1][...], w, preferred_element_type=F32)

    @pl.when(k == nk - 1)
    def _():
        y = of_ref[...]
        mu = jnp.mean(y, axis=-1, keepdims=True)
        yc = y - mu
        var = jnp.mean(yc * yc, axis=-1, keepdims=True)
        out = yc * lax.rsqrt(var + 1e-5) * g_ref[...] + b_ref[...]
        of_ref[...] = out
        ob_ref[...] = out.astype(BF16)


def _mm_ln(parts, w, res, g, b, *, tm, tk, layer=None):
    m = res.shape[0]
    n = w.shape[-1]
    n1 = parts[0].shape[1] // tk
    nk = sum(p.shape[1] for p in parts) // tk
    assert len(parts) <= 2 and all(p.shape[1] % tk == 0 for p in parts)
    a_specs = [pl.BlockSpec((tm, tk), lambda i, k: (i, jnp.minimum(k, n1 - 1)))]
    if len(parts) == 2:
        a_specs.append(pl.BlockSpec((tm, tk), lambda i, k: (i, jnp.maximum(k - n1, 0))))
    return pl.pallas_call(
        functools.partial(_mm_ln_kernel, n1=n1, nk=nk),
        grid=(m // tm, nk),
        in_specs=a_specs + [
            _wspec(w, layer, tk, n, lambda i, k: (k, 0)),
            pl.BlockSpec((tm, n), lambda i, k: (i, 0), pipeline_mode=pl.Buffered(1)),
            pl.BlockSpec((1, n), lambda i, k: (0, 0)),
            pl.BlockSpec((1, n), lambda i, k: (0, 0))],
        out_specs=[pl.BlockSpec((tm, n), lambda i, k: (i, 0)),
                   pl.BlockSpec((tm, n), lambda i, k: (i, 0))],
        out_shape=[jax.ShapeDtypeStruct((m, n), F32), jax.ShapeDtypeStruct((m, n), BF16)],
        compiler_params=_cparams("parallel", "arbitrary"),
        name="mm_ln",
    )(*parts, w, res, g, b)


def _gla_kernel(q_ref, k_ref, v_ref, r_ref, g_ref, wg2_ref, bg_ref, ng_ref, o_ref, st_ref, *, tile):
    L, H, DK, DV = GLA_CHUNK, GLA_HEADS, GLA_DK, GLA_DV

    @pl.when(pl.program_id(1) == 0)
    def _():
        st_ref[...] = jnp.zeros_like(st_ref)

    tril = _tril(L)
    tril_b = tril.astype(BF16)
    g16 = g_ref[:, :GLA_GATE_RANK]
    for h in range(H):
        ks = slice(h * DK, (h + 1) * DK)
        vs = slice(h * DV, (h + 1) * DV)
        log_a = _log_sigmoid(_dot3(g16, wg2_ref[:, ks]) + bg_ref[:, ks]) / GLA_TAU
        q = q_ref[:, ks] * DK ** -0.5
        k = k_ref[:, ks]
        v = v_ref[:, vs]
        st = st_ref[h]
        outs = []
        for c in range(tile // L):
            rows = slice(c * L, (c + 1) * L)
            b = _chunk_cumsum(tril_b, log_a[rows])
            b_last = b[L - 1:L]
            q_dec = (q[rows] * jnp.exp(b)).astype(BF16)
            k_inv = k[rows] * jnp.exp(-b)
            k_end = k[rows] * jnp.exp(b_last - b)
            vb = v[rows]
            scores = jnp.where(tril, _bdot_nt(q_dec, k_inv), 0.0)
            outs.append(_bdot(scores, vb) + _bdot_nt(q_dec, st))
            st = st * jnp.exp(b_last) + _bdot_tn(vb, k_end)
        st_ref[h] = st
        o = jnp.concatenate(outs, axis=0)
        o = o * lax.rsqrt(jnp.mean(o * o, axis=-1, keepdims=True) + 1e-6) * ng_ref[...]
        o_ref[:, vs] = (o * _silu(r_ref[:, vs])).astype(o_ref.dtype)


def _gla(y1, y2, gates, wg2, bg, ng, *, batch, seq, tile):
    nt = seq // tile
    qk_w, v_w = GLA_HEADS * GLA_DK, GLA_HEADS * GLA_DV
    row = lambda b, t: b * nt + t
    return pl.pallas_call(
        functools.partial(_gla_kernel, tile=tile),
        grid=(batch, nt),
        in_specs=[pl.BlockSpec((tile, qk_w), lambda b, t: (row(b, t), 0)),
                  pl.BlockSpec((tile, qk_w), lambda b, t: (row(b, t), 1)),
                  pl.BlockSpec((tile, v_w), lambda b, t: (row(b, t), 1)),
                  pl.BlockSpec((tile, v_w), lambda b, t: (row(b, t), 0)),
                  pl.BlockSpec((tile, LANES), lambda b, t: (row(b, t), 0)),
                  pl.BlockSpec((GLA_GATE_RANK, qk_w), lambda b, t: (0, 0)),
                  pl.BlockSpec((1, qk_w), lambda b, t: (0, 0)),
                  pl.BlockSpec((1, GLA_DV), lambda b, t: (0, 0))],
        out_specs=pl.BlockSpec((tile, v_w), lambda b, t: (row(b, t), 0)),
        out_shape=jax.ShapeDtypeStruct((batch * seq, v_w), BF16),
        scratch_shapes=[pltpu.VMEM((GLA_HEADS, GLA_DV, GLA_DK), F32)],
        compiler_params=_cparams("parallel", "arbitrary"),
        name="gla",
    )(y1, y1, y1, y2, gates, wg2, bg, ng)


LOG2E = 1.4426950408889634
LN2 = 0.6931471805599453


def _dil_kernel(*refs, unit):
    ngrp = len(DIL_PAIRS)
    in_refs, o_ref, og_ref, lg_ref = refs[:5 * ngrp], refs[5 * ngrp], refs[5 * ngrp + 1], refs[5 * ngrp + 2]
    u = pl.program_id(1)
    n = DIL_BLOCK
    qscale = DIL_HEAD_DIM ** -0.5 * LOG2E
    row = lax.broadcasted_iota(jnp.int32, (n, n), 0)
    col = lax.broadcasted_iota(jnp.int32, (n, n), 1)
    prev_band, cur_band = col >= row, col <= row
    ones = jnp.ones((n, LANES), BF16)
    for g, (_, d) in enumerate(DIL_PAIRS):
        q_ref, k_ref, v_ref, kp_ref, vp_ref = in_refs[5 * g:5 * g + 5]
        sub = n * d
        blocks = []
        for s in range(unit // sub):
            for r in range(d):
                cur = pl.ds(s * sub + r, n, stride=d) if d > 1 else pl.ds(s * sub, n)
                if s == 0:
                    prv = pl.ds(r, n, stride=d) if d > 1 else pl.ds(0, n)
                    blocks.append((cur, kp_ref, vp_ref, prv, prev_band & (u > 0)))
                else:
                    prv = pl.ds((s - 1) * sub + r, n, stride=d) if d > 1 else pl.ds((s - 1) * sub, n)
                    blocks.append((cur, k_ref, v_ref, prv, prev_band))
        scores = []
        for cur, kpr, _, prv, pmask in blocks:
            q = (q_ref[cur, :] * qscale).astype(BF16)
            scores.append((jnp.where(pmask, _bdot_nt(q, kpr[prv, :]), NEG_INF),
                           jnp.where(cur_band, _bdot_nt(q, k_ref[cur, :]), NEG_INF)))
        probs = []
        for s_p, s_c in scores:
            m = jnp.maximum(jnp.max(s_p, axis=-1, keepdims=True), jnp.max(s_c, axis=-1, keepdims=True))
            probs.append((jnp.exp2(s_p - m).astype(BF16), jnp.exp2(s_c - m).astype(BF16), m))
        for (cur, _, vpr, prv, _), (p_p, p_c, m) in zip(blocks, probs):
            den = _bdot(p_p, ones) + _bdot(p_c, ones)
            og_ref[g, cur, :] = (_bdot(p_p, vpr[prv, :]) + _bdot(p_c, v_ref[cur, :])) / den
            lg_ref[g, cur, :] = m * LN2 + jnp.log(den)
    lses = [lg_ref[g] for g in range(ngrp)]
    m = functools.reduce(jnp.maximum, lses)
    ws = [jnp.exp(l - m) for l in lses]
    tot = functools.reduce(lambda a, b: a + b, ws)
    acc = sum(w * og_ref[g] for g, w in enumerate(ws))
    o_ref[...] = (acc / tot).astype(o_ref.dtype)


def _dilated(y, col0, *, batch, seq):
    ngrp, H, n = len(DIL_PAIRS), DIL_HEADS, DIL_BLOCK
    unit = max(d for _, d in DIL_PAIRS) * n
    assert seq % unit == 0
    nu = seq // unit
    cb0 = col0 // LANES
    in_specs, args = [], []
    for g, (_, d) in enumerate(DIL_PAIRS):
        sub = n * d
        per = unit // sub
        for part in range(3):
            cb = cb0 + part * ngrp * H + g * H
            in_specs.append(pl.BlockSpec((unit, LANES), lambda b, u, h, cb=cb: (b * nu + u, cb + h)))
            args.append(y)
        for part in (1, 2):
            cb = cb0 + part * ngrp * H + g * H
            in_specs.append(pl.BlockSpec(
                (sub, LANES),
                lambda b, u, h, cb=cb, per=per: (jnp.maximum(b * nu * per + u * per - 1, 0), cb + h)))
            args.append(y)
    return pl.pallas_call(
        functools.partial(_dil_kernel, unit=unit),
        grid=(batch, nu, H),
        in_specs=in_specs,
        out_specs=pl.BlockSpec((unit, LANES), lambda b, u, h: (b * nu + u, h)),
        out_shape=jax.ShapeDtypeStruct((batch * seq, H * LANES), BF16),
        scratch_shapes=[pltpu.VMEM((ngrp, unit, LANES), F32), pltpu.VMEM((ngrp, unit, LANES), F32)],
        compiler_params=_cparams("parallel", "parallel", "arbitrary"),
        name="dilated",
    )(*args)


I_LANE, F_LANE = 64, 68


def _mlstm_kernel(qk_ref, v_ref, co_ref, gt_ref, cw_ref, cb_ref, gb_ref, ng_ref, o_ref,
                  c_ref, m_ref, tail_ref, xbuf_ref, *, tile):
    L, H, DK, DV = MLSTM_CHUNK, MLSTM_HEADS, MLSTM_DQK, MLSTM_DV
    KC = MLSTM_CONV
    PAD = 8

    @pl.when(pl.program_id(1) == 0)
    def _():
        c_ref[...] = jnp.zeros_like(c_ref)
        m_ref[...] = jnp.zeros_like(m_ref)
        tail_ref[...] = jnp.zeros_like(tail_ref)

    xbuf_ref[0:PAD, :] = tail_ref[...]
    xbuf_ref[PAD:PAD + tile, :] = qk_ref[...]
    tail_ref[...] = qk_ref[tile - PAD:tile, :]
    acc = cb_ref[...] + cw_ref[KC - 1:KC, :] * xbuf_ref[PAD:PAD + tile, :]
    for j in range(KC - 1):
        off = PAD - (KC - 1) + j
        acc = acc + cw_ref[j:j + 1, :] * xbuf_ref[off:off + tile, :]
    qk = _silu(acc)

    gates = gt_ref[...] + gb_ref[...]
    lane = lax.broadcasted_iota(jnp.int32, gates.shape, 1)
    z = jnp.where(lane >= F_LANE, _log_sigmoid(gates), gates)
    tril = _tril(L)
    tril_b = tril.astype(BF16)
    nchunk = tile // L
    lane_c = lax.broadcasted_iota(jnp.int32, (L, LANES), 1)
    zs, bs, wts = [], [], []
    for c in range(nchunk):
        zc = z[c * L:(c + 1) * L]
        bc = _chunk_cumsum(tril_b, zc)
        zs.append(zc)
        bs.append(bc)
        wts.append(jnp.where(lane_c >= F_LANE, bc, zc).T)
    ones_col = (lane_c == 0).astype(F32)

    for h in range(H):
        q = qk[:, h * DK:(h + 1) * DK]
        k = qk[:, (H + h) * DK:(H + h + 1) * DK] * DK ** -0.5
        v = v_ref[:, h * DV:(h + 1) * DV]
        ct = c_ref[h]
        m = m_ref[h:h + 1, 0:1]
        outs = []
        for c in range(nchunk):
            rows = slice(c * L, (c + 1) * L)
            b_col = bs[c][:, F_LANE + h:F_LANE + h + 1]
            li_col = zs[c][:, I_LANE + h:I_LANE + h + 1]
            b_row = wts[c][F_LANE + h:F_LANE + h + 1, :]
            li_row = wts[c][I_LANE + h:I_LANE + h + 1, :]
            b_last = bs[c][L - 1:L, F_LANE + h:F_LANE + h + 1]
            qc, kc, vc = q[rows], k[rows], v[rows]
            d_log = jnp.where(tril, b_col - b_row + li_row, NEG_INF)
            inter_log = b_col + m
            m_t = jnp.maximum(inter_log, jnp.max(d_log, axis=-1, keepdims=True))
            qkm = _bdot_nt(qc, kc) * jnp.exp(d_log - m_t)
            inter_scale = jnp.exp(inter_log - m_t)
            inter = _bdot_nt(qc, ct)
            num = _bdot(qkm, vc) + inter_scale * inter[:, :DV]
            nq = jnp.sum(qkm, axis=-1, keepdims=True) + inter_scale * inter[:, DV:DV + 1]
            den = jnp.maximum(jnp.abs(nq), jnp.exp(-m_t))
            outs.append(num / den)
            a_col = b_last - b_col + li_col
            m_loc = jnp.max(a_col, axis=0, keepdims=True)
            kw = kc * jnp.exp(a_col - m_loc)
            v_aug = jnp.concatenate([vc, ones_col], axis=1)
            c_loc = _bdot_tn(v_aug, kw)
            m_new = jnp.maximum(b_last + m, m_loc)
            ct = ct * jnp.exp(b_last + m - m_new) + c_loc * jnp.exp(m_loc - m_new)
            m = m_new
        c_ref[h] = ct
        m_ref[h:h + 1, :] = jnp.broadcast_to(m, (1, LANES))
        hcat = jnp.concatenate(outs, axis=0)
        mu = jnp.mean(hcat, axis=-1, keepdims=True)
        hc = hcat - mu
        var = jnp.mean(hc * hc, axis=-1, keepdims=True)
        hn = hc * lax.rsqrt(var + 1e-5) * ng_ref[...]
        o_ref[:, h * DV:(h + 1) * DV] = (_sigmoid(co_ref[:, h * DV:(h + 1) * DV]) * hn).astype(o_ref.dtype)


def _mlstm(y1, y2, gates, conv_w, conv_b, gate_bias, ng, *, batch, seq, tile):
    nt = seq // tile
    H, DK, DV = MLSTM_HEADS, MLSTM_DQK, MLSTM_DV
    qk_w, v_w = 2 * H * DK, H * DV
    row = lambda b, t: b * nt + t
    return pl.pallas_call(
        functools.partial(_mlstm_kernel, tile=tile),
        grid=(batch, nt),
        in_specs=[pl.BlockSpec((tile, qk_w), lambda b, t: (row(b, t), 0)),
                  pl.BlockSpec((tile, v_w), lambda b, t: (row(b, t), 1)),
                  pl.BlockSpec((tile, v_w), lambda b, t: (row(b, t), 0)),
                  pl.BlockSpec((tile, LANES), lambda b, t: (row(b, t), 0)),
                  pl.BlockSpec((MLSTM_CONV, qk_w), lambda b, t: (0, 0)),
                  pl.BlockSpec((1, qk_w), lambda b, t: (0, 0)),
                  pl.BlockSpec((1, LANES), lambda b, t: (0, 0)),
                  pl.BlockSpec((1, DV), lambda b, t: (0, 0))],
        out_specs=pl.BlockSpec((tile, v_w), lambda b, t: (row(b, t), 0)),
        out_shape=jax.ShapeDtypeStruct((batch * seq, v_w), BF16),
        scratch_shapes=[pltpu.VMEM((H, DV + LANES, DK), F32),
                        pltpu.VMEM((8, LANES), F32),
                        pltpu.VMEM((8, qk_w), F32),
                        pltpu.VMEM((tile + 8, qk_w), F32)],
        compiler_params=_cparams("parallel", "arbitrary"),
        name="mlstm",
    )(y1, y1, y2, gates, conv_w, conv_b, gate_bias, ng)


def _rope128(x, c_ref, s1_ref, s2_ref):
    return x * c_ref[...] + pltpu.roll(x, 32, 1) * s1_ref[...] + pltpu.roll(x, LANES - 32, 1) * s2_ref[...]


def _rms(x, g):
    return x * lax.rsqrt(jnp.mean(x * x, axis=-1, keepdims=True) + 1e-6) * g


MLA_SCORE_SCALE = (MLA_NOPE + MLA_ROPE) ** -0.5 * 1.4426950408889634


def _mla_q_kernel(x_ref, g_ref, w_ref, c_ref, s1_ref, s2_ref, o_ref):
    q = _bdot(_rms(x_ref[...], g_ref[...]), w_ref[...]) * MLA_SCORE_SCALE
    for h in range(MLA_HEADS):
        o_ref[:, 256 * h:256 * h + 128] = q[:, 256 * h:256 * h + 128].astype(o_ref.dtype)
        o_ref[:, 256 * h + 128:256 * h + 256] = _rope128(
            q[:, 256 * h + 128:256 * h + 256], c_ref, s1_ref, s2_ref).astype(o_ref.dtype)


def _mla_kv_kernel(x_ref, g_ref, wk_ref, wvt_ref, kr_ref, c_ref, s1_ref, s2_ref, k_ref, vt_ref):
    xn = _rms(x_ref[...], g_ref[...]).astype(BF16)
    kn = jnp.dot(xn, wk_ref[...], preferred_element_type=F32)
    vt_ref[...] = lax.dot_general(wvt_ref[...], xn, _NT, preferred_element_type=F32).astype(vt_ref.dtype)
    kr = _rope128(kr_ref[...], c_ref, s1_ref, s2_ref).astype(k_ref.dtype)
    for h in range(MLA_HEADS):
        k_ref[:, 256 * h:256 * h + 128] = kn[:, 128 * h:128 * h + 128].astype(k_ref.dtype)
        k_ref[:, 256 * h + 128:256 * h + 256] = kr


def _mla_proj(y, qcb, gates, qg, kvg, wq, wk, wvt, tabs, *, tm):
    m = y.shape[0]
    H = MLA_HEADS
    tab_specs = [pl.BlockSpec((tm, LANES), lambda i: (i, 0))] * 3
    full = lambda a: pl.BlockSpec(a.shape, lambda i: (0, 0))
    qf = pl.pallas_call(
        _mla_q_kernel,
        grid=(m // tm,),
        in_specs=[pl.BlockSpec((tm, MLA_Q_RANK), lambda i: (i, qcb)), full(qg), full(wq)] + tab_specs,
        out_specs=pl.BlockSpec((tm, H * 256), lambda i: (i, 0)),
        out_shape=jax.ShapeDtypeStruct((m, H * 256), BF16),
        compiler_params=_cparams("parallel"),
        name="mla_q",
    )(y, qg, wq, *tabs)
    kf, vt = pl.pallas_call(
        _mla_kv_kernel,
        grid=(m // tm,),
        in_specs=[pl.BlockSpec((tm, MLA_KV_RANK), lambda i: (i, qcb + 1)), full(kvg), full(wk), full(wvt),
                  pl.BlockSpec((tm, LANES), lambda i: (i, 0))] + tab_specs,
        out_specs=[pl.BlockSpec((tm, H * 256), lambda i: (i, 0)),
                   pl.BlockSpec((H * MLA_DV, tm), lambda i: (0, i))],
        out_shape=[jax.ShapeDtypeStruct((m, H * 256), BF16), jax.ShapeDtypeStruct((H * MLA_DV, m), BF16)],
        compiler_params=_cparams("parallel"),
        name="mla_kv",
    )(y, kvg, wk, wvt, gates, *tabs)
    return qf, kf, vt


def _mla_attn_kernel(q_ref, k_ref, vt_ref, o_ref, *, tq, hg):
    i = pl.program_id(2)
    DV = MLA_DV

    def step(j, carry, masked):
        start = pl.multiple_of(j * tq, tq)
        out = []
        sts = [lax.dot_general(k_ref[pl.ds(start, tq), 256 * g:256 * (g + 1)], q_ref[:, 256 * g:256 * (g + 1)],
                               _NT, preferred_element_type=F32) for g in range(hg)]
        for g in range(hg):
            m, l, acc = carry[g]
            vtb = vt_ref[DV * g:DV * (g + 1), pl.ds(start, tq)]
            st = sts[g]
            if masked:
                kk = lax.broadcasted_iota(jnp.int32, (tq, tq), 0)
                qq = lax.broadcasted_iota(jnp.int32, (tq, tq), 1)
                st = jnp.where(kk <= qq, st, NEG_INF)
            m_new = jnp.maximum(m, jnp.max(st, axis=0, keepdims=True))
            pt = jnp.exp2(st - m_new)
            a = jnp.exp2(m - m_new)
            l = a * l + jnp.sum(pt, axis=0, keepdims=True)
            acc = a * acc + jnp.dot(vtb, pt.astype(BF16), preferred_element_type=F32)
            out.append((m_new, l, acc))
        return tuple(out)

    init = tuple((jnp.full((1, tq), NEG_INF, F32), jnp.zeros((1, tq), F32), jnp.zeros((DV, tq), F32))
                 for _ in range(hg))
    carry = lax.fori_loop(0, i, lambda j, cr: step(j, cr, False), init)
    carry = step(i, carry, True)
    for g in range(hg):
        _, l, acc = carry[g]
        o_ref[:, DV * g:DV * (g + 1)] = (acc / l).T.astype(o_ref.dtype)


def _mla_attn(qf, kf, vt, *, batch, seq, tq, hg=4):
    H = MLA_HEADS
    nq = seq // tq
    return pl.pallas_call(
        functools.partial(_mla_attn_kernel, tq=tq, hg=hg),
        grid=(batch, H // hg, nq),
        in_specs=[pl.BlockSpec((tq, 256 * hg), lambda b, h, i: (b * nq + i, h)),
                  pl.BlockSpec((seq, 256 * hg), lambda b, h, i: (b, h)),
                  pl.BlockSpec((MLA_DV * hg, seq), lambda b, h, i: (h, b))],
        out_specs=pl.BlockSpec((tq, MLA_DV * hg), lambda b, h, i: (b * nq + i, h)),
        out_shape=jax.ShapeDtypeStruct((batch * seq, H * MLA_DV), BF16),
        compiler_params=_cparams("parallel", "parallel", "arbitrary"),
        name="mla_attn",
    )(qf, kf, vt)


def _tile_for(n, pref):
    t = pref
    while n % t:
        t //= 2
    return t


def _even_mixer(x_bf, w_in_all, j, wg2, bg, ng, *, batch, seq):
    m = x_bf.shape[0]
    gq_gk_gv = 2 * GLA_HEADS * GLA_DK + GLA_HEADS * GLA_DV
    rest = w_in_all[j, :, gq_gk_gv + GLA_GATE_RANK:].astype(BF16)
    gate = jnp.pad(w_in_all[j, :, gq_gk_gv:gq_gk_gv + GLA_GATE_RANK],
                   ((0, 0), (0, LANES - GLA_GATE_RANK))).astype(BF16)
    tm = _tile_for(m, 2048)
    y1 = _matmul(x_bf, w_in_all, layer=j, ncols=gq_gk_gv, tm=tm, tn=512, out_dtype=F32)
    y2 = _matmul(x_bf, rest, tm=tm, tn=512, out_dtype=F32)
    gts = _matmul(x_bf, gate, tm=tm, tn=LANES, out_dtype=F32)
    o_a = _gla(y1, y2, gts, wg2, bg.reshape(1, -1), ng.reshape(1, -1), batch=batch, seq=seq,
               tile=_tile_for(seq, 256))
    o_b = _dilated(y2, GLA_HEADS * GLA_DV, batch=batch, seq=seq)
    return o_a, o_b


def _rope_tables(positions):
    inv_freq = ROPE_THETA ** (-jnp.arange(0, MLA_ROPE, 2, dtype=F32) / MLA_ROPE)
    ang = positions.astype(F32).reshape(-1, 1) * inv_freq
    cos, sin = jnp.cos(ang), jnp.sin(ang)
    z32 = jnp.zeros_like(cos)
    z64 = jnp.zeros((cos.shape[0], LANES - MLA_ROPE), F32)
    c = jnp.concatenate([cos, cos, z64], axis=1)
    s1 = jnp.concatenate([z32, sin, z64], axis=1)
    s2 = jnp.concatenate([-sin, z32, z64], axis=1)
    return c, s1, s2


def _odd_mixer(x_bf, tabs, w_in_all, j, conv_w, conv_b, bi, bf, ng, qg, kvg, wuq, wukv, *, batch, seq):
    m = x_bf.shape[0]
    H = MLA_HEADS
    cq_ck_cv = 2 * MLSTM_HEADS * MLSTM_DQK + MLSTM_HEADS * MLSTM_DV
    co0 = cq_ck_cv + 2 * MLSTM_HEADS
    kr0 = co0 + MLSTM_HEADS * MLSTM_DV + MLA_Q_RANK + MLA_KV_RANK
    w_in = w_in_all[j]
    rest = w_in[:, co0:kr0].astype(BF16)
    gate = jnp.concatenate([w_in[:, kr0:kr0 + MLA_ROPE], w_in[:, cq_ck_cv:co0],
                            jnp.zeros((w_in.shape[0], LANES - MLA_ROPE - 2 * MLSTM_HEADS), F32)],
                           axis=1).astype(BF16)
    gate_bias = jnp.concatenate([jnp.zeros((I_LANE,), F32), bi, bf,
                                 jnp.zeros((LANES - F_LANE - MLSTM_HEADS,), F32)]).reshape(1, LANES)
    wq = jnp.pad(wuq.reshape(-1, H, MLA_NOPE + MLA_ROPE),
                 ((0, 0), (0, 0), (0, 256 - MLA_NOPE - MLA_ROPE))).reshape(-1, H * 256).astype(BF16)
    wkv = wukv.reshape(-1, H, MLA_NOPE + MLA_DV)
    wk = wkv[:, :, :MLA_NOPE].reshape(-1, H * MLA_NOPE).astype(BF16)
    wvt = wkv[:, :, MLA_NOPE:].reshape(-1, H * MLA_DV).T.astype(BF16)
    tm = _tile_for(m, 2048)
    y1 = _matmul(x_bf, w_in_all, layer=j, ncols=cq_ck_cv, tm=tm, tn=512, out_dtype=F32)
    y2 = _matmul(x_bf, rest, tm=tm, tn=512, out_dtype=F32)
    gts = _matmul(x_bf, gate, tm=tm, tn=LANES, out_dtype=F32)
    o_c = _mlstm(y1, y2, gts, conv_w, conv_b.reshape(1, -1), gate_bias, ng.reshape(1, -1),
                 batch=batch, seq=seq, tile=_tile_for(seq, 256))
    qf, kf, vt = _mla_proj(y2, MLSTM_HEADS * MLSTM_DV // MLA_Q_RANK, gts, qg.reshape(1, -1), kvg.reshape(1, -1),
                           wq, wk, wvt, tabs, tm=_tile_for(m, 512))
    o_d = _mla_attn(qf, kf, vt, batch=batch, seq=seq, tq=_tile_for(seq, 512))
    return o_c, o_d


def kernel(x, positions, even_w_in, even_gla_wg2, even_gla_bg, even_gla_norm_g, even_w_o, odd_w_in, odd_conv_w, odd_conv_b, odd_mlstm_bi, odd_mlstm_bf, odd_mlstm_norm_g, odd_mla_qnorm_g, odd_mla_kvnorm_g, odd_mla_wuq, odd_mla_wukv, odd_w_o, ln1_g, ln1_b, ffn_wgu, ffn_wd, ln2_g, ln2_b):
    batch, seq, d = x.shape
    m = batch * seq
    xf = x.reshape(m, d)
    xb = xf.astype(BF16)
    tabs = _rope_tables(positions)
    depth = ln1_g.shape[0]
    tm_ln = _tile_for(m, 1024)
    for l in range(depth):
        j = l // 2
        if l % 2 == 0:
            parts = _even_mixer(xb, even_w_in, j, even_gla_wg2[j], even_gla_bg[j], even_gla_norm_g[j],
                                batch=batch, seq=seq)
            w_o = even_w_o
        else:
            parts = _odd_mixer(xb, tabs, odd_w_in, j, odd_conv_w[j], odd_conv_b[j], odd_mlstm_bi[j],
                               odd_mlstm_bf[j], odd_mlstm_norm_g[j], odd_mla_qnorm_g[j], odd_mla_kvnorm_g[j],
                               odd_mla_wuq[j], odd_mla_wukv[j], batch=batch, seq=seq)
            w_o = odd_w_o
        xf, xb = _mm_ln(parts, w_o, xf, ln1_g[l].reshape(1, -1), ln1_b[l].reshape(1, -1),
                        tm=tm_ln, tk=512, layer=j)
        hid = _ffn_up(xb, ffn_wgu, tm=_tile_for(m, 2048), tn=512, layer=l)
        xf, xb = _mm_ln((hid,), ffn_wd, xf, ln2_g[l].reshape(1, -1), ln2_b[l].reshape(1, -1),
                        tm=tm_ln, tk=512, layer=l)
    return xf.reshape(batch, seq, d)
```

```python
import functools

import jax
import jax.numpy as jnp
from jax import lax
from jax.experimental import pallas as pl
from jax.experimental.pallas import tpu as pltpu

F32 = jnp.float32
BF16 = jnp.bfloat16

D_MODEL = 2048
DEPTH = 4
GLA_HEADS, GLA_DK, GLA_DV, GLA_GATE_RANK, GLA_TAU, GLA_CHUNK = 4, 128, 256, 16, 16.0, 64
DIL_PAIRS = ((128, 1), (512, 4), (2048, 16))
DIL_HEADS, DIL_HEAD_DIM, DIL_BLOCK = 4, 128, 128
MLSTM_HEADS, MLSTM_DQK, MLSTM_DV, MLSTM_CONV, MLSTM_CHUNK = 4, 128, 256, 4, 64
MLA_HEADS, MLA_Q_RANK, MLA_KV_RANK, MLA_NOPE, MLA_ROPE, MLA_DV = 8, 512, 512, 128, 64, 128
ROPE_THETA = 10000.0
FFN_HIDDEN = 5632
DEEPNORM_ALPHA = (2.0 * DEPTH) ** 0.25

LANES = 128
VMEM_LIMIT = 56 * 1024 * 1024
NEG_INF = float("-inf")

_NT = (((1,), (1,)), ((), ()))


def _cparams(*sem):
    return pltpu.CompilerParams(dimension_semantics=sem, vmem_limit_bytes=VMEM_LIMIT)


def _bdot(a, b):
    return jnp.dot(a.astype(BF16), b.astype(BF16), preferred_element_type=F32)


def _bdot_nt(a, b):
    return lax.dot_general(a.astype(BF16), b.astype(BF16), _NT, preferred_element_type=F32)


def _bdot_tn(a, b):
    return jnp.dot(a.astype(F32).T.astype(BF16), b.astype(BF16), preferred_element_type=F32)


def _split2(a):
    hi = a.astype(BF16)
    lo = (a - hi.astype(F32)).astype(BF16)
    return hi, lo


def _dot3(a, b):
    ah, al = _split2(a)
    bh, bl = _split2(b)
    f = functools.partial(jnp.dot, preferred_element_type=F32)
    return f(ah, bh) + (f(ah, bl) + f(al, bh))


def _chunk_cumsum(tril_b, x):
    hi, lo = _split2(x)
    lo2 = (x - hi.astype(F32) - lo.astype(F32)).astype(BF16)
    f = functools.partial(jnp.dot, preferred_element_type=F32)
    return f(tril_b, hi) + (f(tril_b, lo) + f(tril_b, lo2))


def _tril(n):
    r = lax.broadcasted_iota(jnp.int32, (n, n), 0)
    c = lax.broadcasted_iota(jnp.int32, (n, n), 1)
    return c <= r


def _log_sigmoid(x):
    return jnp.minimum(x, 0.0) - jnp.log1p(jnp.exp(-jnp.abs(x)))


def _sigmoid(x):
    return 1.0 / (1.0 + jnp.exp(-x))


def _silu(x):
    return x * _sigmoid(x)


def _mm_kernel(a_ref, w_ref, o_ref):
    o_ref[...] = jnp.dot(a_ref[...], w_ref[...].astype(BF16), preferred_element_type=F32).astype(o_ref.dtype)


def _wspec(w, layer, rows, cols, index_map):
    if w.ndim == 2:
        return pl.BlockSpec((rows, cols), index_map)
    return pl.BlockSpec((None, rows, cols), lambda *ids: (layer,) + tuple(index_map(*ids)))


def _matmul(a, w, *, tm, tn, out_dtype, layer=None, ncols=None):
    m, k = a.shape
    n = w.shape[-1] if ncols is None else ncols
    return pl.pallas_call(
        _mm_kernel,
        grid=(m // tm, n // tn),
        in_specs=[pl.BlockSpec((tm, k), lambda i, j: (i, 0)),
                  _wspec(w, layer, k, tn, lambda i, j: (0, j))],
        out_specs=pl.BlockSpec((tm, tn), lambda i, j: (i, j)),
        out_shape=jax.ShapeDtypeStruct((m, n), out_dtype),
        compiler_params=_cparams("parallel", "arbitrary"),
        name="mm",
    )(a, w)


def _mm_nt_kernel(a_ref, wt_ref, o_ref):
    o_ref[...] = lax.dot_general(a_ref[...], wt_ref[0].astype(BF16), _NT,
                                 preferred_element_type=F32).astype(o_ref.dtype)


def _matmul_t(a, wt, *, layer, row0, ncols, tm, tn, out_dtype):
    m, k = a.shape
    return pl.pallas_call(
        _mm_nt_kernel,
        grid=(m // tm, ncols // tn),
        in_specs=[pl.BlockSpec((tm, k), lambda i, j: (i, 0)),
                  pl.BlockSpec((pl.Element(1), pl.Element(tn), pl.Element(k)),
                               lambda i, j: (layer, pl.multiple_of(row0 + j * tn, 8), 0))],
        out_specs=pl.BlockSpec((tm, tn), lambda i, j: (i, j)),
        out_shape=jax.ShapeDtypeStruct((m, ncols), out_dtype),
        compiler_params=_cparams("parallel", "arbitrary"),
        name="mm_nt",
    )(a, wt)


def _ffn_up_kernel(x_ref, wg_ref, wu_ref, o_ref):
    x = x_ref[...]
    g = jnp.dot(x, wg_ref[...].astype(BF16), preferred_element_type=F32)
    u = jnp.dot(x, wu_ref[...].astype(BF16), preferred_element_type=F32)
    o_ref[...] = (_silu(g) * u).astype(o_ref.dtype)


def _ffn_up(x_bf, w_gu, *, tm, tn, layer=None):
    m, k = x_bf.shape
    hidden = w_gu.shape[-1] // 2
    nb = hidden // tn
    return pl.pallas_call(
        _ffn_up_kernel,
        grid=(m // tm, nb),
        in_specs=[pl.BlockSpec((tm, k), lambda i, j: (i, 0)),
                  _wspec(w_gu, layer, k, tn, lambda i, j: (0, j)),
                  _wspec(w_gu, layer, k, tn, lambda i, j: (0, j + nb))],
        out_specs=pl.BlockSpec((tm, tn), lambda i, j: (i, j)),
        out_shape=jax.ShapeDtypeStruct((m, hidden), BF16),
        compiler_params=_cparams("parallel", "arbitrary"),
        name="ffn_up",
    )(x_bf, w_gu, w_gu)


def _mm_ln_kernel(*refs, widths, nk):
    a_refs, (w_ref, res_ref, g_ref, b_ref, of_ref, ob_ref) = refs[:-6], refs[-6:]

    def finish(y):
        mu = jnp.mean(y, axis=-1, keepdims=True)
        yc = y - mu
        var = jnp.mean(yc * yc, axis=-1, keepdims=True)
        out = yc * lax.rsqrt(var + 1e-5) * g_ref[...] + b_ref[...]
        of_ref[...] = out
        ob_ref[...] = out.astype(BF16)

    if nk == 1:
        y = DEEPNORM_ALPHA * res_ref[...]
        off = 0
        for a_ref, width in zip(a_refs, widths):
            y = y + jnp.dot(a_ref[...], w_ref[off:off + width, :].astype(BF16), preferred_element_type=F32)
            off += width
        finish(y)
        return

    k = pl.program_id(1)

    def step_dot():
        return jnp.dot(a_refs[0][...], w_ref[...].astype(BF16), preferred_element_type=F32)

    @pl.when(k == 0)
    def _():
        of_ref[...] = DEEPNORM_ALPHA * res_ref[...] + step_dot()

    @pl.when(k > 0)
    def _():
        of_ref[...] = of_ref[...] + step_dot()

    @pl.when(k == nk - 1)
    def _():
        finish(of_ref[...])


def _mm_ln(parts, w, res, g, b, *, tm, nk=1, layer=None):
    m = res.shape[0]
    n = w.shape[-1]
    widths = tuple(p.shape[1] for p in parts)
    assert nk == 1 or len(parts) == 1
    tk = sum(widths) // nk
    once = pl.Buffered(1)
    if nk == 1:
        a_specs = [pl.BlockSpec((tm, wd), lambda i, k: (i, 0)) for wd in widths]
        w_spec = pl.BlockSpec((None, tk, n), lambda i, k: (layer, 0, 0), pipeline_mode=once)
        res_spec = pl.BlockSpec((tm, n), lambda i, k: (i, 0))
    else:
        a_specs = [pl.BlockSpec((tm, tk), lambda i, k: (i, k))]
        w_spec = pl.BlockSpec((None, tk, n), lambda i, k: (layer, k, 0))
        res_spec = pl.BlockSpec((tm, n), lambda i, k: (i, 0), pipeline_mode=once)
    return pl.pallas_call(
        functools.partial(_mm_ln_kernel, widths=widths, nk=nk),
        grid=(m // tm, nk),
        in_specs=a_specs + [
            w_spec, res_spec,
            pl.BlockSpec((1, n), lambda i, k: (0, 0)),
            pl.BlockSpec((1, n), lambda i, k: (0, 0))],
        out_specs=[pl.BlockSpec((tm, n), lambda i, k: (i, 0)),
                   pl.BlockSpec((tm, n), lambda i, k: (i, 0))],
        out_shape=[jax.ShapeDtypeStruct((m, n), F32), jax.ShapeDtypeStruct((m, n), BF16)],
        compiler_params=_cparams("parallel", "arbitrary"),
        name="mm_ln",
    )(*parts, w, res, g, b)


def _gla_kernel(q_ref, k_ref, v_ref, r_ref, g_ref, wg2_ref, bg_ref, ng_ref, o_ref, st_ref, *, tile):
    L, H, DK, DV = GLA_CHUNK, GLA_HEADS, GLA_DK, GLA_DV

    @pl.when(pl.program_id(1) == 0)
    def _():
        st_ref[...] = jnp.zeros_like(st_ref)

    tril = _tril(L)
    tril_b = tril.astype(BF16)
    g16 = g_ref[:, :GLA_GATE_RANK]
    nchunk = tile // L
    hc = [(h, c) for c in range(nchunk) for h in range(H)]
    ksl = lambda h: slice(h * DK, (h + 1) * DK)
    vsl = lambda h: slice(h * DV, (h + 1) * DV)
    rsl = lambda c: slice(c * L, (c + 1) * L)
    log_a = [_log_sigmoid(_dot3(g16, wg2_ref[:, ksl(h)]) + bg_ref[:, ksl(h)]) / GLA_TAU for h in range(H)]
    b = {(h, c): _chunk_cumsum(tril_b, log_a[h][rsl(c)]) for h, c in hc}
    q_dec, k_inv, k_end, decay, vb = {}, {}, {}, {}, {}
    for h, c in hc:
        bb = b[h, c]
        b_last = bb[L - 1:L]
        kk = k_ref[rsl(c), ksl(h)]
        q_dec[h, c] = (q_ref[rsl(c), ksl(h)] * DK ** -0.5 * jnp.exp(bb)).astype(BF16)
        k_inv[h, c] = (kk * jnp.exp(-bb)).astype(BF16)
        k_end[h, c] = (kk * jnp.exp(b_last - bb)).astype(BF16)
        decay[h, c] = jnp.exp(b_last)
        vb[h, c] = v_ref[rsl(c), vsl(h)]
    scores = {p: _bdot_nt(q_dec[p], k_inv[p]) for p in hc}
    scores = {p: jnp.where(tril, scores[p], 0.0).astype(BF16) for p in hc}
    intra = {p: _bdot(scores[p], vb[p]) for p in hc}
    s_loc = {p: _bdot_tn(vb[p], k_end[p]) for p in hc}
    st = [st_ref[h] for h in range(H)]
    outs = {}
    for h, c in hc:
        outs[h, c] = intra[h, c] + _bdot_nt(q_dec[h, c], st[h])
        st[h] = st[h] * decay[h, c] + s_loc[h, c]
    for h in range(H):
        st_ref[h] = st[h]
        o = jnp.concatenate([outs[h, c] for c in range(nchunk)], axis=0)
        o = o * lax.rsqrt(jnp.mean(o * o, axis=-1, keepdims=True) + 1e-6) * ng_ref[...]
        o_ref[:, vsl(h)] = (o * _silu(r_ref[:, vsl(h)])).astype(o_ref.dtype)


def _gla(y1, y2, gates, wg2, bg, ng, *, batch, seq, tile):
    nt = seq // tile
    qk_w, v_w = GLA_HEADS * GLA_DK, GLA_HEADS * GLA_DV
    row = lambda b, t: b * nt + t
    return pl.pallas_call(
        functools.partial(_gla_kernel, tile=tile),
        grid=(batch, nt),
        in_specs=[pl.BlockSpec((tile, qk_w), lambda b, t: (row(b, t), 0)),
                  pl.BlockSpec((tile, qk_w), lambda b, t: (row(b, t), 1)),
                  pl.BlockSpec((tile, v_w), lambda b, t: (row(b, t), 1)),
                  pl.BlockSpec((tile, v_w), lambda b, t: (row(b, t), 0)),
                  pl.BlockSpec((tile, LANES), lambda b, t: (row(b, t), 0)),
                  pl.BlockSpec((GLA_GATE_RANK, qk_w), lambda b, t: (0, 0)),
                  pl.BlockSpec((1, qk_w), lambda b, t: (0, 0)),
                  pl.BlockSpec((1, GLA_DV), lambda b, t: (0, 0))],
        out_specs=pl.BlockSpec((tile, v_w), lambda b, t: (row(b, t), 0)),
        out_shape=jax.ShapeDtypeStruct((batch * seq, v_w), BF16),
        scratch_shapes=[pltpu.VMEM((GLA_HEADS, GLA_DV, GLA_DK), F32)],
        compiler_params=_cparams("parallel", "arbitrary"),
        name="gla",
    )(y1, y1, y1, y2, gates, wg2, bg, ng)


LOG2E = 1.4426950408889634
LN2 = 0.6931471805599453


def _dil_kernel(*refs, unit):
    ngrp = len(DIL_PAIRS)
    in_refs, o_ref, og_ref, lg_ref = refs[:5 * ngrp], refs[5 * ngrp], refs[5 * ngrp + 1], refs[5 * ngrp + 2]
    u = pl.program_id(1)
    n = DIL_BLOCK
    qscale = DIL_HEAD_DIM ** -0.5 * LOG2E
    row = lax.broadcasted_iota(jnp.int32, (n, n), 0)
    col = lax.broadcasted_iota(jnp.int32, (n, n), 1)
    prev_band, cur_band = col >= row, col <= row
    ones = jnp.ones((n, LANES), BF16)
    for g, (_, d) in enumerate(DIL_PAIRS):
        q_ref, k_ref, v_ref, kp_ref, vp_ref = in_refs[5 * g:5 * g + 5]
        sub = n * d
        blocks = []
        for s in range(unit // sub):
            for r in range(d):
                cur = pl.ds(s * sub + r, n, stride=d) if d > 1 else pl.ds(s * sub, n)
                if s == 0:
                    prv = pl.ds(r, n, stride=d) if d > 1 else pl.ds(0, n)
                    blocks.append((cur, kp_ref, vp_ref, prv, prev_band & (u > 0)))
                else:
                    prv = pl.ds((s - 1) * sub + r, n, stride=d) if d > 1 else pl.ds((s - 1) * sub, n)
                    blocks.append((cur, k_ref, v_ref, prv, prev_band))
        scores = []
        for cur, kpr, _, prv, pmask in blocks:
            q = (q_ref[cur, :] * qscale).astype(BF16)
            scores.append((jnp.where(pmask, _bdot_nt(q, kpr[prv, :]), NEG_INF),
                           jnp.where(cur_band, _bdot_nt(q, k_ref[cur, :]), NEG_INF)))
        probs = []
        for s_p, s_c in scores:
            m = jnp.maximum(jnp.max(s_p, axis=-1, keepdims=True), jnp.max(s_c, axis=-1, keepdims=True))
            probs.append((jnp.exp2(s_p - m).astype(BF16), jnp.exp2(s_c - m).astype(BF16), m))
        for (cur, _, vpr, prv, _), (p_p, p_c, m) in zip(blocks, probs):
            den = _bdot(p_p, ones) + _bdot(p_c, ones)
            og_ref[g, cur, :] = (_bdot(p_p, vpr[prv, :]) + _bdot(p_c, v_ref[cur, :])) / den
            lg_ref[g, cur, :] = m * LN2 + jnp.log(den)
    lses = [lg_ref[g] for g in range(ngrp)]
    m = functools.reduce(jnp.maximum, lses)
    ws = [jnp.exp(l - m) for l in lses]
    tot = functools.reduce(lambda a, b: a + b, ws)
    acc = sum(w * og_ref[g] for g, w in enumerate(ws))
    o_ref[...] = (acc / tot).astype(o_ref.dtype)


def _dilated(y, col0, *, batch, seq):
    ngrp, H, n = len(DIL_PAIRS), DIL_HEADS, DIL_BLOCK
    unit = max(d for _, d in DIL_PAIRS) * n
    assert seq % unit == 0
    nu = seq // unit
    cb0 = col0 // LANES
    in_specs, args = [], []
    for g, (_, d) in enumerate(DIL_PAIRS):
        sub = n * d
        per = unit // sub
        for part in range(3):
            cb = cb0 + part * ngrp * H + g * H
            in_specs.append(pl.BlockSpec((unit, LANES), lambda b, u, h, cb=cb: (b * nu + u, cb + h)))
            args.append(y)
        for part in (1, 2):
            cb = cb0 + part * ngrp * H + g * H
            in_specs.append(pl.BlockSpec(
                (sub, LANES),
                lambda b, u, h, cb=cb, per=per: (jnp.maximum(b * nu * per + u * per - 1, 0), cb + h)))
            args.append(y)
    return pl.pallas_call(
        functools.partial(_dil_kernel, unit=unit),
        grid=(batch, nu, H),
        in_specs=in_specs,
        out_specs=pl.BlockSpec((unit, LANES), lambda b, u, h: (b * nu + u, h)),
        out_shape=jax.ShapeDtypeStruct((batch * seq, H * LANES), BF16),
        scratch_shapes=[pltpu.VMEM((ngrp, unit, LANES), F32), pltpu.VMEM((ngrp, unit, LANES), F32)],
        compiler_params=_cparams("parallel", "parallel", "arbitrary"),
        name="dilated",
    )(*args)


I_LANE, F_LANE = 0, MLSTM_HEADS


def _mlstm_kernel(qk_ref, v_ref, co_ref, gt_ref, cw_ref, cb_ref, gb_ref, ng_ref, o_ref,
                  c_ref, m_ref, tail_ref, xbuf_ref, *, tile):
    L, H, DK, DV = MLSTM_CHUNK, MLSTM_HEADS, MLSTM_DQK, MLSTM_DV
    KC = MLSTM_CONV
    PAD = 8

    @pl.when(pl.program_id(1) == 0)
    def _():
        c_ref[...] = jnp.zeros_like(c_ref)
        m_ref[...] = jnp.zeros_like(m_ref)
        tail_ref[...] = jnp.zeros_like(tail_ref)

    xbuf_ref[0:PAD, :] = tail_ref[...]
    xbuf_ref[PAD:PAD + tile, :] = qk_ref[...]
    tail_ref[...] = qk_ref[tile - PAD:tile, :]
    acc = cb_ref[...] + cw_ref[KC - 1:KC, :] * xbuf_ref[PAD:PAD + tile, :]
    for j in range(KC - 1):
        off = PAD - (KC - 1) + j
        acc = acc + cw_ref[j:j + 1, :] * xbuf_ref[off:off + tile, :]
    qk = _silu(acc)

    gates = gt_ref[...] + gb_ref[...]
    lane = lax.broadcasted_iota(jnp.int32, gates.shape, 1)
    z = jnp.where(lane >= F_LANE, _log_sigmoid(gates), gates)
    tril = _tril(L)
    tril_b = tril.astype(BF16)
    nchunk = tile // L
    lane_c = lax.broadcasted_iota(jnp.int32, (L, LANES), 1)
    zs, bs, wts = [], [], []
    for c in range(nchunk):
        zc = z[c * L:(c + 1) * L]
        bc = _chunk_cumsum(tril_b, zc)
        zs.append(zc)
        bs.append(bc)
        wts.append(jnp.where(lane_c >= F_LANE, bc, zc).T)
    ones_col = (lane_c == 0).astype(F32)

    hc = [(h, c) for c in range(nchunk) for h in range(H)]
    rsl = lambda c: slice(c * L, (c + 1) * L)
    b_col, inter_log, kwt, s_old, s_new, d_log = {}, {}, {}, {}, {}, {}
    m = [m_ref[h:h + 1, 0:1] for h in range(H)]
    for h, c in hc:
        bc = bs[c][:, F_LANE + h:F_LANE + h + 1]
        li_col = zs[c][:, I_LANE + h:I_LANE + h + 1]
        b_row = wts[c][F_LANE + h:F_LANE + h + 1, :]
        li_row = wts[c][I_LANE + h:I_LANE + h + 1, :]
        b_last = bs[c][L - 1:L, F_LANE + h:F_LANE + h + 1]
        d_log[h, c] = jnp.where(tril, bc - b_row + li_row, NEG_INF)
        inter_log[h, c] = bc + m[h]
        a_col = b_last - bc + li_col
        m_loc = jnp.max(a_col, axis=0, keepdims=True)
        kwt[h, c] = jnp.exp(a_col - m_loc)
        m_new = jnp.maximum(b_last + m[h], m_loc)
        s_old[h, c] = jnp.exp(b_last + m[h] - m_new)
        s_new[h, c] = jnp.exp(m_loc - m_new)
        m[h] = m_new
    qb, kb, vb, qkm, m_t = {}, {}, {}, {}, {}
    for h, c in hc:
        qb[h, c] = qk[rsl(c), h * DK:(h + 1) * DK].astype(BF16)
        kb[h, c] = qk[rsl(c), (H + h) * DK:(H + h + 1) * DK] * DK ** -0.5
        vb[h, c] = v_ref[rsl(c), h * DV:(h + 1) * DV]
    raw = {p: _bdot_nt(qb[p], kb[p]) for p in hc}
    for p in hc:
        m_t[p] = jnp.maximum(inter_log[p], jnp.max(d_log[p], axis=-1, keepdims=True))
        qkm[p] = raw[p] * jnp.exp(d_log[p] - m_t[p])
    v_aug = {p: jnp.concatenate([vb[p], ones_col], axis=1).astype(BF16) for p in hc}
    intra = {p: _bdot(qkm[p], v_aug[p]) for p in hc}
    c_loc = {p: _bdot_tn(kb[p] * kwt[p], v_aug[p]) for p in hc}
    ct = [c_ref[h] for h in range(H)]
    outs = {}
    for h, c in hc:
        p = (h, c)
        inter_scale = jnp.exp(inter_log[p] - m_t[p])
        tot = intra[p] + inter_scale * _bdot(qb[p], ct[h])
        outs[p] = tot[:, :DV] / jnp.maximum(jnp.abs(tot[:, DV:DV + 1]), jnp.exp(-m_t[p]))
        ct[h] = ct[h] * s_old[p] + c_loc[p] * s_new[p]
    for h in range(H):
        c_ref[h] = ct[h]
        m_ref[h:h + 1, :] = jnp.broadcast_to(m[h], (1, LANES))
        hcat = jnp.concatenate([outs[h, c] for c in range(nchunk)], axis=0)
        mu = jnp.mean(hcat, axis=-1, keepdims=True)
        hcen = hcat - mu
        var = jnp.mean(hcen * hcen, axis=-1, keepdims=True)
        hn = hcen * lax.rsqrt(var + 1e-5) * ng_ref[...]
        o_ref[:, h * DV:(h + 1) * DV] = (_sigmoid(co_ref[:, h * DV:(h + 1) * DV]) * hn).astype(o_ref.dtype)


def _mlstm(y1, y2, gates, conv_w, conv_b, gate_bias, ng, *, batch, seq, tile):
    nt = seq // tile
    H, DK, DV = MLSTM_HEADS, MLSTM_DQK, MLSTM_DV
    qk_w, v_w = 2 * H * DK, H * DV
    row = lambda b, t: b * nt + t
    return pl.pallas_call(
        functools.partial(_mlstm_kernel, tile=tile),
        grid=(batch, nt),
        in_specs=[pl.BlockSpec((tile, qk_w), lambda b, t: (row(b, t), 0)),
                  pl.BlockSpec((tile, v_w), lambda b, t: (row(b, t), 1)),
                  pl.BlockSpec((tile, v_w), lambda b, t: (row(b, t), 0)),
                  pl.BlockSpec((tile, LANES), lambda b, t: (row(b, t), 0)),
                  pl.BlockSpec((MLSTM_CONV, qk_w), lambda b, t: (0, 0)),
                  pl.BlockSpec((1, qk_w), lambda b, t: (0, 0)),
                  pl.BlockSpec((1, LANES), lambda b, t: (0, 0)),
                  pl.BlockSpec((1, DV), lambda b, t: (0, 0))],
        out_specs=pl.BlockSpec((tile, v_w), lambda b, t: (row(b, t), 0)),
        out_shape=jax.ShapeDtypeStruct((batch * seq, v_w), BF16),
        scratch_shapes=[pltpu.VMEM((H, DK, DV + LANES), F32),
                        pltpu.VMEM((8, LANES), F32),
                        pltpu.VMEM((8, qk_w), F32),
                        pltpu.VMEM((tile + 8, qk_w), F32)],
        compiler_params=_cparams("parallel", "arbitrary"),
        name="mlstm",
    )(y1, y1, y2, gates, conv_w, conv_b, gate_bias, ng)


def _rope128(x, c_ref, s1_ref, s2_ref):
    return x * c_ref[...] + pltpu.roll(x, 32, 1) * s1_ref[...] + pltpu.roll(x, LANES - 32, 1) * s2_ref[...]


def _rms(x, g):
    return x * lax.rsqrt(jnp.mean(x * x, axis=-1, keepdims=True) + 1e-6) * g


MLA_SCORE_SCALE = (MLA_NOPE + MLA_ROPE) ** -0.5 * 1.4426950408889634


def _mla_q_kernel(x_ref, g_ref, w_ref, c_ref, s1_ref, s2_ref, o_ref):
    q = _bdot(_rms(x_ref[...], g_ref[...]), w_ref[...]) * MLA_SCORE_SCALE
    for h in range(MLA_HEADS):
        o_ref[:, 256 * h:256 * h + 128] = q[:, 256 * h:256 * h + 128].astype(o_ref.dtype)
        o_ref[:, 256 * h + 128:256 * h + 256] = _rope128(
            q[:, 256 * h + 128:256 * h + 256], c_ref, s1_ref, s2_ref).astype(o_ref.dtype)


def _mla_kv_kernel(x_ref, g_ref, wk_ref, wvt_ref, kr_ref, c_ref, s1_ref, s2_ref, k_ref, vt_ref):
    xn = _rms(x_ref[...], g_ref[...]).astype(BF16)
    kn = jnp.dot(xn, wk_ref[...], preferred_element_type=F32)
    vt_ref[...] = lax.dot_general(wvt_ref[...], xn, _NT, preferred_element_type=F32).astype(vt_ref.dtype)
    kr = _rope128(kr_ref[...], c_ref, s1_ref, s2_ref).astype(k_ref.dtype)
    for h in range(MLA_HEADS):
        k_ref[:, 256 * h:256 * h + 128] = kn[:, 128 * h:128 * h + 128].astype(k_ref.dtype)
        k_ref[:, 256 * h + 128:256 * h + 256] = kr


def _mla_proj(y, qcb, gates, qg, kvg, wq, wk, wvt, tabs, *, tm):
    m = y.shape[0]
    H = MLA_HEADS
    tab_specs = [pl.BlockSpec((tm, LANES), lambda i: (i, 0))] * 3
    full = lambda a: pl.BlockSpec(a.shape, lambda i: (0, 0))
    qf = pl.pallas_call(
        _mla_q_kernel,
        grid=(m // tm,),
        in_specs=[pl.BlockSpec((tm, MLA_Q_RANK), lambda i: (i, qcb)), full(qg), full(wq)] + tab_specs,
        out_specs=pl.BlockSpec((tm, H * 256), lambda i: (i, 0)),
        out_shape=jax.ShapeDtypeStruct((m, H * 256), BF16),
        compiler_params=_cparams("parallel"),
        name="mla_q",
    )(y, qg, wq, *tabs)
    kf, vt = pl.pallas_call(
        _mla_kv_kernel,
        grid=(m // tm,),
        in_specs=[pl.BlockSpec((tm, MLA_KV_RANK), lambda i: (i, qcb + 1)), full(kvg), full(wk), full(wvt),
                  pl.BlockSpec((tm, LANES), lambda i: (i, 0))] + tab_specs,
        out_specs=[pl.BlockSpec((tm, H * 256), lambda i: (i, 0)),
                   pl.BlockSpec((H * MLA_DV, tm), lambda i: (0, i))],
        out_shape=[jax.ShapeDtypeStruct((m, H * 256), BF16), jax.ShapeDtypeStruct((H * MLA_DV, m), BF16)],
        compiler_params=_cparams("parallel"),
        name="mla_kv",
    )(y, kvg, wk, wvt, gates, *tabs)
    return qf, kf, vt


def _mla_attn_kernel(q_ref, k_ref, vt_ref, o_ref, *, tq, hg):
    i = pl.program_id(2)
    DV = MLA_DV

    def step(j, carry, masked):
        start = pl.multiple_of(j * tq, tq)
        out = []
        sts = [lax.dot_general(k_ref[pl.ds(start, tq), 256 * g:256 * (g + 1)], q_ref[:, 256 * g:256 * (g + 1)],
                               _NT, preferred_element_type=F32) for g in range(hg)]
        for g in range(hg):
            m, l, acc = carry[g]
            vtb = vt_ref[DV * g:DV * (g + 1), pl.ds(start, tq)]
            st = sts[g]
            if masked:
                kk = lax.broadcasted_iota(jnp.int32, (tq, tq), 0)
                qq = lax.broadcasted_iota(jnp.int32, (tq, tq), 1)
                st = jnp.where(kk <= qq, st, NEG_INF)
            m_new = jnp.maximum(m, jnp.max(st, axis=0, keepdims=True))
            pt = jnp.exp2(st - m_new)
            a = jnp.exp2(m - m_new)
            l = a * l + jnp.sum(pt, axis=0, keepdims=True)
            acc = a * acc + jnp.dot(vtb, pt.astype(BF16), preferred_element_type=F32)
            out.append((m_new, l, acc))
        return tuple(out)

    init = tuple((jnp.full((1, tq), NEG_INF, F32), jnp.zeros((1, tq), F32), jnp.zeros((DV, tq), F32))
                 for _ in range(hg))
    carry = lax.fori_loop(0, i, lambda j, cr: step(j, cr, False), init)
    carry = step(i, carry, True)
    for g in range(hg):
        _, l, acc = carry[g]
        o_ref[:, DV * g:DV * (g + 1)] = (acc / l).T.astype(o_ref.dtype)


def _mla_attn(qf, kf, vt, *, batch, seq, tq, hg=4):
    H = MLA_HEADS
    nq = seq // tq
    return pl.pallas_call(
        functools.partial(_mla_attn_kernel, tq=tq, hg=hg),
        grid=(batch, H // hg, nq),
        in_specs=[pl.BlockSpec((tq, 256 * hg), lambda b, h, i: (b * nq + i, h)),
                  pl.BlockSpec((seq, 256 * hg), lambda b, h, i: (b, h)),
                  pl.BlockSpec((MLA_DV * hg, seq), lambda b, h, i: (h, b))],
        out_specs=pl.BlockSpec((tq, MLA_DV * hg), lambda b, h, i: (b * nq + i, h)),
        out_shape=jax.ShapeDtypeStruct((batch * seq, H * MLA_DV), BF16),
        compiler_params=_cparams("parallel", "parallel", "arbitrary"),
        name="mla_attn",
    )(qf, kf, vt)


def _tile_for(n, pref):
    t = pref
    while n % t:
        t //= 2
    return t


def _even_mixer(x_bf, wt_all, j, wg2, bg, ng, *, batch, seq):
    m = x_bf.shape[0]
    gq_gk_gv = 2 * GLA_HEADS * GLA_DK + GLA_HEADS * GLA_DV
    rest0 = gq_gk_gv + GLA_GATE_RANK
    tm = _tile_for(m, 2048)
    mm = functools.partial(_matmul_t, x_bf, wt_all, layer=j, tm=tm, out_dtype=F32)
    y1 = mm(row0=0, ncols=gq_gk_gv, tn=512)
    y2 = mm(row0=rest0, ncols=wt_all.shape[1] - rest0, tn=512)
    gts = mm(row0=gq_gk_gv, ncols=LANES, tn=LANES)
    o_a = _gla(y1, y2, gts, wg2, bg.reshape(1, -1), ng.reshape(1, -1), batch=batch, seq=seq,
               tile=_tile_for(seq, 256))
    o_b = _dilated(y2, GLA_HEADS * GLA_DV, batch=batch, seq=seq)
    return o_a, o_b


def _rope_tables(positions):
    inv_freq = ROPE_THETA ** (-jnp.arange(0, MLA_ROPE, 2, dtype=F32) / MLA_ROPE)
    ang = positions.astype(F32).reshape(-1, 1) * inv_freq
    cos, sin = jnp.cos(ang), jnp.sin(ang)
    z32 = jnp.zeros_like(cos)
    z64 = jnp.zeros((cos.shape[0], LANES - MLA_ROPE), F32)
    c = jnp.concatenate([z64, cos, cos], axis=1)
    s1 = jnp.concatenate([z64, z32, sin], axis=1)
    s2 = jnp.concatenate([z64, -sin, z32], axis=1)
    return c, s1, s2


def _odd_mixer(x_bf, tabs, wt_all, j, conv_w, conv_b, bi, bf, ng, qg, kvg, wuq, wukv, *, batch, seq):
    m = x_bf.shape[0]
    H = MLA_HEADS
    cq_ck_cv = 2 * MLSTM_HEADS * MLSTM_DQK + MLSTM_HEADS * MLSTM_DV
    co0 = cq_ck_cv + 2 * MLSTM_HEADS
    kr0 = co0 + MLSTM_HEADS * MLSTM_DV + MLA_Q_RANK + MLA_KV_RANK
    gate_bias = jnp.concatenate([bi, bf, jnp.zeros((LANES - 2 * MLSTM_HEADS,), F32)]).reshape(1, LANES)
    wq3 = wuq.reshape(-1, H, MLA_NOPE + MLA_ROPE)
    wq = jnp.concatenate([wq3[:, :, :MLA_NOPE], jnp.zeros(wq3.shape[:2] + (LANES - MLA_ROPE,), F32),
                          wq3[:, :, MLA_NOPE:]], axis=2).reshape(-1, H * 256).astype(BF16)
    wkv = wukv.reshape(-1, H, MLA_NOPE + MLA_DV)
    wk = wkv[:, :, :MLA_NOPE].reshape(-1, H * MLA_NOPE).astype(BF16)
    wvt = wkv[:, :, MLA_NOPE:].reshape(-1, H * MLA_DV).T.astype(BF16)
    tm = _tile_for(m, 2048)
    mm = functools.partial(_matmul_t, x_bf, wt_all, layer=j, tm=tm, out_dtype=F32)
    y1 = mm(row0=0, ncols=cq_ck_cv, tn=512)
    y2 = mm(row0=co0, ncols=kr0 - co0, tn=512)
    gts = mm(row0=cq_ck_cv, ncols=LANES, tn=LANES)
    kr = mm(row0=kr0 + MLA_ROPE - LANES, ncols=LANES, tn=LANES)
    o_c = _mlstm(y1, y2, gts, conv_w, conv_b.reshape(1, -1), gate_bias, ng.reshape(1, -1),
                 batch=batch, seq=seq, tile=_tile_for(seq, 256))
    qf, kf, vt = _mla_proj(y2, MLSTM_HEADS * MLSTM_DV // MLA_Q_RANK, kr, qg.reshape(1, -1), kvg.reshape(1, -1),
                           wq, wk, wvt, tabs, tm=_tile_for(m, 512))
    o_d = _mla_attn(qf, kf, vt, batch=batch, seq=seq, tq=_tile_for(seq, 512))
    return o_c, o_d


def kernel(x, positions, even_w_in, even_gla_wg2, even_gla_bg, even_gla_norm_g, even_w_o, odd_w_in, odd_conv_w, odd_conv_b, odd_mlstm_bi, odd_mlstm_bf, odd_mlstm_norm_g, odd_mla_qnorm_g, odd_mla_kvnorm_g, odd_mla_wuq, odd_mla_wukv, odd_w_o, ln1_g, ln1_b, ffn_wgu, ffn_wd, ln2_g, ln2_b):
    batch, seq, d = x.shape
    m = batch * seq
    xf = x.reshape(m, d)
    xb = xf.astype(BF16)
    tabs = _rope_tables(positions)
    even_wt = jnp.swapaxes(even_w_in, 1, 2)
    odd_wt = jnp.swapaxes(odd_w_in, 1, 2)
    depth = ln1_g.shape[0]
    tm_ln = _tile_for(m, 512)
    for l in range(depth):
        j = l // 2
        if l % 2 == 0:
            parts = _even_mixer(xb, even_wt, j, even_gla_wg2[j], even_gla_bg[j], even_gla_norm_g[j],
                                batch=batch, seq=seq)
            w_o = even_w_o
        else:
            parts = _odd_mixer(xb, tabs, odd_wt, j, odd_conv_w[j], odd_conv_b[j], odd_mlstm_bi[j],
                               odd_mlstm_bf[j], odd_mlstm_norm_g[j], odd_mla_qnorm_g[j], odd_mla_kvnorm_g[j],
                               odd_mla_wuq[j], odd_mla_wukv[j], batch=batch, seq=seq)
            w_o = odd_w_o
        xf, xb = _mm_ln(parts, w_o, xf, ln1_g[l].reshape(1, -1), ln1_b[l].reshape(1, -1), tm=tm_ln, layer=j)
        hid = _ffn_up(xb, ffn_wgu, tm=_tile_for(m, 2048), tn=512, layer=l)
        xf, xb = _mm_ln((hid,), ffn_wd, xf, ln2_g[l].reshape(1, -1), ln2_b[l].reshape(1, -1),
                        tm=tm_ln, nk=4, layer=l)
    return xf.reshape(batch, seq, d)
```

```python
import functools

import jax
import jax.numpy as jnp
from jax import lax
from jax.experimental import pallas as pl
from jax.experimental.pallas import tpu as pltpu

F32 = jnp.float32
BF16 = jnp.bfloat16

D_MODEL = 2048
DEPTH = 4
GLA_HEADS, GLA_DK, GLA_DV, GLA_GATE_RANK, GLA_TAU, GLA_CHUNK = 4, 128, 256, 16, 16.0, 64
DIL_PAIRS = ((128, 1), (512, 4), (2048, 16))
DIL_HEADS, DIL_HEAD_DIM, DIL_BLOCK = 4, 128, 128
MLSTM_HEADS, MLSTM_DQK, MLSTM_DV, MLSTM_CONV, MLSTM_CHUNK = 4, 128, 256, 4, 64
MLA_HEADS, MLA_Q_RANK, MLA_KV_RANK, MLA_NOPE, MLA_ROPE, MLA_DV = 8, 512, 512, 128, 64, 128
ROPE_THETA = 10000.0
FFN_HIDDEN = 5632
DEEPNORM_ALPHA = (2.0 * DEPTH) ** 0.25

LANES = 128
VMEM_LIMIT = 56 * 1024 * 1024
NEG_INF = float("-inf")

_NT = (((1,), (1,)), ((), ()))


def _cparams(*sem):
    return pltpu.CompilerParams(dimension_semantics=sem, vmem_limit_bytes=VMEM_LIMIT)


def _bdot(a, b):
    return jnp.dot(a.astype(BF16), b.astype(BF16), preferred_element_type=F32)


def _bdot_nt(a, b):
    return lax.dot_general(a.astype(BF16), b.astype(BF16), _NT, preferred_element_type=F32)


def _bdot_tn(a, b):
    return jnp.dot(a.astype(F32).T.astype(BF16), b.astype(BF16), preferred_element_type=F32)


def _split2(a):
    hi = a.astype(BF16)
    lo = (a - hi.astype(F32)).astype(BF16)
    return hi, lo


def _dot3(a, b):
    ah, al = _split2(a)
    bh, bl = _split2(b)
    f = functools.partial(jnp.dot, preferred_element_type=F32)
    return f(ah, bh) + (f(ah, bl) + f(al, bh))


def _chunk_cumsum(tril_b, x):
    hi, lo = _split2(x)
    lo2 = (x - hi.astype(F32) - lo.astype(F32)).astype(BF16)
    f = functools.partial(jnp.dot, preferred_element_type=F32)
    return f(tril_b, hi) + (f(tril_b, lo) + f(tril_b, lo2))


def _tril(n):
    r = lax.broadcasted_iota(jnp.int32, (n, n), 0)
    c = lax.broadcasted_iota(jnp.int32, (n, n), 1)
    return c <= r


def _log_sigmoid(x):
    return jnp.minimum(x, 0.0) - jnp.log1p(jnp.exp(-jnp.abs(x)))


def _sigmoid(x):
    return 1.0 / (1.0 + jnp.exp(-x))


def _silu(x):
    return x * _sigmoid(x)


def _wspec(w, layer, rows, cols, index_map):
    if w.ndim == 2:
        return pl.BlockSpec((rows, cols), index_map)
    return pl.BlockSpec((None, rows, cols), lambda *ids: (layer,) + tuple(index_map(*ids)))


def _mm_nt_kernel(*refs, nside):
    a_ref, wt_ref, side_w = refs[0], refs[1], refs[2:2 + nside]
    o_ref, side_o = refs[2 + nside], refs[3 + nside:]

    def nt(w_ref):
        return lax.dot_general(a_ref[...], w_ref[0].astype(BF16), _NT, preferred_element_type=F32)

    o_ref[...] = nt(wt_ref).astype(o_ref.dtype)

    @pl.when(pl.program_id(1) == 0)
    def _():
        for w_ref, s_ref in zip(side_w, side_o):
            s_ref[...] = nt(w_ref).astype(s_ref.dtype)


def _matmul_t(a, wt, *, layer, row0, ncols, tm, tn, out_dtype, side_rows=()):
    m, k = a.shape
    wspec = lambda rows, imap: pl.BlockSpec((pl.Element(1), pl.Element(rows), pl.Element(k)), imap)
    outs = pl.pallas_call(
        functools.partial(_mm_nt_kernel, nside=len(side_rows)),
        grid=(m // tm, ncols // tn),
        in_specs=[pl.BlockSpec((tm, k), lambda i, j: (i, 0)),
                  wspec(tn, lambda i, j: (layer, pl.multiple_of(row0 + j * tn, 8), 0))]
                 + [wspec(LANES, lambda i, j, r=r: (layer, r, 0)) for r in side_rows],
        out_specs=[pl.BlockSpec((tm, tn), lambda i, j: (i, j))]
                  + [pl.BlockSpec((tm, LANES), lambda i, j: (i, 0)) for _ in side_rows],
        out_shape=[jax.ShapeDtypeStruct((m, ncols), out_dtype)]
                  + [jax.ShapeDtypeStruct((m, LANES), out_dtype) for _ in side_rows],
        compiler_params=_cparams("parallel", "arbitrary"),
        name="mm_nt",
    )(a, wt, *([wt] * len(side_rows)))
    return outs[0] if not side_rows else tuple(outs)


def _ffn_up_kernel(x_ref, wg_ref, wu_ref, o_ref):
    x = x_ref[...]
    g = jnp.dot(x, wg_ref[...].astype(BF16), preferred_element_type=F32)
    u = jnp.dot(x, wu_ref[...].astype(BF16), preferred_element_type=F32)
    o_ref[...] = (_silu(g) * u).astype(o_ref.dtype)


def _ffn_up(x_bf, w_gu, *, tm, tn, layer=None):
    m, k = x_bf.shape
    hidden = w_gu.shape[-1] // 2
    nb = hidden // tn
    return pl.pallas_call(
        _ffn_up_kernel,
        grid=(m // tm, nb),
        in_specs=[pl.BlockSpec((tm, k), lambda i, j: (i, 0)),
                  _wspec(w_gu, layer, k, tn, lambda i, j: (0, j)),
                  _wspec(w_gu, layer, k, tn, lambda i, j: (0, j + nb))],
        out_specs=pl.BlockSpec((tm, tn), lambda i, j: (i, j)),
        out_shape=jax.ShapeDtypeStruct((m, hidden), BF16),
        compiler_params=_cparams("parallel", "arbitrary"),
        name="ffn_up",
    )(x_bf, w_gu, w_gu)


def _mm_ln_kernel(*refs, widths, nk):
    a_refs, (w_ref, res_ref, g_ref, b_ref, of_ref, ob_ref) = refs[:-6], refs[-6:]

    def finish(y):
        mu = jnp.mean(y, axis=-1, keepdims=True)
        yc = y - mu
        var = jnp.mean(yc * yc, axis=-1, keepdims=True)
        out = yc * lax.rsqrt(var + 1e-5) * g_ref[...] + b_ref[...]
        of_ref[...] = out
        ob_ref[...] = out.astype(BF16)

    if nk == 1:
        y = DEEPNORM_ALPHA * res_ref[...]
        off = 0
        for a_ref, width in zip(a_refs, widths):
            y = y + jnp.dot(a_ref[...], w_ref[off:off + width, :].astype(BF16), preferred_element_type=F32)
            off += width
        finish(y)
        return

    k = pl.program_id(1)

    def step_dot():
        return jnp.dot(a_refs[0][...], w_ref[...].astype(BF16), preferred_element_type=F32)

    @pl.when(k == 0)
    def _():
        of_ref[...] = DEEPNORM_ALPHA * res_ref[...] + step_dot()

    @pl.when(k > 0)
    def _():
        of_ref[...] = of_ref[...] + step_dot()

    @pl.when(k == nk - 1)
    def _():
        finish(of_ref[...])


def _mm_ln(parts, w, res, g, b, *, tm, nk=1, layer=None):
    m = res.shape[0]
    n = w.shape[-1]
    widths = tuple(p.shape[1] for p in parts)
    assert nk == 1 or len(parts) == 1
    tk = sum(widths) // nk
    once = pl.Buffered(1)
    if nk == 1:
        a_specs = [pl.BlockSpec((tm, wd), lambda i, k: (i, 0)) for wd in widths]
        w_spec = pl.BlockSpec((None, tk, n), lambda i, k: (layer, 0, 0), pipeline_mode=once)
        res_spec = pl.BlockSpec((tm, n), lambda i, k: (i, 0))
    else:
        a_specs = [pl.BlockSpec((tm, tk), lambda i, k: (i, k))]
        w_spec = pl.BlockSpec((None, tk, n), lambda i, k: (layer, k, 0))
        res_spec = pl.BlockSpec((tm, n), lambda i, k: (i, 0), pipeline_mode=once)
    return pl.pallas_call(
        functools.partial(_mm_ln_kernel, widths=widths, nk=nk),
        grid=(m // tm, nk),
        in_specs=a_specs + [
            w_spec, res_spec,
            pl.BlockSpec((1, n), lambda i, k: (0, 0)),
            pl.BlockSpec((1, n), lambda i, k: (0, 0))],
        out_specs=[pl.BlockSpec((tm, n), lambda i, k: (i, 0)),
                   pl.BlockSpec((tm, n), lambda i, k: (i, 0))],
        out_shape=[jax.ShapeDtypeStruct((m, n), F32), jax.ShapeDtypeStruct((m, n), BF16)],
        compiler_params=_cparams("parallel", "arbitrary"),
        name="mm_ln",
    )(*parts, w, res, g, b)


def _gla_kernel(q_ref, k_ref, v_ref, r_ref, g_ref, wg2_ref, bg_ref, ng_ref, o_ref, st_ref, *, tile):
    L, H, DK, DV = GLA_CHUNK, GLA_HEADS, GLA_DK, GLA_DV

    @pl.when(pl.program_id(1) == 0)
    def _():
        st_ref[...] = jnp.zeros_like(st_ref)

    tril = _tril(L)
    tril_b = tril.astype(BF16)
    g16 = g_ref[:, :GLA_GATE_RANK]
    nchunk = tile // L
    hc = [(h, c) for c in range(nchunk) for h in range(H)]
    ksl = lambda h: slice(h * DK, (h + 1) * DK)
    vsl = lambda h: slice(h * DV, (h + 1) * DV)
    rsl = lambda c: slice(c * L, (c + 1) * L)
    log_a = [_log_sigmoid(_dot3(g16, wg2_ref[:, ksl(h)]) + bg_ref[:, ksl(h)]) / GLA_TAU for h in range(H)]
    b = {(h, c): _chunk_cumsum(tril_b, log_a[h][rsl(c)]) for h, c in hc}
    q_dec, k_inv, k_end, decay, vb = {}, {}, {}, {}, {}
    for h, c in hc:
        bb = b[h, c]
        b_last = bb[L - 1:L]
        kk = k_ref[rsl(c), ksl(h)]
        q_dec[h, c] = (q_ref[rsl(c), ksl(h)] * DK ** -0.5 * jnp.exp(bb)).astype(BF16)
        k_inv[h, c] = (kk * jnp.exp(-bb)).astype(BF16)
        k_end[h, c] = (kk * jnp.exp(b_last - bb)).astype(BF16)
        decay[h, c] = jnp.exp(b_last)
        vb[h, c] = v_ref[rsl(c), vsl(h)]
    scores = {p: _bdot_nt(q_dec[p], k_inv[p]) for p in hc}
    scores = {p: jnp.where(tril, scores[p], 0.0).astype(BF16) for p in hc}
    intra = {p: _bdot(scores[p], vb[p]) for p in hc}
    s_loc = {p: _bdot_tn(vb[p], k_end[p]) for p in hc}
    st = [st_ref[h] for h in range(H)]
    outs = {}
    for h, c in hc:
        outs[h, c] = intra[h, c] + _bdot_nt(q_dec[h, c], st[h])
        st[h] = st[h] * decay[h, c] + s_loc[h, c]
    for h in range(H):
        st_ref[h] = st[h]
        o = jnp.concatenate([outs[h, c] for c in range(nchunk)], axis=0)
        o = o * lax.rsqrt(jnp.mean(o * o, axis=-1, keepdims=True) + 1e-6) * ng_ref[...]
        o_ref[:, vsl(h)] = (o * _silu(r_ref[:, vsl(h)])).astype(o_ref.dtype)


def _gla(y1, y2, gates, wg2, bg, ng, *, batch, seq, tile):
    nt = seq // tile
    qk_w, v_w = GLA_HEADS * GLA_DK, GLA_HEADS * GLA_DV
    row = lambda b, t: b * nt + t
    return pl.pallas_call(
        functools.partial(_gla_kernel, tile=tile),
        grid=(batch, nt),
        in_specs=[pl.BlockSpec((tile, qk_w), lambda b, t: (row(b, t), 0)),
                  pl.BlockSpec((tile, qk_w), lambda b, t: (row(b, t), 1)),
                  pl.BlockSpec((tile, v_w), lambda b, t: (row(b, t), 1)),
                  pl.BlockSpec((tile, v_w), lambda b, t: (row(b, t), 0)),
                  pl.BlockSpec((tile, LANES), lambda b, t: (row(b, t), 0)),
                  pl.BlockSpec((GLA_GATE_RANK, qk_w), lambda b, t: (0, 0)),
                  pl.BlockSpec((1, qk_w), lambda b, t: (0, 0)),
                  pl.BlockSpec((1, GLA_DV), lambda b, t: (0, 0))],
        out_specs=pl.BlockSpec((tile, v_w), lambda b, t: (row(b, t), 0)),
        out_shape=jax.ShapeDtypeStruct((batch * seq, v_w), BF16),
        scratch_shapes=[pltpu.VMEM((GLA_HEADS, GLA_DV, GLA_DK), F32)],
        compiler_params=_cparams("parallel", "arbitrary"),
        name="gla",
    )(y1, y1, y1, y2, gates, wg2, bg, ng)


LOG2E = 1.4426950408889634
LN2 = 0.6931471805599453


def _dil_kernel(*refs, unit):
    ngrp = len(DIL_PAIRS)
    in_refs, o_ref, og_ref, lg_ref = refs[:5 * ngrp], refs[5 * ngrp], refs[5 * ngrp + 1], refs[5 * ngrp + 2]
    u = pl.program_id(1)
    n = DIL_BLOCK
    qscale = DIL_HEAD_DIM ** -0.5 * LOG2E
    row = lax.broadcasted_iota(jnp.int32, (n, n), 0)
    col = lax.broadcasted_iota(jnp.int32, (n, n), 1)
    prev_band, cur_band = col >= row, col <= row
    ones = jnp.ones((n, LANES), BF16)
    for g, (_, d) in enumerate(DIL_PAIRS):
        q_ref, k_ref, v_ref, kp_ref, vp_ref = in_refs[5 * g:5 * g + 5]
        sub = n * d
        blocks = []
        for s in range(unit // sub):
            for r in range(d):
                cur = pl.ds(s * sub + r, n, stride=d) if d > 1 else pl.ds(s * sub, n)
                if s == 0:
                    prv = pl.ds(r, n, stride=d) if d > 1 else pl.ds(0, n)
                    blocks.append((cur, kp_ref, vp_ref, prv, prev_band & (u > 0)))
                else:
                    prv = pl.ds((s - 1) * sub + r, n, stride=d) if d > 1 else pl.ds((s - 1) * sub, n)
                    blocks.append((cur, k_ref, v_ref, prv, prev_band))
        scores = []
        for cur, kpr, _, prv, pmask in blocks:
            q = (q_ref[cur, :] * qscale).astype(BF16)
            scores.append((jnp.where(pmask, _bdot_nt(q, kpr[prv, :]), NEG_INF),
                           jnp.where(cur_band, _bdot_nt(q, k_ref[cur, :]), NEG_INF)))
        probs = []
        for s_p, s_c in scores:
            m = jnp.maximum(jnp.max(s_p, axis=-1, keepdims=True), jnp.max(s_c, axis=-1, keepdims=True))
            probs.append((jnp.exp2(s_p - m).astype(BF16), jnp.exp2(s_c - m).astype(BF16), m))
        for (cur, _, vpr, prv, _), (p_p, p_c, m) in zip(blocks, probs):
            den = _bdot(p_p, ones) + _bdot(p_c, ones)
            og_ref[g, cur, :] = (_bdot(p_p, vpr[prv, :]) + _bdot(p_c, v_ref[cur, :])) / den
            lg_ref[g, cur, :] = m * LN2 + jnp.log(den)
    lses = [lg_ref[g] for g in range(ngrp)]
    m = functools.reduce(jnp.maximum, lses)
    ws = [jnp.exp(l - m) for l in lses]
    tot = functools.reduce(lambda a, b: a + b, ws)
    acc = sum(w * og_ref[g] for g, w in enumerate(ws))
    o_ref[...] = (acc / tot).astype(o_ref.dtype)


def _dilated(y, col0, *, batch, seq):
    ngrp, H, n = len(DIL_PAIRS), DIL_HEADS, DIL_BLOCK
    unit = max(d for _, d in DIL_PAIRS) * n
    assert seq % unit == 0
    nu = seq // unit
    cb0 = col0 // LANES
    in_specs, args = [], []
    for g, (_, d) in enumerate(DIL_PAIRS):
        sub = n * d
        per = unit // sub
        for part in range(3):
            cb = cb0 + part * ngrp * H + g * H
            in_specs.append(pl.BlockSpec((unit, LANES), lambda b, u, h, cb=cb: (b * nu + u, cb + h)))
            args.append(y)
        for part in (1, 2):
            cb = cb0 + part * ngrp * H + g * H
            in_specs.append(pl.BlockSpec(
                (sub, LANES),
                lambda b, u, h, cb=cb, per=per: (jnp.maximum(b * nu * per + u * per - 1, 0), cb + h)))
            args.append(y)
    return pl.pallas_call(
        functools.partial(_dil_kernel, unit=unit),
        grid=(batch, nu, H),
        in_specs=in_specs,
        out_specs=pl.BlockSpec((unit, LANES), lambda b, u, h: (b * nu + u, h)),
        out_shape=jax.ShapeDtypeStruct((batch * seq, H * LANES), BF16),
        scratch_shapes=[pltpu.VMEM((ngrp, unit, LANES), F32), pltpu.VMEM((ngrp, unit, LANES), F32)],
        compiler_params=_cparams("parallel", "parallel", "arbitrary"),
        name="dilated",
    )(*args)


I_LANE, F_LANE = 0, MLSTM_HEADS


def _mlstm_kernel(qk_ref, v_ref, co_ref, gt_ref, cw_ref, cb_ref, gb_ref, ng_ref, o_ref,
                  c_ref, m_ref, tail_ref, xbuf_ref, *, tile):
    L, H, DK, DV = MLSTM_CHUNK, MLSTM_HEADS, MLSTM_DQK, MLSTM_DV
    KC = MLSTM_CONV
    PAD = 8

    @pl.when(pl.program_id(1) == 0)
    def _():
        c_ref[...] = jnp.zeros_like(c_ref)
        m_ref[...] = jnp.zeros_like(m_ref)
        tail_ref[...] = jnp.zeros_like(tail_ref)

    xbuf_ref[0:PAD, :] = tail_ref[...]
    xbuf_ref[PAD:PAD + tile, :] = qk_ref[...]
    tail_ref[...] = qk_ref[tile - PAD:tile, :]
    acc = cb_ref[...] + cw_ref[KC - 1:KC, :] * xbuf_ref[PAD:PAD + tile, :]
    for j in range(KC - 1):
        off = PAD - (KC - 1) + j
        acc = acc + cw_ref[j:j + 1, :] * xbuf_ref[off:off + tile, :]
    qk = _silu(acc)

    gates = gt_ref[...] + gb_ref[...]
    lane = lax.broadcasted_iota(jnp.int32, gates.shape, 1)
    z = jnp.where(lane >= F_LANE, _log_sigmoid(gates), gates)
    tril = _tril(L)
    tril_b = tril.astype(BF16)
    nchunk = tile // L
    lane_c = lax.broadcasted_iota(jnp.int32, (L, LANES), 1)
    zs, bs, wts = [], [], []
    for c in range(nchunk):
        zc = z[c * L:(c + 1) * L]
        bc = _chunk_cumsum(tril_b, zc)
        zs.append(zc)
        bs.append(bc)
        wts.append(jnp.where(lane_c >= F_LANE, bc, zc).T)
    ones_col = (lane_c == 0).astype(F32)

    hc = [(h, c) for c in range(nchunk) for h in range(H)]
    rsl = lambda c: slice(c * L, (c + 1) * L)
    b_col, inter_log, kwt, s_old, s_new, d_log = {}, {}, {}, {}, {}, {}
    m = [m_ref[h:h + 1, 0:1] for h in range(H)]
    for h, c in hc:
        bc = bs[c][:, F_LANE + h:F_LANE + h + 1]
        li_col = zs[c][:, I_LANE + h:I_LANE + h + 1]
        b_row = wts[c][F_LANE + h:F_LANE + h + 1, :]
        li_row = wts[c][I_LANE + h:I_LANE + h + 1, :]
        b_last = bs[c][L - 1:L, F_LANE + h:F_LANE + h + 1]
        d_log[h, c] = jnp.where(tril, bc - b_row + li_row, NEG_INF)
        inter_log[h, c] = bc + m[h]
        a_col = b_last - bc + li_col
        m_loc = jnp.max(a_col, axis=0, keepdims=True)
        kwt[h, c] = jnp.exp(a_col - m_loc)
        m_new = jnp.maximum(b_last + m[h], m_loc)
        s_old[h, c] = jnp.exp(b_last + m[h] - m_new)
        s_new[h, c] = jnp.exp(m_loc - m_new)
        m[h] = m_new
    qb, kb, vb, qkm, m_t = {}, {}, {}, {}, {}
    for h, c in hc:
        qb[h, c] = qk[rsl(c), h * DK:(h + 1) * DK].astype(BF16)
        kb[h, c] = qk[rsl(c), (H + h) * DK:(H + h + 1) * DK] * DK ** -0.5
        vb[h, c] = v_ref[rsl(c), h * DV:(h + 1) * DV]
    raw = {p: _bdot_nt(qb[p], kb[p]) for p in hc}
    for p in hc:
        m_t[p] = jnp.maximum(inter_log[p], jnp.max(d_log[p], axis=-1, keepdims=True))
        qkm[p] = raw[p] * jnp.exp(d_log[p] - m_t[p])
    v_aug = {p: jnp.concatenate([vb[p], ones_col], axis=1).astype(BF16) for p in hc}
    intra = {p: _bdot(qkm[p], v_aug[p]) for p in hc}
    c_loc = {p: _bdot_tn(kb[p] * kwt[p], v_aug[p]) for p in hc}
    ct = [c_ref[h] for h in range(H)]
    outs = {}
    for h, c in hc:
        p = (h, c)
        inter_scale = jnp.exp(inter_log[p] - m_t[p])
        tot = intra[p] + inter_scale * _bdot(qb[p], ct[h])
        outs[p] = tot[:, :DV] / jnp.maximum(jnp.abs(tot[:, DV:DV + 1]), jnp.exp(-m_t[p]))
        ct[h] = ct[h] * s_old[p] + c_loc[p] * s_new[p]
    for h in range(H):
        c_ref[h] = ct[h]
        m_ref[h:h + 1, :] = jnp.broadcast_to(m[h], (1, LANES))
        hcat = jnp.concatenate([outs[h, c] for c in range(nchunk)], axis=0)
        mu = jnp.mean(hcat, axis=-1, keepdims=True)
        hcen = hcat - mu
        var = jnp.mean(hcen * hcen, axis=-1, keepdims=True)
        hn = hcen * lax.rsqrt(var + 1e-5) * ng_ref[...]
        o_ref[:, h * DV:(h + 1) * DV] = (_sigmoid(co_ref[:, h * DV:(h + 1) * DV]) * hn).astype(o_ref.dtype)


def _mlstm(y1, y2, gates, conv_w, conv_b, gate_bias, ng, *, batch, seq, tile):
    nt = seq // tile
    H, DK, DV = MLSTM_HEADS, MLSTM_DQK, MLSTM_DV
    qk_w, v_w = 2 * H * DK, H * DV
    row = lambda b, t: b * nt + t
    return pl.pallas_call(
        functools.partial(_mlstm_kernel, tile=tile),
        grid=(batch, nt),
        in_specs=[pl.BlockSpec((tile, qk_w), lambda b, t: (row(b, t), 0)),
                  pl.BlockSpec((tile, v_w), lambda b, t: (row(b, t), 1)),
                  pl.BlockSpec((tile, v_w), lambda b, t: (row(b, t), 0)),
                  pl.BlockSpec((tile, LANES), lambda b, t: (row(b, t), 0)),
                  pl.BlockSpec((MLSTM_CONV, qk_w), lambda b, t: (0, 0)),
                  pl.BlockSpec((1, qk_w), lambda b, t: (0, 0)),
                  pl.BlockSpec((1, LANES), lambda b, t: (0, 0)),
                  pl.BlockSpec((1, DV), lambda b, t: (0, 0))],
        out_specs=pl.BlockSpec((tile, v_w), lambda b, t: (row(b, t), 0)),
        out_shape=jax.ShapeDtypeStruct((batch * seq, v_w), BF16),
        scratch_shapes=[pltpu.VMEM((H, DK, DV + LANES), F32),
                        pltpu.VMEM((8, LANES), F32),
                        pltpu.VMEM((8, qk_w), F32),
                        pltpu.VMEM((tile + 8, qk_w), F32)],
        compiler_params=_cparams("parallel", "arbitrary"),
        name="mlstm",
    )(y1, y1, y2, gates, conv_w, conv_b, gate_bias, ng)


def _rope128(x, c_ref, s1_ref, s2_ref):
    return x * c_ref[...] + pltpu.roll(x, 32, 1) * s1_ref[...] + pltpu.roll(x, LANES - 32, 1) * s2_ref[...]


def _rms(x, g):
    return x * lax.rsqrt(jnp.mean(x * x, axis=-1, keepdims=True) + 1e-6) * g


MLA_SCORE_SCALE = (MLA_NOPE + MLA_ROPE) ** -0.5 * 1.4426950408889634


def _mla_q_kernel(x_ref, g_ref, w_ref, c_ref, s1_ref, s2_ref, o_ref):
    q = _bdot(_rms(x_ref[...], g_ref[...]), w_ref[...]) * MLA_SCORE_SCALE
    for h in range(MLA_HEADS):
        o_ref[:, 256 * h:256 * h + 128] = q[:, 256 * h:256 * h + 128].astype(o_ref.dtype)
        o_ref[:, 256 * h + 128:256 * h + 256] = _rope128(
            q[:, 256 * h + 128:256 * h + 256], c_ref, s1_ref, s2_ref).astype(o_ref.dtype)


def _mla_kv_kernel(x_ref, g_ref, wk_ref, wvt_ref, kr_ref, c_ref, s1_ref, s2_ref, k_ref, vt_ref):
    xn = _rms(x_ref[...], g_ref[...]).astype(BF16)
    kn = jnp.dot(xn, wk_ref[...], preferred_element_type=F32)
    vt_ref[...] = lax.dot_general(wvt_ref[...], xn, _NT, preferred_element_type=F32).astype(vt_ref.dtype)
    kr = _rope128(kr_ref[...], c_ref, s1_ref, s2_ref).astype(k_ref.dtype)
    for h in range(MLA_HEADS):
        k_ref[:, 256 * h:256 * h + 128] = kn[:, 128 * h:128 * h + 128].astype(k_ref.dtype)
        k_ref[:, 256 * h + 128:256 * h + 256] = kr


def _mla_proj(y, qcb, gates, qg, kvg, wq, wk, wvt, tabs, *, tm):
    m = y.shape[0]
    H = MLA_HEADS
    tab_specs = [pl.BlockSpec((tm, LANES), lambda i: (i, 0))] * 3
    full = lambda a: pl.BlockSpec(a.shape, lambda i: (0, 0))
    qf = pl.pallas_call(
        _mla_q_kernel,
        grid=(m // tm,),
        in_specs=[pl.BlockSpec((tm, MLA_Q_RANK), lambda i: (i, qcb)), full(qg), full(wq)] + tab_specs,
        out_specs=pl.BlockSpec((tm, H * 256), lambda i: (i, 0)),
        out_shape=jax.ShapeDtypeStruct((m, H * 256), BF16),
        compiler_params=_cparams("parallel"),
        name="mla_q",
    )(y, qg, wq, *tabs)
    kf, vt = pl.pallas_call(
        _mla_kv_kernel,
        grid=(m // tm,),
        in_specs=[pl.BlockSpec((tm, MLA_KV_RANK), lambda i: (i, qcb + 1)), full(kvg), full(wk), full(wvt),
                  pl.BlockSpec((tm, LANES), lambda i: (i, 0))] + tab_specs,
        out_specs=[pl.BlockSpec((tm, H * 256), lambda i: (i, 0)),
                   pl.BlockSpec((H * MLA_DV, tm), lambda i: (0, i))],
        out_shape=[jax.ShapeDtypeStruct((m, H * 256), BF16), jax.ShapeDtypeStruct((H * MLA_DV, m), BF16)],
        compiler_params=_cparams("parallel"),
        name="mla_kv",
    )(y, kvg, wk, wvt, gates, *tabs)
    return qf, kf, vt


def _mla_attn_kernel(q_ref, k_ref, vt_ref, o_ref, s_ref, *, tq, hg):
    i = pl.program_id(2)
    DV = MLA_DV

    def scores(j, slot):
        start = pl.multiple_of(j * tq, tq)
        for g in range(hg):
            s_ref[slot, g] = lax.dot_general(k_ref[pl.ds(start, tq), 256 * g:256 * (g + 1)],
                                             q_ref[:, 256 * g:256 * (g + 1)], _NT,
                                             preferred_element_type=F32)

    def step(j, slot, carry, masked):
        start = pl.multiple_of(j * tq, tq)
        out = []
        for g in range(hg):
            m, l, acc = carry[g]
            vtb = vt_ref[DV * g:DV * (g + 1), pl.ds(start, tq)]
            st = s_ref[slot, g]
            if masked:
                kk = lax.broadcasted_iota(jnp.int32, (tq, tq), 0)
                qq = lax.broadcasted_iota(jnp.int32, (tq, tq), 1)
                st = jnp.where(kk <= qq, st, NEG_INF)
            m_new = jnp.maximum(m, jnp.max(st, axis=0, keepdims=True))
            pt = jnp.exp2(st - m_new)
            a = jnp.exp2(m - m_new)
            l = a * l + jnp.sum(pt, axis=0, keepdims=True)
            acc = a * acc + jnp.dot(vtb, pt.astype(BF16), preferred_element_type=F32)
            out.append((m_new, l, acc))
        return tuple(out)

    init = tuple((jnp.full((1, tq), NEG_INF, F32), jnp.zeros((1, tq), F32), jnp.zeros((DV, tq), F32))
                 for _ in range(hg))

    def pair(jj, carry):
        scores(2 * jj + 1, 1)
        carry = step(2 * jj, 0, carry, False)
        scores(2 * jj + 2, 0)
        return step(2 * jj + 1, 1, carry, False)

    def finish(carry):
        for g in range(hg):
            _, l, acc = carry[g]
            o_ref[:, DV * g:DV * (g + 1)] = (acc / l).T.astype(o_ref.dtype)

    scores(0, 0)
    carry = lax.fori_loop(0, i // 2, pair, init)

    @pl.when(i % 2 == 0)
    def _():
        finish(step(i, 0, carry, True))

    @pl.when(i % 2 == 1)
    def _():
        scores(i, 1)
        finish(step(i, 1, step(i - 1, 0, carry, False), True))


def _mla_attn(qf, kf, vt, *, batch, seq, tq, hg=4):
    H = MLA_HEADS
    nq = seq // tq
    return pl.pallas_call(
        functools.partial(_mla_attn_kernel, tq=tq, hg=hg),
        grid=(batch, H // hg, nq),
        in_specs=[pl.BlockSpec((tq, 256 * hg), lambda b, h, i: (b * nq + i, h)),
                  pl.BlockSpec((seq, 256 * hg), lambda b, h, i: (b, h)),
                  pl.BlockSpec((MLA_DV * hg, seq), lambda b, h, i: (h, b))],
        out_specs=pl.BlockSpec((tq, MLA_DV * hg), lambda b, h, i: (b * nq + i, h)),
        out_shape=jax.ShapeDtypeStruct((batch * seq, H * MLA_DV), BF16),
        scratch_shapes=[pltpu.VMEM((2, hg, tq, tq), F32)],
        compiler_params=_cparams("parallel", "parallel", "arbitrary"),
        name="mla_attn",
    )(qf, kf, vt)


def _tile_for(n, pref):
    t = pref
    while n % t:
        t //= 2
    return t


def _even_mixer(x_bf, wt_all, j, wg2, bg, ng, *, batch, seq):
    m = x_bf.shape[0]
    gq_gk_gv = 2 * GLA_HEADS * GLA_DK + GLA_HEADS * GLA_DV
    rest0 = gq_gk_gv + GLA_GATE_RANK
    tm = _tile_for(m, 2048)
    mm = functools.partial(_matmul_t, x_bf, wt_all, layer=j, tm=tm, out_dtype=F32)
    y1, gts = mm(row0=0, ncols=gq_gk_gv, tn=512, side_rows=(gq_gk_gv,))
    y2 = mm(row0=rest0, ncols=wt_all.shape[1] - rest0, tn=512)
    o_a = _gla(y1, y2, gts, wg2, bg.reshape(1, -1), ng.reshape(1, -1), batch=batch, seq=seq,
               tile=_tile_for(seq, 256))
    o_b = _dilated(y2, GLA_HEADS * GLA_DV, batch=batch, seq=seq)
    return o_a, o_b


def _rope_tables(positions):
    inv_freq = ROPE_THETA ** (-jnp.arange(0, MLA_ROPE, 2, dtype=F32) / MLA_ROPE)
    ang = positions.astype(F32).reshape(-1, 1) * inv_freq
    cos, sin = jnp.cos(ang), jnp.sin(ang)
    z32 = jnp.zeros_like(cos)
    z64 = jnp.zeros((cos.shape[0], LANES - MLA_ROPE), F32)
    c = jnp.concatenate([z64, cos, cos], axis=1)
    s1 = jnp.concatenate([z64, z32, sin], axis=1)
    s2 = jnp.concatenate([z64, -sin, z32], axis=1)
    return c, s1, s2


def _odd_mixer(x_bf, tabs, wt_all, j, conv_w, conv_b, bi, bf, ng, qg, kvg, wuq, wukv, *, batch, seq):
    m = x_bf.shape[0]
    H = MLA_HEADS
    cq_ck_cv = 2 * MLSTM_HEADS * MLSTM_DQK + MLSTM_HEADS * MLSTM_DV
    co0 = cq_ck_cv + 2 * MLSTM_HEADS
    kr0 = co0 + MLSTM_HEADS * MLSTM_DV + MLA_Q_RANK + MLA_KV_RANK
    gate_bias = jnp.concatenate([bi, bf, jnp.zeros((LANES - 2 * MLSTM_HEADS,), F32)]).reshape(1, LANES)
    wq3 = wuq.reshape(-1, H, MLA_NOPE + MLA_ROPE)
    wq = jnp.concatenate([wq3[:, :, :MLA_NOPE], jnp.zeros(wq3.shape[:2] + (LANES - MLA_ROPE,), F32),
                          wq3[:, :, MLA_NOPE:]], axis=2).reshape(-1, H * 256).astype(BF16)
    wkv = wukv.reshape(-1, H, MLA_NOPE + MLA_DV)
    wk = wkv[:, :, :MLA_NOPE].reshape(-1, H * MLA_NOPE).astype(BF16)
    wvt = wkv[:, :, MLA_NOPE:].reshape(-1, H * MLA_DV).T.astype(BF16)
    tm = _tile_for(m, 2048)
    mm = functools.partial(_matmul_t, x_bf, wt_all, layer=j, tm=tm, out_dtype=F32)
    y1, gts, kr = mm(row0=0, ncols=cq_ck_cv, tn=512, side_rows=(cq_ck_cv, kr0 + MLA_ROPE - LANES))
    y2 = mm(row0=co0, ncols=kr0 - co0, tn=512)
    o_c = _mlstm(y1, y2, gts, conv_w, conv_b.reshape(1, -1), gate_bias, ng.reshape(1, -1),
                 batch=batch, seq=seq, tile=_tile_for(seq, 256))
    qf, kf, vt = _mla_proj(y2, MLSTM_HEADS * MLSTM_DV // MLA_Q_RANK, kr, qg.reshape(1, -1), kvg.reshape(1, -1),
                           wq, wk, wvt, tabs, tm=_tile_for(m, 512))
    o_d = _mla_attn(qf, kf, vt, batch=batch, seq=seq, tq=_tile_for(seq, 512))
    return o_c, o_d


def kernel(x, positions, even_w_in, even_gla_wg2, even_gla_bg, even_gla_norm_g, even_w_o, odd_w_in, odd_conv_w, odd_conv_b, odd_mlstm_bi, odd_mlstm_bf, odd_mlstm_norm_g, odd_mla_qnorm_g, odd_mla_kvnorm_g, odd_mla_wuq, odd_mla_wukv, odd_w_o, ln1_g, ln1_b, ffn_wgu, ffn_wd, ln2_g, ln2_b):
    batch, seq, d = x.shape
    m = batch * seq
    xf = x.reshape(m, d)
    xb = xf.astype(BF16)
    tabs = _rope_tables(positions)
    even_wt = jnp.swapaxes(even_w_in, 1, 2)
    odd_wt = jnp.swapaxes(odd_w_in, 1, 2)
    depth = ln1_g.shape[0]
    tm_ln = _tile_for(m, 512)
    for l in range(depth):
        j = l // 2
        if l % 2 == 0:
            parts = _even_mixer(xb, even_wt, j, even_gla_wg2[j], even_gla_bg[j], even_gla_norm_g[j],
                                batch=batch, seq=seq)
            w_o = even_w_o
        else:
            parts = _odd_mixer(xb, tabs, odd_wt, j, odd_conv_w[j], odd_conv_b[j], odd_mlstm_bi[j],
                               odd_mlstm_bf[j], odd_mlstm_norm_g[j], odd_mla_qnorm_g[j], odd_mla_kvnorm_g[j],
                               odd_mla_wuq[j], odd_mla_wukv[j], batch=batch, seq=seq)
            w_o = odd_w_o
        xf, xb = _mm_ln(parts, w_o, xf, ln1_g[l].reshape(1, -1), ln1_b[l].reshape(1, -1), tm=tm_ln, layer=j)
        hid = _ffn_up(xb, ffn_wgu, tm=_tile_for(m, 2048), tn=512, layer=l)
        xf, xb = _mm_ln((hid,), ffn_wd, xf, ln2_g[l].reshape(1, -1), ln2_b[l].reshape(1, -1),
                        tm=_tile_for(m, 1024), nk=hid.shape[1] // 512, layer=l)
    return xf.reshape(batch, seq, d)
```

```python
import functools

import jax
import jax.numpy as jnp
from jax import lax
from jax.experimental import pallas as pl
from jax.experimental.pallas import tpu as pltpu

F32 = jnp.float32
BF16 = jnp.bfloat16

D_MODEL = 2048
DEPTH = 4
GLA_HEADS, GLA_DK, GLA_DV, GLA_GATE_RANK, GLA_TAU, GLA_CHUNK = 4, 128, 256, 16, 16.0, 64
DIL_PAIRS = ((128, 1), (512, 4), (2048, 16))
DIL_HEADS, DIL_HEAD_DIM, DIL_BLOCK = 4, 128, 128
MLSTM_HEADS, MLSTM_DQK, MLSTM_DV, MLSTM_CONV, MLSTM_CHUNK = 4, 128, 256, 4, 64
MLA_HEADS, MLA_Q_RANK, MLA_KV_RANK, MLA_NOPE, MLA_ROPE, MLA_DV = 8, 512, 512, 128, 64, 128
ROPE_THETA = 10000.0
FFN_HIDDEN = 5632
DEEPNORM_ALPHA = (2.0 * DEPTH) ** 0.25

LANES = 128
VMEM_LIMIT = 56 * 1024 * 1024
NEG_INF = float("-inf")

_NT = (((1,), (1,)), ((), ()))


def _cparams(*sem):
    return pltpu.CompilerParams(dimension_semantics=sem, vmem_limit_bytes=VMEM_LIMIT)


def _bdot(a, b):
    return jnp.dot(a.astype(BF16), b.astype(BF16), preferred_element_type=F32)


def _bdot_nt(a, b):
    return lax.dot_general(a.astype(BF16), b.astype(BF16), _NT, preferred_element_type=F32)


def _bdot_tn(a, b):
    return jnp.dot(a.astype(F32).T.astype(BF16), b.astype(BF16), preferred_element_type=F32)


def _split2(a):
    hi = a.astype(BF16)
    lo = (a - hi.astype(F32)).astype(BF16)
    return hi, lo


def _dot3(a, b):
    ah, al = _split2(a)
    bh, bl = _split2(b)
    f = functools.partial(jnp.dot, preferred_element_type=F32)
    return f(ah, bh) + (f(ah, bl) + f(al, bh))


def _chunk_cumsum(tril_b, x):
    hi, lo = _split2(x)
    lo2 = (x - hi.astype(F32) - lo.astype(F32)).astype(BF16)
    f = functools.partial(jnp.dot, preferred_element_type=F32)
    return f(tril_b, hi) + (f(tril_b, lo) + f(tril_b, lo2))


def _tril(n):
    r = lax.broadcasted_iota(jnp.int32, (n, n), 0)
    c = lax.broadcasted_iota(jnp.int32, (n, n), 1)
    return c <= r


def _log_sigmoid(x):
    return jnp.minimum(x, 0.0) - jnp.log1p(jnp.exp(-jnp.abs(x)))


def _sigmoid(x):
    return 1.0 / (1.0 + jnp.exp(-x))


def _silu(x):
    return x * _sigmoid(x)


def _wspec(w, layer, rows, cols, index_map):
    if w.ndim == 2:
        return pl.BlockSpec((rows, cols), index_map)
    return pl.BlockSpec((None, rows, cols), lambda *ids: (layer,) + tuple(index_map(*ids)))


def _mm_nt_kernel(*refs, nside):
    a_ref, wt_ref, side_w = refs[0], refs[1], refs[2:2 + nside]
    o_ref, side_o = refs[2 + nside], refs[3 + nside:]

    def nt(w_ref):
        return lax.dot_general(a_ref[...], w_ref[0].astype(BF16), _NT, preferred_element_type=F32)

    o_ref[...] = nt(wt_ref).astype(o_ref.dtype)

    @pl.when(pl.program_id(1) == 0)
    def _():
        for w_ref, s_ref in zip(side_w, side_o):
            s_ref[...] = nt(w_ref).astype(s_ref.dtype)


def _matmul_t(a, wt, *, layer, row0, ncols, tm, tn, out_dtype, side_rows=()):
    m, k = a.shape
    wspec = lambda rows, imap: pl.BlockSpec((pl.Element(1), pl.Element(rows), pl.Element(k)), imap)
    outs = pl.pallas_call(
        functools.partial(_mm_nt_kernel, nside=len(side_rows)),
        grid=(m // tm, ncols // tn),
        in_specs=[pl.BlockSpec((tm, k), lambda i, j: (i, 0)),
                  wspec(tn, lambda i, j: (layer, pl.multiple_of(row0 + j * tn, 8), 0))]
                 + [wspec(LANES, lambda i, j, r=r: (layer, r, 0)) for r in side_rows],
        out_specs=[pl.BlockSpec((tm, tn), lambda i, j: (i, j))]
                  + [pl.BlockSpec((tm, LANES), lambda i, j: (i, 0)) for _ in side_rows],
        out_shape=[jax.ShapeDtypeStruct((m, ncols), out_dtype)]
                  + [jax.ShapeDtypeStruct((m, LANES), out_dtype) for _ in side_rows],
        compiler_params=_cparams("parallel", "arbitrary"),
        name="mm_nt",
    )(a, wt, *([wt] * len(side_rows)))
    return outs[0] if not side_rows else tuple(outs)


def _ffn_up_kernel(x_ref, wg_ref, wu_ref, wd_ref, o_ref, wdb_ref):
    x = x_ref[...]
    g = jnp.dot(x, wg_ref[...].astype(BF16), preferred_element_type=F32)
    u = jnp.dot(x, wu_ref[...].astype(BF16), preferred_element_type=F32)
    o_ref[...] = (_silu(g) * u).astype(o_ref.dtype)
    wdb_ref[...] = wd_ref[...].astype(wdb_ref.dtype)


def _ffn_up(x_bf, w_gu, w_d, *, tm, tn, layer):
    m, k = x_bf.shape
    hidden = w_gu.shape[-1] // 2
    nb = hidden // tn
    steps = (m // tm) * nb
    slab = hidden // steps
    assert slab * steps == hidden and slab % 16 == 0
    return pl.pallas_call(
        _ffn_up_kernel,
        grid=(m // tm, nb),
        in_specs=[pl.BlockSpec((tm, k), lambda i, j: (i, 0)),
                  _wspec(w_gu, layer, k, tn, lambda i, j: (0, j)),
                  _wspec(w_gu, layer, k, tn, lambda i, j: (0, j + nb)),
                  _wspec(w_d, layer, slab, w_d.shape[-1], lambda i, j: (i * nb + j, 0))],
        out_specs=[pl.BlockSpec((tm, tn), lambda i, j: (i, j)),
                   pl.BlockSpec((slab, w_d.shape[-1]), lambda i, j: (i * nb + j, 0))],
        out_shape=[jax.ShapeDtypeStruct((m, hidden), BF16),
                   jax.ShapeDtypeStruct((hidden, w_d.shape[-1]), BF16)],
        compiler_params=_cparams("parallel", "arbitrary"),
        name="ffn_up",
    )(x_bf, w_gu, w_gu, w_d)


def _mm_ln_kernel(*refs, widths, nk):
    a_refs, (w_ref, res_ref, g_ref, b_ref, of_ref, ob_ref) = refs[:-6], refs[-6:]

    def finish(y):
        mu = jnp.mean(y, axis=-1, keepdims=True)
        yc = y - mu
        var = jnp.mean(yc * yc, axis=-1, keepdims=True)
        out = yc * lax.rsqrt(var + 1e-5) * g_ref[...] + b_ref[...]
        of_ref[...] = out
        ob_ref[...] = out.astype(BF16)

    if nk == 1:
        y = DEEPNORM_ALPHA * res_ref[...]
        off = 0
        for a_ref, width in zip(a_refs, widths):
            y = y + jnp.dot(a_ref[...], w_ref[off:off + width, :].astype(BF16), preferred_element_type=F32)
            off += width
        finish(y)
        return

    k = pl.program_id(1)

    def step_dot():
        return jnp.dot(a_refs[0][...], w_ref[...].astype(BF16), preferred_element_type=F32)

    @pl.when(k == 0)
    def _():
        of_ref[...] = DEEPNORM_ALPHA * res_ref[...] + step_dot()

    @pl.when(k > 0)
    def _():
        of_ref[...] = of_ref[...] + step_dot()

    @pl.when(k == nk - 1)
    def _():
        finish(of_ref[...])


def _mm_ln(parts, w, res, g, b, *, tm, nk=1, layer=None):
    m = res.shape[0]
    n = w.shape[-1]
    widths = tuple(p.shape[1] for p in parts)
    assert nk == 1 or len(parts) == 1
    tk = sum(widths) // nk
    once = pl.Buffered(1)
    if nk == 1:
        a_specs = [pl.BlockSpec((tm, wd), lambda i, k: (i, 0)) for wd in widths]
        w_spec = pl.BlockSpec((None, tk, n), lambda i, k: (layer, 0, 0), pipeline_mode=once)
        res_spec = pl.BlockSpec((tm, n), lambda i, k: (i, 0))
    else:
        a_specs = [pl.BlockSpec((tm, tk), lambda i, k: (i, k))]
        w_spec = _wspec(w, layer, tk, n, lambda i, k: (k, 0))
        res_spec = pl.BlockSpec((tm, n), lambda i, k: (i, 0), pipeline_mode=once)
    return pl.pallas_call(
        functools.partial(_mm_ln_kernel, widths=widths, nk=nk),
        grid=(m // tm, nk),
        in_specs=a_specs + [
            w_spec, res_spec,
            pl.BlockSpec((1, n), lambda i, k: (0, 0)),
            pl.BlockSpec((1, n), lambda i, k: (0, 0))],
        out_specs=[pl.BlockSpec((tm, n), lambda i, k: (i, 0)),
                   pl.BlockSpec((tm, n), lambda i, k: (i, 0))],
        out_shape=[jax.ShapeDtypeStruct((m, n), F32), jax.ShapeDtypeStruct((m, n), BF16)],
        compiler_params=_cparams("parallel", "arbitrary"),
        name="mm_ln",
    )(*parts, w, res, g, b)


def _gla_kernel(q_ref, k_ref, v_ref, r_ref, g_ref, wg2_ref, bg_ref, ng_ref, o_ref, st_ref, *, tile):
    L, H, DK, DV = GLA_CHUNK, GLA_HEADS, GLA_DK, GLA_DV

    @pl.when(pl.program_id(1) == 0)
    def _():
        st_ref[...] = jnp.zeros_like(st_ref)

    tril = _tril(L)
    tril_b = tril.astype(BF16)
    g16 = g_ref[:, :GLA_GATE_RANK]
    nchunk = tile // L
    hc = [(h, c) for c in range(nchunk) for h in range(H)]
    ksl = lambda h: slice(h * DK, (h + 1) * DK)
    vsl = lambda h: slice(h * DV, (h + 1) * DV)
    rsl = lambda c: slice(c * L, (c + 1) * L)
    log_a = [_log_sigmoid(_dot3(g16, wg2_ref[:, ksl(h)]) + bg_ref[:, ksl(h)]) / GLA_TAU for h in range(H)]
    b = {(h, c): _chunk_cumsum(tril_b, log_a[h][rsl(c)]) for h, c in hc}
    q_dec, k_inv, k_end, decay, vb = {}, {}, {}, {}, {}
    for h, c in hc:
        bb = b[h, c]
        b_last = bb[L - 1:L]
        kk = k_ref[rsl(c), ksl(h)]
        q_dec[h, c] = (q_ref[rsl(c), ksl(h)] * DK ** -0.5 * jnp.exp(bb)).astype(BF16)
        k_inv[h, c] = (kk * jnp.exp(-bb)).astype(BF16)
        k_end[h, c] = (kk * jnp.exp(b_last - bb)).astype(BF16)
        decay[h, c] = jnp.exp(b_last)
        vb[h, c] = v_ref[rsl(c), vsl(h)]
    scores = {p: _bdot_nt(q_dec[p], k_inv[p]) for p in hc}
    scores = {p: jnp.where(tril, scores[p], 0.0).astype(BF16) for p in hc}
    intra = {p: _bdot(scores[p], vb[p]) for p in hc}
    s_loc = {p: _bdot_tn(vb[p], k_end[p]) for p in hc}
    st = [st_ref[h] for h in range(H)]
    outs = {}
    for h, c in hc:
        outs[h, c] = intra[h, c] + _bdot_nt(q_dec[h, c], st[h])
        st[h] = st[h] * decay[h, c] + s_loc[h, c]
    for h in range(H):
        st_ref[h] = st[h]
        o = jnp.concatenate([outs[h, c] for c in range(nchunk)], axis=0)
        o = o * lax.rsqrt(jnp.mean(o * o, axis=-1, keepdims=True) + 1e-6) * ng_ref[...]
        o_ref[:, vsl(h)] = (o * _silu(r_ref[:, vsl(h)])).astype(o_ref.dtype)


def _gla(y1, y2, gates, wg2, bg, ng, *, batch, seq, tile):
    nt = seq // tile
    qk_w, v_w = GLA_HEADS * GLA_DK, GLA_HEADS * GLA_DV
    row = lambda b, t: b * nt + t
    return pl.pallas_call(
        functools.partial(_gla_kernel, tile=tile),
        grid=(batch, nt),
        in_specs=[pl.BlockSpec((tile, qk_w), lambda b, t: (row(b, t), 0)),
                  pl.BlockSpec((tile, qk_w), lambda b, t: (row(b, t), 1)),
                  pl.BlockSpec((tile, v_w), lambda b, t: (row(b, t), 1)),
                  pl.BlockSpec((tile, v_w), lambda b, t: (row(b, t), 0)),
                  pl.BlockSpec((tile, LANES), lambda b, t: (row(b, t), 0)),
                  pl.BlockSpec((GLA_GATE_RANK, qk_w), lambda b, t: (0, 0)),
                  pl.BlockSpec((1, qk_w), lambda b, t: (0, 0)),
                  pl.BlockSpec((1, GLA_DV), lambda b, t: (0, 0))],
        out_specs=pl.BlockSpec((tile, v_w), lambda b, t: (row(b, t), 0)),
        out_shape=jax.ShapeDtypeStruct((batch * seq, v_w), BF16),
        scratch_shapes=[pltpu.VMEM((GLA_HEADS, GLA_DV, GLA_DK), F32)],
        compiler_params=_cparams("parallel", "arbitrary"),
        name="gla",
    )(y1, y1, y1, y2, gates, wg2, bg, ng)


LOG2E = 1.4426950408889634
LN2 = 0.6931471805599453


def _dil_kernel(*refs, unit):
    ngrp = len(DIL_PAIRS)
    in_refs, o_ref, og_ref, lg_ref = refs[:5 * ngrp], refs[5 * ngrp], refs[5 * ngrp + 1], refs[5 * ngrp + 2]
    u = pl.program_id(1)
    n = DIL_BLOCK
    qscale = DIL_HEAD_DIM ** -0.5 * LOG2E
    row = lax.broadcasted_iota(jnp.int32, (n, n), 0)
    col = lax.broadcasted_iota(jnp.int32, (n, n), 1)
    prev_band, cur_band = col >= row, col <= row
    ones = jnp.ones((n, LANES), BF16)
    for g, (_, d) in enumerate(DIL_PAIRS):
        q_ref, k_ref, v_ref, kp_ref, vp_ref = in_refs[5 * g:5 * g + 5]
        sub = n * d
        blocks = []
        for s in range(unit // sub):
            for r in range(d):
                cur = pl.ds(s * sub + r, n, stride=d) if d > 1 else pl.ds(s * sub, n)
                if s == 0:
                    prv = pl.ds(r, n, stride=d) if d > 1 else pl.ds(0, n)
                    blocks.append((cur, kp_ref, vp_ref, prv, prev_band & (u > 0)))
                else:
                    prv = pl.ds((s - 1) * sub + r, n, stride=d) if d > 1 else pl.ds((s - 1) * sub, n)
                    blocks.append((cur, k_ref, v_ref, prv, prev_band))
        scores = []
        for cur, kpr, _, prv, pmask in blocks:
            q = (q_ref[cur, :] * qscale).astype(BF16)
            scores.append((jnp.where(pmask, _bdot_nt(q, kpr[prv, :]), NEG_INF),
                           jnp.where(cur_band, _bdot_nt(q, k_ref[cur, :]), NEG_INF)))
        probs = []
        for s_p, s_c in scores:
            m = jnp.maximum(jnp.max(s_p, axis=-1, keepdims=True), jnp.max(s_c, axis=-1, keepdims=True))
            probs.append((jnp.exp2(s_p - m).astype(BF16), jnp.exp2(s_c - m).astype(BF16), m))
        for (cur, _, vpr, prv, _), (p_p, p_c, m) in zip(blocks, probs):
            den = _bdot(p_p, ones) + _bdot(p_c, ones)
            og_ref[g, cur, :] = (_bdot(p_p, vpr[prv, :]) + _bdot(p_c, v_ref[cur, :])) / den
            lg_ref[g, cur, :] = m * LN2 + jnp.log(den)
    lses = [lg_ref[g] for g in range(ngrp)]
    m = functools.reduce(jnp.maximum, lses)
    ws = [jnp.exp(l - m) for l in lses]
    tot = functools.reduce(lambda a, b: a + b, ws)
    acc = sum(w * og_ref[g] for g, w in enumerate(ws))
    o_ref[...] = (acc / tot).astype(o_ref.dtype)


def _dilated(y, col0, *, batch, seq):
    ngrp, H, n = len(DIL_PAIRS), DIL_HEADS, DIL_BLOCK
    unit = max(d for _, d in DIL_PAIRS) * n
    assert seq % unit == 0
    nu = seq // unit
    cb0 = col0 // LANES
    in_specs, args = [], []
    for g, (_, d) in enumerate(DIL_PAIRS):
        sub = n * d
        per = unit // sub
        for part in range(3):
            cb = cb0 + part * ngrp * H + g * H
            in_specs.append(pl.BlockSpec((unit, LANES), lambda b, u, h, cb=cb: (b * nu + u, cb + h)))
            args.append(y)
        for part in (1, 2):
            cb = cb0 + part * ngrp * H + g * H
            in_specs.append(pl.BlockSpec(
                (sub, LANES),
                lambda b, u, h, cb=cb, per=per: (jnp.maximum(b * nu * per + u * per - 1, 0), cb + h)))
            args.append(y)
    return pl.pallas_call(
        functools.partial(_dil_kernel, unit=unit),
        grid=(batch, nu, H),
        in_specs=in_specs,
        out_specs=pl.BlockSpec((unit, LANES), lambda b, u, h: (b * nu + u, h)),
        out_shape=jax.ShapeDtypeStruct((batch * seq, H * LANES), BF16),
        scratch_shapes=[pltpu.VMEM((ngrp, unit, LANES), F32), pltpu.VMEM((ngrp, unit, LANES), F32)],
        compiler_params=_cparams("parallel", "parallel", "arbitrary"),
        name="dilated",
    )(*args)


I_LANE, F_LANE = 0, MLSTM_HEADS


def _mlstm_kernel(qk_ref, v_ref, co_ref, gt_ref, cw_ref, cb_ref, gb_ref, ng_ref, o_ref,
                  c_ref, m_ref, tail_ref, xbuf_ref, *, tile):
    L, H, DK, DV = MLSTM_CHUNK, MLSTM_HEADS, MLSTM_DQK, MLSTM_DV
    KC = MLSTM_CONV
    PAD = 8

    @pl.when(pl.program_id(1) == 0)
    def _():
        c_ref[...] = jnp.zeros_like(c_ref)
        m_ref[...] = jnp.zeros_like(m_ref)
        tail_ref[...] = jnp.zeros_like(tail_ref)

    xbuf_ref[0:PAD, :] = tail_ref[...]
    xbuf_ref[PAD:PAD + tile, :] = qk_ref[...]
    tail_ref[...] = qk_ref[tile - PAD:tile, :]
    acc = cb_ref[...] + cw_ref[KC - 1:KC, :] * xbuf_ref[PAD:PAD + tile, :]
    for j in range(KC - 1):
        off = PAD - (KC - 1) + j
        acc = acc + cw_ref[j:j + 1, :] * xbuf_ref[off:off + tile, :]
    qk = _silu(acc)

    gates = gt_ref[...] + gb_ref[...]
    lane = lax.broadcasted_iota(jnp.int32, gates.shape, 1)
    z = jnp.where(lane >= F_LANE, _log_sigmoid(gates), gates)
    tril = _tril(L)
    tril_b = tril.astype(BF16)
    nchunk = tile // L
    lane_c = lax.broadcasted_iota(jnp.int32, (L, LANES), 1)
    zs, bs, wts = [], [], []
    for c in range(nchunk):
        zc = z[c * L:(c + 1) * L]
        bc = _chunk_cumsum(tril_b, zc)
        zs.append(zc)
        bs.append(bc)
        wts.append(jnp.where(lane_c >= F_LANE, bc, zc).T)
    ones_col = (lane_c == 0).astype(F32)

    hc = [(h, c) for c in range(nchunk) for h in range(H)]
    rsl = lambda c: slice(c * L, (c + 1) * L)
    b_col, inter_log, kwt, s_old, s_new, d_log = {}, {}, {}, {}, {}, {}
    m = [m_ref[h:h + 1, 0:1] for h in range(H)]
    for h, c in hc:
        bc = bs[c][:, F_LANE + h:F_LANE + h + 1]
        li_col = zs[c][:, I_LANE + h:I_LANE + h + 1]
        b_row = wts[c][F_LANE + h:F_LANE + h + 1, :]
        li_row = wts[c][I_LANE + h:I_LANE + h + 1, :]
        b_last = bs[c][L - 1:L, F_LANE + h:F_LANE + h + 1]
        d_log[h, c] = jnp.where(tril, bc - b_row + li_row, NEG_INF)
        inter_log[h, c] = bc + m[h]
        a_col = b_last - bc + li_col
        m_loc = jnp.max(a_col, axis=0, keepdims=True)
        kwt[h, c] = jnp.exp(a_col - m_loc)
        m_new = jnp.maximum(b_last + m[h], m_loc)
        s_old[h, c] = jnp.exp(b_last + m[h] - m_new)
        s_new[h, c] = jnp.exp(m_loc - m_new)
        m[h] = m_new
    qb, kb, vb, qkm, m_t = {}, {}, {}, {}, {}
    for h, c in hc:
        qb[h, c] = qk[rsl(c), h * DK:(h + 1) * DK].astype(BF16)
        kb[h, c] = qk[rsl(c), (H + h) * DK:(H + h + 1) * DK] * DK ** -0.5
        vb[h, c] = v_ref[rsl(c), h * DV:(h + 1) * DV]
    raw = {p: _bdot_nt(qb[p], kb[p]) for p in hc}
    for p in hc:
        m_t[p] = jnp.maximum(inter_log[p], jnp.max(d_log[p], axis=-1, keepdims=True))
        qkm[p] = raw[p] * jnp.exp(d_log[p] - m_t[p])
    v_aug = {p: jnp.concatenate([vb[p], ones_col], axis=1).astype(BF16) for p in hc}
    intra = {p: _bdot(qkm[p], v_aug[p]) for p in hc}
    c_loc = {p: _bdot_tn(kb[p] * kwt[p], v_aug[p]) for p in hc}
    ct = [c_ref[h] for h in range(H)]
    outs = {}
    for h, c in hc:
        p = (h, c)
        inter_scale = jnp.exp(inter_log[p] - m_t[p])
        tot = intra[p] + inter_scale * _bdot(qb[p], ct[h])
        outs[p] = tot[:, :DV] / jnp.maximum(jnp.abs(tot[:, DV:DV + 1]), jnp.exp(-m_t[p]))
        ct[h] = ct[h] * s_old[p] + c_loc[p] * s_new[p]
    for h in range(H):
        c_ref[h] = ct[h]
        m_ref[h:h + 1, :] = jnp.broadcast_to(m[h], (1, LANES))
        hcat = jnp.concatenate([outs[h, c] for c in range(nchunk)], axis=0)
        mu = jnp.mean(hcat, axis=-1, keepdims=True)
        hcen = hcat - mu
        var = jnp.mean(hcen * hcen, axis=-1, keepdims=True)
        hn = hcen * lax.rsqrt(var + 1e-5) * ng_ref[...]
        o_ref[:, h * DV:(h + 1) * DV] = (_sigmoid(co_ref[:, h * DV:(h + 1) * DV]) * hn).astype(o_ref.dtype)


def _mlstm(y1, y2, gates, conv_w, conv_b, gate_bias, ng, *, batch, seq, tile):
    nt = seq // tile
    H, DK, DV = MLSTM_HEADS, MLSTM_DQK, MLSTM_DV
    qk_w, v_w = 2 * H * DK, H * DV
    row = lambda b, t: b * nt + t
    return pl.pallas_call(
        functools.partial(_mlstm_kernel, tile=tile),
        grid=(batch, nt),
        in_specs=[pl.BlockSpec((tile, qk_w), lambda b, t: (row(b, t), 0)),
                  pl.BlockSpec((tile, v_w), lambda b, t: (row(b, t), 1)),
                  pl.BlockSpec((tile, v_w), lambda b, t: (row(b, t), 0)),
                  pl.BlockSpec((tile, LANES), lambda b, t: (row(b, t), 0)),
                  pl.BlockSpec((MLSTM_CONV, qk_w), lambda b, t: (0, 0)),
                  pl.BlockSpec((1, qk_w), lambda b, t: (0, 0)),
                  pl.BlockSpec((1, LANES), lambda b, t: (0, 0)),
                  pl.BlockSpec((1, DV), lambda b, t: (0, 0))],
        out_specs=pl.BlockSpec((tile, v_w), lambda b, t: (row(b, t), 0)),
        out_shape=jax.ShapeDtypeStruct((batch * seq, v_w), BF16),
        scratch_shapes=[pltpu.VMEM((H, DK, DV + LANES), F32),
                        pltpu.VMEM((8, LANES), F32),
                        pltpu.VMEM((8, qk_w), F32),
                        pltpu.VMEM((tile + 8, qk_w), F32)],
        compiler_params=_cparams("parallel", "arbitrary"),
        name="mlstm",
    )(y1, y1, y2, gates, conv_w, conv_b, gate_bias, ng)


def _rope128(x, c_ref, s1_ref, s2_ref):
    return x * c_ref[...] + pltpu.roll(x, 32, 1) * s1_ref[...] + pltpu.roll(x, LANES - 32, 1) * s2_ref[...]


def _rms(x, g):
    return x * lax.rsqrt(jnp.mean(x * x, axis=-1, keepdims=True) + 1e-6) * g


MLA_SCORE_SCALE = (MLA_NOPE + MLA_ROPE) ** -0.5 * 1.4426950408889634


def _mla_q_kernel(x_ref, g_ref, w_ref, c_ref, s1_ref, s2_ref, o_ref):
    q = _bdot(_rms(x_ref[...], g_ref[...]), w_ref[...]) * MLA_SCORE_SCALE
    for h in range(MLA_HEADS):
        o_ref[:, 256 * h:256 * h + 128] = q[:, 256 * h:256 * h + 128].astype(o_ref.dtype)
        o_ref[:, 256 * h + 128:256 * h + 256] = _rope128(
            q[:, 256 * h + 128:256 * h + 256], c_ref, s1_ref, s2_ref).astype(o_ref.dtype)


def _mla_kv_kernel(x_ref, g_ref, wk_ref, wvt_ref, kr_ref, c_ref, s1_ref, s2_ref, k_ref, vt_ref):
    xn = _rms(x_ref[...], g_ref[...]).astype(BF16)
    kn = jnp.dot(xn, wk_ref[...], preferred_element_type=F32)
    vt_ref[...] = lax.dot_general(wvt_ref[...], xn, _NT, preferred_element_type=F32).astype(vt_ref.dtype)
    kr = _rope128(kr_ref[...], c_ref, s1_ref, s2_ref).astype(k_ref.dtype)
    for h in range(MLA_HEADS):
        k_ref[:, 256 * h:256 * h + 128] = kn[:, 128 * h:128 * h + 128].astype(k_ref.dtype)
        k_ref[:, 256 * h + 128:256 * h + 256] = kr


def _mla_proj(y, qcb, gates, qg, kvg, wq, wk, wvt, tabs, *, tm):
    m = y.shape[0]
    H = MLA_HEADS
    tab_specs = [pl.BlockSpec((tm, LANES), lambda i: (i, 0))] * 3
    full = lambda a: pl.BlockSpec(a.shape, lambda i: (0, 0))
    qf = pl.pallas_call(
        _mla_q_kernel,
        grid=(m // tm,),
        in_specs=[pl.BlockSpec((tm, MLA_Q_RANK), lambda i: (i, qcb)), full(qg), full(wq)] + tab_specs,
        out_specs=pl.BlockSpec((tm, H * 256), lambda i: (i, 0)),
        out_shape=jax.ShapeDtypeStruct((m, H * 256), BF16),
        compiler_params=_cparams("parallel"),
        name="mla_q",
    )(y, qg, wq, *tabs)
    kf, vt = pl.pallas_call(
        _mla_kv_kernel,
        grid=(m // tm,),
        in_specs=[pl.BlockSpec((tm, MLA_KV_RANK), lambda i: (i, qcb + 1)), full(kvg), full(wk), full(wvt),
                  pl.BlockSpec((tm, LANES), lambda i: (i, 0))] + tab_specs,
        out_specs=[pl.BlockSpec((tm, H * 256), lambda i: (i, 0)),
                   pl.BlockSpec((H * MLA_DV, tm), lambda i: (0, i))],
        out_shape=[jax.ShapeDtypeStruct((m, H * 256), BF16), jax.ShapeDtypeStruct((H * MLA_DV, m), BF16)],
        compiler_params=_cparams("parallel"),
        name="mla_kv",
    )(y, kvg, wk, wvt, gates, *tabs)
    return qf, kf, vt


def _mla_attn_kernel(q_ref, k_ref, vt_ref, o_ref, s_ref, *, tq, hg):
    i = pl.program_id(2)
    DV = MLA_DV

    def scores(j, slot):
        start = pl.multiple_of(j * tq, tq)
        for g in range(hg):
            s_ref[slot, g] = lax.dot_general(k_ref[pl.ds(start, tq), 256 * g:256 * (g + 1)],
                                             q_ref[:, 256 * g:256 * (g + 1)], _NT,
                                             preferred_element_type=F32)

    def step(j, slot, carry, masked):
        start = pl.multiple_of(j * tq, tq)
        out = []
        for g in range(hg):
            m, l, acc = carry[g]
            vtb = vt_ref[DV * g:DV * (g + 1), pl.ds(start, tq)]
            st = s_ref[slot, g]
            if masked:
                kk = lax.broadcasted_iota(jnp.int32, (tq, tq), 0)
                qq = lax.broadcasted_iota(jnp.int32, (tq, tq), 1)
                st = jnp.where(kk <= qq, st, NEG_INF)
            m_new = jnp.maximum(m, jnp.max(st, axis=0, keepdims=True))
            pt = jnp.exp2(st - m_new)
            a = jnp.exp2(m - m_new)
            l = a * l + jnp.sum(pt, axis=0, keepdims=True)
            acc = a * acc + jnp.dot(vtb, pt.astype(BF16), preferred_element_type=F32)
            out.append((m_new, l, acc))
        return tuple(out)

    init = tuple((jnp.full((1, tq), NEG_INF, F32), jnp.zeros((1, tq), F32), jnp.zeros((DV, tq), F32))
                 for _ in range(hg))

    def pair(jj, carry):
        scores(2 * jj + 1, 1)
        carry = step(2 * jj, 0, carry, False)
        scores(2 * jj + 2, 0)
        return step(2 * jj + 1, 1, carry, False)

    def finish(carry):
        for g in range(hg):
            _, l, acc = carry[g]
            o_ref[:, DV * g:DV * (g + 1)] = (acc / l).T.astype(o_ref.dtype)

    scores(0, 0)
    carry = lax.fori_loop(0, i // 2, pair, init)

    @pl.when(i % 2 == 0)
    def _():
        finish(step(i, 0, carry, True))

    @pl.when(i % 2 == 1)
    def _():
        scores(i, 1)
        finish(step(i, 1, step(i - 1, 0, carry, False), True))


def _mla_attn(qf, kf, vt, *, batch, seq, tq, hg=4):
    H = MLA_HEADS
    nq = seq // tq
    return pl.pallas_call(
        functools.partial(_mla_attn_kernel, tq=tq, hg=hg),
        grid=(batch, H // hg, nq),
        in_specs=[pl.BlockSpec((tq, 256 * hg), lambda b, h, i: (b * nq + i, h)),
                  pl.BlockSpec((seq, 256 * hg), lambda b, h, i: (b, h)),
                  pl.BlockSpec((MLA_DV * hg, seq), lambda b, h, i: (h, b))],
        out_specs=pl.BlockSpec((tq, MLA_DV * hg), lambda b, h, i: (b * nq + i, h)),
        out_shape=jax.ShapeDtypeStruct((batch * seq, H * MLA_DV), BF16),
        scratch_shapes=[pltpu.VMEM((2, hg, tq, tq), F32)],
        compiler_params=_cparams("parallel", "parallel", "arbitrary"),
        name="mla_attn",
    )(qf, kf, vt)


def _tile_for(n, pref):
    t = pref
    while n % t:
        t //= 2
    return t


def _even_mixer(x_bf, wt_all, j, wg2, bg, ng, *, batch, seq):
    m = x_bf.shape[0]
    gq_gk_gv = 2 * GLA_HEADS * GLA_DK + GLA_HEADS * GLA_DV
    rest0 = gq_gk_gv + GLA_GATE_RANK
    tm = _tile_for(m, 2048)
    mm = functools.partial(_matmul_t, x_bf, wt_all, layer=j, tm=tm, out_dtype=F32)
    y1, gts = mm(row0=0, ncols=gq_gk_gv, tn=512, side_rows=(gq_gk_gv,))
    y2 = mm(row0=rest0, ncols=wt_all.shape[1] - rest0, tn=512)
    o_a = _gla(y1, y2, gts, wg2, bg.reshape(1, -1), ng.reshape(1, -1), batch=batch, seq=seq,
               tile=_tile_for(seq, 512))
    o_b = _dilated(y2, GLA_HEADS * GLA_DV, batch=batch, seq=seq)
    return o_a, o_b


def _rope_tables(positions):
    inv_freq = ROPE_THETA ** (-jnp.arange(0, MLA_ROPE, 2, dtype=F32) / MLA_ROPE)
    ang = positions.astype(F32).reshape(-1, 1) * inv_freq
    cos, sin = jnp.cos(ang), jnp.sin(ang)
    z32 = jnp.zeros_like(cos)
    z64 = jnp.zeros((cos.shape[0], LANES - MLA_ROPE), F32)
    c = jnp.concatenate([z64, cos, cos], axis=1)
    s1 = jnp.concatenate([z64, z32, sin], axis=1)
    s2 = jnp.concatenate([z64, -sin, z32], axis=1)
    return c, s1, s2


def _odd_mixer(x_bf, tabs, wt_all, j, conv_w, conv_b, bi, bf, ng, qg, kvg, wuq, wukv, *, batch, seq):
    m = x_bf.shape[0]
    H = MLA_HEADS
    cq_ck_cv = 2 * MLSTM_HEADS * MLSTM_DQK + MLSTM_HEADS * MLSTM_DV
    co0 = cq_ck_cv + 2 * MLSTM_HEADS
    kr0 = co0 + MLSTM_HEADS * MLSTM_DV + MLA_Q_RANK + MLA_KV_RANK
    gate_bias = jnp.concatenate([bi, bf, jnp.zeros((LANES - 2 * MLSTM_HEADS,), F32)]).reshape(1, LANES)
    wq3 = wuq.reshape(-1, H, MLA_NOPE + MLA_ROPE)
    wq = jnp.concatenate([wq3[:, :, :MLA_NOPE], jnp.zeros(wq3.shape[:2] + (LANES - MLA_ROPE,), F32),
                          wq3[:, :, MLA_NOPE:]], axis=2).reshape(-1, H * 256).astype(BF16)
    wkv = wukv.reshape(-1, H, MLA_NOPE + MLA_DV)
    wk = wkv[:, :, :MLA_NOPE].reshape(-1, H * MLA_NOPE).astype(BF16)
    wvt = wkv[:, :, MLA_NOPE:].reshape(-1, H * MLA_DV).T.astype(BF16)
    tm = _tile_for(m, 2048)
    mm = functools.partial(_matmul_t, x_bf, wt_all, layer=j, tm=tm, out_dtype=F32)
    y1, gts, kr = mm(row0=0, ncols=cq_ck_cv, tn=512, side_rows=(cq_ck_cv, kr0 + MLA_ROPE - LANES))
    y2 = mm(row0=co0, ncols=kr0 - co0, tn=512)
    o_c = _mlstm(y1, y2, gts, conv_w, conv_b.reshape(1, -1), gate_bias, ng.reshape(1, -1),
                 batch=batch, seq=seq, tile=_tile_for(seq, 512))
    qf, kf, vt = _mla_proj(y2, MLSTM_HEADS * MLSTM_DV // MLA_Q_RANK, kr, qg.reshape(1, -1), kvg.reshape(1, -1),
                           wq, wk, wvt, tabs, tm=_tile_for(m, 512))
    o_d = _mla_attn(qf, kf, vt, batch=batch, seq=seq, tq=_tile_for(seq, 512))
    return o_c, o_d


def kernel(x, positions, even_w_in, even_gla_wg2, even_gla_bg, even_gla_norm_g, even_w_o, odd_w_in, odd_conv_w, odd_conv_b, odd_mlstm_bi, odd_mlstm_bf, odd_mlstm_norm_g, odd_mla_qnorm_g, odd_mla_kvnorm_g, odd_mla_wuq, odd_mla_wukv, odd_w_o, ln1_g, ln1_b, ffn_wgu, ffn_wd, ln2_g, ln2_b):
    batch, seq, d = x.shape
    m = batch * seq
    xf = x.reshape(m, d)
    xb = xf.astype(BF16)
    tabs = _rope_tables(positions)
    even_wt = jnp.swapaxes(even_w_in, 1, 2)
    odd_wt = jnp.swapaxes(odd_w_in, 1, 2)
    depth = ln1_g.shape[0]
    tm_ln = _tile_for(m, 512)
    for l in range(depth):
        j = l // 2
        if l % 2 == 0:
            parts = _even_mixer(xb, even_wt, j, even_gla_wg2[j], even_gla_bg[j], even_gla_norm_g[j],
                                batch=batch, seq=seq)
            w_o = even_w_o
        else:
            parts = _odd_mixer(xb, tabs, odd_wt, j, odd_conv_w[j], odd_conv_b[j], odd_mlstm_bi[j],
                               odd_mlstm_bf[j], odd_mlstm_norm_g[j], odd_mla_qnorm_g[j], odd_mla_kvnorm_g[j],
                               odd_mla_wuq[j], odd_mla_wukv[j], batch=batch, seq=seq)
            w_o = odd_w_o
        xf, xb = _mm_ln(parts, w_o, xf, ln1_g[l].reshape(1, -1), ln1_b[l].reshape(1, -1), tm=tm_ln, layer=j)
        hid, wd_bf = _ffn_up(xb, ffn_wgu, ffn_wd, tm=_tile_for(m, 2048), tn=512, layer=l)
        xf, xb = _mm_ln((hid,), wd_bf, xf, ln2_g[l].reshape(1, -1), ln2_b[l].reshape(1, -1),
                        tm=_tile_for(m, 1024), nk=4)
    return xf.reshape(batch, seq, d)
```

```python
import functools

import jax
import jax.numpy as jnp
from jax import lax
from jax.experimental import pallas as pl
from jax.experimental.pallas import tpu as pltpu

F32 = jnp.float32
BF16 = jnp.bfloat16

D_MODEL = 2048
DEPTH = 4
GLA_HEADS, GLA_DK, GLA_DV, GLA_GATE_RANK, GLA_TAU, GLA_CHUNK = 4, 128, 256, 16, 16.0, 64
DIL_PAIRS = ((128, 1), (512, 4), (2048, 16))
DIL_HEADS, DIL_HEAD_DIM, DIL_BLOCK = 4, 128, 128
MLSTM_HEADS, MLSTM_DQK, MLSTM_DV, MLSTM_CONV, MLSTM_CHUNK = 4, 128, 256, 4, 64
MLA_HEADS, MLA_Q_RANK, MLA_KV_RANK, MLA_NOPE, MLA_ROPE, MLA_DV = 8, 512, 512, 128, 64, 128
ROPE_THETA = 10000.0
FFN_HIDDEN = 5632
DEEPNORM_ALPHA = (2.0 * DEPTH) ** 0.25

LANES = 128
VMEM_LIMIT = 56 * 1024 * 1024
NEG_INF = float("-inf")

_NT = (((1,), (1,)), ((), ()))


def _cparams(*sem):
    return pltpu.CompilerParams(dimension_semantics=sem, vmem_limit_bytes=VMEM_LIMIT)


def _bdot(a, b):
    return jnp.dot(a.astype(BF16), b.astype(BF16), preferred_element_type=F32)


def _bdot_nt(a, b):
    return lax.dot_general(a.astype(BF16), b.astype(BF16), _NT, preferred_element_type=F32)


def _bdot_tn(a, b):
    return jnp.dot(a.astype(F32).T.astype(BF16), b.astype(BF16), preferred_element_type=F32)


def _split2(a):
    hi = a.astype(BF16)
    lo = (a - hi.astype(F32)).astype(BF16)
    return hi, lo


def _dot3(a, b):
    ah, al = _split2(a)
    bh, bl = _split2(b)
    f = functools.partial(jnp.dot, preferred_element_type=F32)
    return f(ah, bh) + (f(ah, bl) + f(al, bh))


def _chunk_cumsum(tril_b, x):
    hi, lo = _split2(x)
    lo2 = (x - hi.astype(F32) - lo.astype(F32)).astype(BF16)
    f = functools.partial(jnp.dot, preferred_element_type=F32)
    return f(tril_b, hi) + (f(tril_b, lo) + f(tril_b, lo2))


def _tril(n):
    r = lax.broadcasted_iota(jnp.int32, (n, n), 0)
    c = lax.broadcasted_iota(jnp.int32, (n, n), 1)
    return c <= r


def _log_sigmoid(x):
    return jnp.minimum(x, 0.0) - jnp.log1p(jnp.exp(-jnp.abs(x)))


def _sigmoid(x):
    return 1.0 / (1.0 + jnp.exp(-x))


def _silu(x):
    return x * _sigmoid(x)


def _wspec(w, layer, rows, cols, index_map):
    if w.ndim == 2:
        return pl.BlockSpec((rows, cols), index_map)
    return pl.BlockSpec((None, rows, cols), lambda *ids: (layer,) + tuple(index_map(*ids)))


def _mm_nt_kernel(*refs, nside):
    a_ref, wt_ref, side_w = refs[0], refs[1], refs[2:2 + nside]
    o_ref, side_o = refs[2 + nside], refs[3 + nside:]

    def nt(w_ref):
        return lax.dot_general(a_ref[...], w_ref[0].astype(BF16), _NT, preferred_element_type=F32)

    o_ref[...] = nt(wt_ref).astype(o_ref.dtype)

    @pl.when(pl.program_id(1) == 0)
    def _():
        for w_ref, s_ref in zip(side_w, side_o):
            s_ref[...] = nt(w_ref).astype(s_ref.dtype)


def _matmul_t(a, wt, *, layer, row0, ncols, tm, tn, out_dtype, side_rows=()):
    m, k = a.shape
    wspec = lambda rows, imap: pl.BlockSpec((pl.Element(1), pl.Element(rows), pl.Element(k)), imap)
    outs = pl.pallas_call(
        functools.partial(_mm_nt_kernel, nside=len(side_rows)),
        grid=(m // tm, ncols // tn),
        in_specs=[pl.BlockSpec((tm, k), lambda i, j: (i, 0)),
                  wspec(tn, lambda i, j: (layer, pl.multiple_of(row0 + j * tn, 8), 0))]
                 + [wspec(LANES, lambda i, j, r=r: (layer, r, 0)) for r in side_rows],
        out_specs=[pl.BlockSpec((tm, tn), lambda i, j: (i, j))]
                  + [pl.BlockSpec((tm, LANES), lambda i, j: (i, 0)) for _ in side_rows],
        out_shape=[jax.ShapeDtypeStruct((m, ncols), out_dtype)]
                  + [jax.ShapeDtypeStruct((m, LANES), out_dtype) for _ in side_rows],
        compiler_params=_cparams("parallel", "arbitrary"),
        name="mm_nt",
    )(a, wt, *([wt] * len(side_rows)))
    return outs[0] if not side_rows else tuple(outs)


def _ffn_up_kernel(x_ref, wg_ref, wu_ref, wd_ref, o_ref, wdb_ref):
    x = x_ref[...]
    g = jnp.dot(x, wg_ref[...].astype(BF16), preferred_element_type=F32)
    u = jnp.dot(x, wu_ref[...].astype(BF16), preferred_element_type=F32)
    o_ref[...] = (_silu(g) * u).astype(o_ref.dtype)
    wdb_ref[...] = wd_ref[...].astype(wdb_ref.dtype)


def _ffn_up(x_bf, w_gu, w_d, *, tm, tn, layer):
    m, k = x_bf.shape
    hidden = w_gu.shape[-1] // 2
    nb = hidden // tn
    steps = (m // tm) * nb
    slab = hidden // steps
    assert slab * steps == hidden and slab % 16 == 0
    return pl.pallas_call(
        _ffn_up_kernel,
        grid=(m // tm, nb),
        in_specs=[pl.BlockSpec((tm, k), lambda i, j: (i, 0)),
                  _wspec(w_gu, layer, k, tn, lambda i, j: (0, j)),
                  _wspec(w_gu, layer, k, tn, lambda i, j: (0, j + nb)),
                  _wspec(w_d, layer, slab, w_d.shape[-1], lambda i, j: (i * nb + j, 0))],
        out_specs=[pl.BlockSpec((tm, tn), lambda i, j: (i, j)),
                   pl.BlockSpec((slab, w_d.shape[-1]), lambda i, j: (i * nb + j, 0))],
        out_shape=[jax.ShapeDtypeStruct((m, hidden), BF16),
                   jax.ShapeDtypeStruct((hidden, w_d.shape[-1]), BF16)],
        compiler_params=_cparams("parallel", "arbitrary"),
        name="ffn_up",
    )(x_bf, w_gu, w_gu, w_d)


def _mm_ln_kernel(*refs, widths, nk):
    a_refs, (w_ref, res_ref, g_ref, b_ref, of_ref, ob_ref) = refs[:-6], refs[-6:]

    def finish(y):
        mu = jnp.mean(y, axis=-1, keepdims=True)
        yc = y - mu
        var = jnp.mean(yc * yc, axis=-1, keepdims=True)
        out = yc * lax.rsqrt(var + 1e-5) * g_ref[...] + b_ref[...]
        of_ref[...] = out
        ob_ref[...] = out.astype(BF16)

    if nk == 1:
        y = DEEPNORM_ALPHA * res_ref[...]
        off = 0
        for a_ref, width in zip(a_refs, widths):
            y = y + jnp.dot(a_ref[...], w_ref[off:off + width, :].astype(BF16), preferred_element_type=F32)
            off += width
        finish(y)
        return

    k = pl.program_id(1)

    def step_dot():
        return jnp.dot(a_refs[0][...], w_ref[...].astype(BF16), preferred_element_type=F32)

    @pl.when(k == 0)
    def _():
        of_ref[...] = DEEPNORM_ALPHA * res_ref[...] + step_dot()

    @pl.when(k > 0)
    def _():
        of_ref[...] = of_ref[...] + step_dot()

    @pl.when(k == nk - 1)
    def _():
        finish(of_ref[...])


def _mm_ln(parts, w, res, g, b, *, tm, nk=1, layer=None):
    m = res.shape[0]
    n = w.shape[-1]
    widths = tuple(p.shape[1] for p in parts)
    assert nk == 1 or len(parts) == 1
    tk = sum(widths) // nk
    once = pl.Buffered(1)
    if nk == 1:
        a_specs = [pl.BlockSpec((tm, wd), lambda i, k: (i, 0)) for wd in widths]
        w_spec = (pl.BlockSpec((tk, n), lambda i, k: (0, 0), pipeline_mode=once) if w.ndim == 2 else
                  pl.BlockSpec((None, tk, n), lambda i, k: (layer, 0, 0), pipeline_mode=once))
        res_spec = pl.BlockSpec((tm, n), lambda i, k: (i, 0))
    else:
        a_specs = [pl.BlockSpec((tm, tk), lambda i, k: (i, k))]
        w_spec = _wspec(w, layer, tk, n, lambda i, k: (k, 0))
        res_spec = pl.BlockSpec((tm, n), lambda i, k: (i, 0), pipeline_mode=once)
    return pl.pallas_call(
        functools.partial(_mm_ln_kernel, widths=widths, nk=nk),
        grid=(m // tm, nk),
        in_specs=a_specs + [
            w_spec, res_spec,
            pl.BlockSpec((1, n), lambda i, k: (0, 0)),
            pl.BlockSpec((1, n), lambda i, k: (0, 0))],
        out_specs=[pl.BlockSpec((tm, n), lambda i, k: (i, 0)),
                   pl.BlockSpec((tm, n), lambda i, k: (i, 0))],
        out_shape=[jax.ShapeDtypeStruct((m, n), F32), jax.ShapeDtypeStruct((m, n), BF16)],
        compiler_params=_cparams("parallel", "arbitrary"),
        name="mm_ln",
    )(*parts, w, res, g, b)


def _gla_kernel(q_ref, k_ref, v_ref, r_ref, g_ref, wg2_ref, bg_ref, ng_ref, o_ref, st_ref, *, tile):
    L, H, DK, DV = GLA_CHUNK, GLA_HEADS, GLA_DK, GLA_DV

    @pl.when(pl.program_id(1) == 0)
    def _():
        st_ref[...] = jnp.zeros_like(st_ref)

    tril = _tril(L)
    tril_b = tril.astype(BF16)
    g16 = g_ref[:, :GLA_GATE_RANK]
    nchunk = tile // L
    hc = [(h, c) for c in range(nchunk) for h in range(H)]
    ksl = lambda h: slice(h * DK, (h + 1) * DK)
    vsl = lambda h: slice(h * DV, (h + 1) * DV)
    rsl = lambda c: slice(c * L, (c + 1) * L)
    log_a = [_log_sigmoid(_dot3(g16, wg2_ref[:, ksl(h)]) + bg_ref[:, ksl(h)]) / GLA_TAU for h in range(H)]
    b = {(h, c): _chunk_cumsum(tril_b, log_a[h][rsl(c)]) for h, c in hc}
    q_dec, k_inv, k_end, decay, vb = {}, {}, {}, {}, {}
    for h, c in hc:
        bb = b[h, c]
        b_last = bb[L - 1:L]
        kk = k_ref[rsl(c), ksl(h)]
        q_dec[h, c] = (q_ref[rsl(c), ksl(h)] * DK ** -0.5 * jnp.exp(bb)).astype(BF16)
        k_inv[h, c] = (kk * jnp.exp(-bb)).astype(BF16)
        k_end[h, c] = (kk * jnp.exp(b_last - bb)).astype(BF16)
        decay[h, c] = jnp.exp(b_last)
        vb[h, c] = v_ref[rsl(c), vsl(h)]
    scores = {p: _bdot_nt(q_dec[p], k_inv[p]) for p in hc}
    scores = {p: jnp.where(tril, scores[p], 0.0).astype(BF16) for p in hc}
    intra = {p: _bdot(scores[p], vb[p]) for p in hc}
    s_loc = {p: _bdot_tn(vb[p], k_end[p]) for p in hc}
    st = [st_ref[h] for h in range(H)]
    outs = {}
    for h, c in hc:
        outs[h, c] = intra[h, c] + _bdot_nt(q_dec[h, c], st[h])
        st[h] = st[h] * decay[h, c] + s_loc[h, c]
    for h in range(H):
        st_ref[h] = st[h]
        o = jnp.concatenate([outs[h, c] for c in range(nchunk)], axis=0)
        o = o * lax.rsqrt(jnp.mean(o * o, axis=-1, keepdims=True) + 1e-6) * ng_ref[...]
        o_ref[:, vsl(h)] = (o * _silu(r_ref[:, vsl(h)])).astype(o_ref.dtype)


def _gla(y1, y2, gates, wg2, bg, ng, *, batch, seq, tile):
    nt = seq // tile
    qk_w, v_w = GLA_HEADS * GLA_DK, GLA_HEADS * GLA_DV
    row = lambda b, t: b * nt + t
    return pl.pallas_call(
        functools.partial(_gla_kernel, tile=tile),
        grid=(batch, nt),
        in_specs=[pl.BlockSpec((tile, qk_w), lambda b, t: (row(b, t), 0)),
                  pl.BlockSpec((tile, qk_w), lambda b, t: (row(b, t), 1)),
                  pl.BlockSpec((tile, v_w), lambda b, t: (row(b, t), 1)),
                  pl.BlockSpec((tile, v_w), lambda b, t: (row(b, t), 0)),
                  pl.BlockSpec((tile, LANES), lambda b, t: (row(b, t), 0)),
                  pl.BlockSpec((GLA_GATE_RANK, qk_w), lambda b, t: (0, 0)),
                  pl.BlockSpec((1, qk_w), lambda b, t: (0, 0)),
                  pl.BlockSpec((1, GLA_DV), lambda b, t: (0, 0))],
        out_specs=pl.BlockSpec((tile, v_w), lambda b, t: (row(b, t), 0)),
        out_shape=jax.ShapeDtypeStruct((batch * seq, v_w), BF16),
        scratch_shapes=[pltpu.VMEM((GLA_HEADS, GLA_DV, GLA_DK), F32)],
        compiler_params=_cparams("parallel", "arbitrary"),
        name="gla",
    )(y1, y1, y1, y2, gates, wg2, bg, ng)


LOG2E = 1.4426950408889634
LN2 = 0.6931471805599453


def _dil_kernel(*refs, unit):
    ngrp = len(DIL_PAIRS)
    in_refs, o_ref, og_ref, lg_ref = refs[:5 * ngrp], refs[5 * ngrp], refs[5 * ngrp + 1], refs[5 * ngrp + 2]
    u = pl.program_id(1)
    n = DIL_BLOCK
    qscale = DIL_HEAD_DIM ** -0.5 * LOG2E
    row = lax.broadcasted_iota(jnp.int32, (n, n), 0)
    col = lax.broadcasted_iota(jnp.int32, (n, n), 1)
    prev_band, cur_band = col >= row, col <= row
    ones = jnp.ones((n, LANES), BF16)
    for g, (_, d) in enumerate(DIL_PAIRS):
        q_ref, k_ref, v_ref, kp_ref, vp_ref = in_refs[5 * g:5 * g + 5]
        sub = n * d
        blocks = []
        for s in range(unit // sub):
            for r in range(d):
                cur = pl.ds(s * sub + r, n, stride=d) if d > 1 else pl.ds(s * sub, n)
                if s == 0:
                    prv = pl.ds(r, n, stride=d) if d > 1 else pl.ds(0, n)
                    blocks.append((cur, kp_ref, vp_ref, prv, prev_band & (u > 0)))
                else:
                    prv = pl.ds((s - 1) * sub + r, n, stride=d) if d > 1 else pl.ds((s - 1) * sub, n)
                    blocks.append((cur, k_ref, v_ref, prv, prev_band))
        scores = []
        for cur, kpr, _, prv, pmask in blocks:
            q = (q_ref[cur, :] * qscale).astype(BF16)
            scores.append((jnp.where(pmask, _bdot_nt(q, kpr[prv, :]), NEG_INF),
                           jnp.where(cur_band, _bdot_nt(q, k_ref[cur, :]), NEG_INF)))
        probs = []
        for s_p, s_c in scores:
            m = jnp.maximum(jnp.max(s_p, axis=-1, keepdims=True), jnp.max(s_c, axis=-1, keepdims=True))
            probs.append((jnp.exp2(s_p - m).astype(BF16), jnp.exp2(s_c - m).astype(BF16), m))
        for (cur, _, vpr, prv, _), (p_p, p_c, m) in zip(blocks, probs):
            den = _bdot(p_p, ones) + _bdot(p_c, ones)
            og_ref[g, cur, :] = (_bdot(p_p, vpr[prv, :]) + _bdot(p_c, v_ref[cur, :])) / den
            lg_ref[g, cur, :] = m * LN2 + jnp.log(den)
    lses = [lg_ref[g] for g in range(ngrp)]
    m = functools.reduce(jnp.maximum, lses)
    ws = [jnp.exp(l - m) for l in lses]
    tot = functools.reduce(lambda a, b: a + b, ws)
    acc = sum(w * og_ref[g] for g, w in enumerate(ws))
    o_ref[...] = (acc / tot).astype(o_ref.dtype)


def _dilated(y, col0, *, batch, seq):
    ngrp, H, n = len(DIL_PAIRS), DIL_HEADS, DIL_BLOCK
    unit = max(d for _, d in DIL_PAIRS) * n
    assert seq % unit == 0
    nu = seq // unit
    cb0 = col0 // LANES
    in_specs, args = [], []
    for g, (_, d) in enumerate(DIL_PAIRS):
        sub = n * d
        per = unit // sub
        for part in range(3):
            cb = cb0 + part * ngrp * H + g * H
            in_specs.append(pl.BlockSpec((unit, LANES), lambda b, u, h, cb=cb: (b * nu + u, cb + h)))
            args.append(y)
        for part in (1, 2):
            cb = cb0 + part * ngrp * H + g * H
            in_specs.append(pl.BlockSpec(
                (sub, LANES),
                lambda b, u, h, cb=cb, per=per: (jnp.maximum(b * nu * per + u * per - 1, 0), cb + h)))
            args.append(y)
    return pl.pallas_call(
        functools.partial(_dil_kernel, unit=unit),
        grid=(batch, nu, H),
        in_specs=in_specs,
        out_specs=pl.BlockSpec((unit, LANES), lambda b, u, h: (b * nu + u, h)),
        out_shape=jax.ShapeDtypeStruct((batch * seq, H * LANES), BF16),
        scratch_shapes=[pltpu.VMEM((ngrp, unit, LANES), F32), pltpu.VMEM((ngrp, unit, LANES), F32)],
        compiler_params=_cparams("parallel", "parallel", "arbitrary"),
        name="dilated",
    )(*args)


I_LANE, F_LANE = 0, MLSTM_HEADS


def _mlstm_kernel(qk_ref, v_ref, co_ref, gt_ref, cw_ref, cb_ref, gb_ref, ng_ref, o_ref,
                  c_ref, m_ref, tail_ref, xbuf_ref, *, tile):
    L, H, DK, DV = MLSTM_CHUNK, MLSTM_HEADS, MLSTM_DQK, MLSTM_DV
    KC = MLSTM_CONV
    PAD = 8

    @pl.when(pl.program_id(1) == 0)
    def _():
        c_ref[...] = jnp.zeros_like(c_ref)
        m_ref[...] = jnp.zeros_like(m_ref)
        tail_ref[...] = jnp.zeros_like(tail_ref)

    xbuf_ref[0:PAD, :] = tail_ref[...]
    xbuf_ref[PAD:PAD + tile, :] = qk_ref[...]
    tail_ref[...] = qk_ref[tile - PAD:tile, :]
    acc = cb_ref[...] + cw_ref[KC - 1:KC, :] * xbuf_ref[PAD:PAD + tile, :]
    for j in range(KC - 1):
        off = PAD - (KC - 1) + j
        acc = acc + cw_ref[j:j + 1, :] * xbuf_ref[off:off + tile, :]
    qk = _silu(acc)

    gates = gt_ref[...] + gb_ref[...]
    lane = lax.broadcasted_iota(jnp.int32, gates.shape, 1)
    z = jnp.where(lane >= F_LANE, _log_sigmoid(gates), gates)
    tril = _tril(L)
    tril_b = tril.astype(BF16)
    nchunk = tile // L
    lane_c = lax.broadcasted_iota(jnp.int32, (L, LANES), 1)
    zs, bs, wts = [], [], []
    for c in range(nchunk):
        zc = z[c * L:(c + 1) * L]
        bc = _chunk_cumsum(tril_b, zc)
        zs.append(zc)
        bs.append(bc)
        wts.append(jnp.where(lane_c >= F_LANE, bc, zc).T)
    ones_col = (lane_c == 0).astype(F32)

    hc = [(h, c) for c in range(nchunk) for h in range(H)]
    rsl = lambda c: slice(c * L, (c + 1) * L)
    b_col, inter_log, kwt, s_old, s_new, d_log = {}, {}, {}, {}, {}, {}
    m = [m_ref[h:h + 1, 0:1] for h in range(H)]
    for h, c in hc:
        bc = bs[c][:, F_LANE + h:F_LANE + h + 1]
        li_col = zs[c][:, I_LANE + h:I_LANE + h + 1]
        b_row = wts[c][F_LANE + h:F_LANE + h + 1, :]
        li_row = wts[c][I_LANE + h:I_LANE + h + 1, :]
        b_last = bs[c][L - 1:L, F_LANE + h:F_LANE + h + 1]
        d_log[h, c] = jnp.where(tril, bc - b_row + li_row, NEG_INF)
        inter_log[h, c] = bc + m[h]
        a_col = b_last - bc + li_col
        m_loc = jnp.max(a_col, axis=0, keepdims=True)
        kwt[h, c] = jnp.exp(a_col - m_loc)
        m_new = jnp.maximum(b_last + m[h], m_loc)
        s_old[h, c] = jnp.exp(b_last + m[h] - m_new)
        s_new[h, c] = jnp.exp(m_loc - m_new)
        m[h] = m_new
    qb, kb, vb, qkm, m_t = {}, {}, {}, {}, {}
    for h, c in hc:
        qb[h, c] = qk[rsl(c), h * DK:(h + 1) * DK].astype(BF16)
        kb[h, c] = qk[rsl(c), (H + h) * DK:(H + h + 1) * DK] * DK ** -0.5
        vb[h, c] = v_ref[rsl(c), h * DV:(h + 1) * DV]
    raw = {p: _bdot_nt(qb[p], kb[p]) for p in hc}
    for p in hc:
        m_t[p] = jnp.maximum(inter_log[p], jnp.max(d_log[p], axis=-1, keepdims=True))
        qkm[p] = raw[p] * jnp.exp(d_log[p] - m_t[p])
    v_aug = {p: jnp.concatenate([vb[p], ones_col], axis=1).astype(BF16) for p in hc}
    intra = {p: _bdot(qkm[p], v_aug[p]) for p in hc}
    c_loc = {p: _bdot_tn(kb[p] * kwt[p], v_aug[p]) for p in hc}
    ct = [c_ref[h] for h in range(H)]
    outs = {}
    for h, c in hc:
        p = (h, c)
        inter_scale = jnp.exp(inter_log[p] - m_t[p])
        tot = intra[p] + inter_scale * _bdot(qb[p], ct[h])
        outs[p] = tot[:, :DV] / jnp.maximum(jnp.abs(tot[:, DV:DV + 1]), jnp.exp(-m_t[p]))
        ct[h] = ct[h] * s_old[p] + c_loc[p] * s_new[p]
    for h in range(H):
        c_ref[h] = ct[h]
        m_ref[h:h + 1, :] = jnp.broadcast_to(m[h], (1, LANES))
        hcat = jnp.concatenate([outs[h, c] for c in range(nchunk)], axis=0)
        mu = jnp.mean(hcat, axis=-1, keepdims=True)
        hcen = hcat - mu
        var = jnp.mean(hcen * hcen, axis=-1, keepdims=True)
        hn = hcen * lax.rsqrt(var + 1e-5) * ng_ref[...]
        o_ref[:, h * DV:(h + 1) * DV] = (_sigmoid(co_ref[:, h * DV:(h + 1) * DV]) * hn).astype(o_ref.dtype)


def _mlstm(y1, y2, gates, conv_w, conv_b, gate_bias, ng, *, batch, seq, tile):
    nt = seq // tile
    H, DK, DV = MLSTM_HEADS, MLSTM_DQK, MLSTM_DV
    qk_w, v_w = 2 * H * DK, H * DV
    row = lambda b, t: b * nt + t
    return pl.pallas_call(
        functools.partial(_mlstm_kernel, tile=tile),
        grid=(batch, nt),
        in_specs=[pl.BlockSpec((tile, qk_w), lambda b, t: (row(b, t), 0)),
                  pl.BlockSpec((tile, v_w), lambda b, t: (row(b, t), 1)),
                  pl.BlockSpec((tile, v_w), lambda b, t: (row(b, t), 0)),
                  pl.BlockSpec((tile, LANES), lambda b, t: (row(b, t), 0)),
                  pl.BlockSpec((MLSTM_CONV, qk_w), lambda b, t: (0, 0)),
                  pl.BlockSpec((1, qk_w), lambda b, t: (0, 0)),
                  pl.BlockSpec((1, LANES), lambda b, t: (0, 0)),
                  pl.BlockSpec((1, DV), lambda b, t: (0, 0))],
        out_specs=pl.BlockSpec((tile, v_w), lambda b, t: (row(b, t), 0)),
        out_shape=jax.ShapeDtypeStruct((batch * seq, v_w), BF16),
        scratch_shapes=[pltpu.VMEM((H, DK, DV + LANES), F32),
                        pltpu.VMEM((8, LANES), F32),
                        pltpu.VMEM((8, qk_w), F32),
                        pltpu.VMEM((tile + 8, qk_w), F32)],
        compiler_params=_cparams("parallel", "arbitrary"),
        name="mlstm",
    )(y1, y1, y2, gates, conv_w, conv_b, gate_bias, ng)


def _rope128(x, c_ref, s1_ref, s2_ref):
    return x * c_ref[...] + pltpu.roll(x, 32, 1) * s1_ref[...] + pltpu.roll(x, LANES - 32, 1) * s2_ref[...]


def _rms(x, g):
    return x * lax.rsqrt(jnp.mean(x * x, axis=-1, keepdims=True) + 1e-6) * g


MLA_SCORE_SCALE = (MLA_NOPE + MLA_ROPE) ** -0.5 * 1.4426950408889634


def _mla_q_kernel(x_ref, g_ref, w_ref, c_ref, s1_ref, s2_ref, o_ref):
    q = _bdot(_rms(x_ref[...], g_ref[...]), w_ref[...]) * MLA_SCORE_SCALE
    for h in range(MLA_HEADS):
        o_ref[:, 256 * h:256 * h + 128] = q[:, 256 * h:256 * h + 128].astype(o_ref.dtype)
        o_ref[:, 256 * h + 128:256 * h + 256] = _rope128(
            q[:, 256 * h + 128:256 * h + 256], c_ref, s1_ref, s2_ref).astype(o_ref.dtype)


def _mla_kv_kernel(x_ref, g_ref, wk_ref, wvt_ref, kr_ref, c_ref, s1_ref, s2_ref, k_ref, vt_ref):
    xn = _rms(x_ref[...], g_ref[...]).astype(BF16)
    kn = jnp.dot(xn, wk_ref[...], preferred_element_type=F32)
    vt_ref[...] = lax.dot_general(wvt_ref[...], xn, _NT, preferred_element_type=F32).astype(vt_ref.dtype)
    kr = _rope128(kr_ref[...], c_ref, s1_ref, s2_ref).astype(k_ref.dtype)
    for h in range(MLA_HEADS):
        k_ref[:, 256 * h:256 * h + 128] = kn[:, 128 * h:128 * h + 128].astype(k_ref.dtype)
        k_ref[:, 256 * h + 128:256 * h + 256] = kr


def _mla_proj(y, qcb, gates, qg, kvg, wq, wk, wvt, tabs, *, tm):
    m = y.shape[0]
    H = MLA_HEADS
    tab_specs = [pl.BlockSpec((tm, LANES), lambda i: (i, 0))] * 3
    full = lambda a: pl.BlockSpec(a.shape, lambda i: (0, 0))
    qf = pl.pallas_call(
        _mla_q_kernel,
        grid=(m // tm,),
        in_specs=[pl.BlockSpec((tm, MLA_Q_RANK), lambda i: (i, qcb)), full(qg), full(wq)] + tab_specs,
        out_specs=pl.BlockSpec((tm, H * 256), lambda i: (i, 0)),
        out_shape=jax.ShapeDtypeStruct((m, H * 256), BF16),
        compiler_params=_cparams("parallel"),
        name="mla_q",
    )(y, qg, wq, *tabs)
    kf, vt = pl.pallas_call(
        _mla_kv_kernel,
        grid=(m // tm,),
        in_specs=[pl.BlockSpec((tm, MLA_KV_RANK), lambda i: (i, qcb + 1)), full(kvg), full(wk), full(wvt),
                  pl.BlockSpec((tm, LANES), lambda i: (i, 0))] + tab_specs,
        out_specs=[pl.BlockSpec((tm, H * 256), lambda i: (i, 0)),
                   pl.BlockSpec((H * MLA_DV, tm), lambda i: (0, i))],
        out_shape=[jax.ShapeDtypeStruct((m, H * 256), BF16), jax.ShapeDtypeStruct((H * MLA_DV, m), BF16)],
        compiler_params=_cparams("parallel"),
        name="mla_kv",
    )(y, kvg, wk, wvt, gates, *tabs)
    return qf, kf, vt


def _mla_attn_kernel(q_ref, k_ref, vt_ref, o_ref, s_ref, mx_ref, *, tq, hg):
    i = pl.program_id(2)
    DV = MLA_DV

    def scores(j, slot):
        start = pl.multiple_of(j * tq, tq)
        for g in range(hg):
            st = lax.dot_general(k_ref[pl.ds(start, tq), 256 * g:256 * (g + 1)],
                                 q_ref[:, 256 * g:256 * (g + 1)], _NT, preferred_element_type=F32)
            s_ref[slot, g] = st
            mx_ref[slot, g] = jnp.max(st, axis=0, keepdims=True)

    def step(j, slot, carry, masked):
        start = pl.multiple_of(j * tq, tq)
        out = []
        for g in range(hg):
            m, l, acc = carry[g]
            vtb = vt_ref[DV * g:DV * (g + 1), pl.ds(start, tq)]
            st = s_ref[slot, g]
            if masked:
                kk = lax.broadcasted_iota(jnp.int32, (tq, tq), 0)
                qq = lax.broadcasted_iota(jnp.int32, (tq, tq), 1)
                st = jnp.where(kk <= qq, st, NEG_INF)
                m_new = jnp.maximum(m, jnp.max(st, axis=0, keepdims=True))
            else:
                m_new = jnp.maximum(m, mx_ref[slot, g])
            pt = jnp.exp2(st - m_new)
            a = jnp.exp2(m - m_new)
            l = a * l + jnp.sum(pt, axis=0, keepdims=True)
            acc = a * acc + jnp.dot(vtb, pt.astype(BF16), preferred_element_type=F32)
            out.append((m_new, l, acc))
        return tuple(out)

    init = tuple((jnp.full((1, tq), NEG_INF, F32), jnp.zeros((1, tq), F32), jnp.zeros((DV, tq), F32))
                 for _ in range(hg))

    def pair(jj, carry):
        scores(2 * jj + 1, 1)
        carry = step(2 * jj, 0, carry, False)
        scores(2 * jj + 2, 0)
        return step(2 * jj + 1, 1, carry, False)

    def finish(carry):
        for g in range(hg):
            _, l, acc = carry[g]
            o_ref[:, DV * g:DV * (g + 1)] = (acc / l).T.astype(o_ref.dtype)

    scores(0, 0)
    carry = lax.fori_loop(0, i // 2, pair, init)

    @pl.when(i % 2 == 0)
    def _():
        finish(step(i, 0, carry, True))

    @pl.when(i % 2 == 1)
    def _():
        scores(i, 1)
        finish(step(i, 1, step(i - 1, 0, carry, False), True))


def _mla_attn(qf, kf, vt, *, batch, seq, tq, hg=4):
    H = MLA_HEADS
    nq = seq // tq
    return pl.pallas_call(
        functools.partial(_mla_attn_kernel, tq=tq, hg=hg),
        grid=(batch, H // hg, nq),
        in_specs=[pl.BlockSpec((tq, 256 * hg), lambda b, h, i: (b * nq + i, h)),
                  pl.BlockSpec((seq, 256 * hg), lambda b, h, i: (b, h)),
                  pl.BlockSpec((MLA_DV * hg, seq), lambda b, h, i: (h, b))],
        out_specs=pl.BlockSpec((tq, MLA_DV * hg), lambda b, h, i: (b * nq + i, h)),
        out_shape=jax.ShapeDtypeStruct((batch * seq, H * MLA_DV), BF16),
        scratch_shapes=[pltpu.VMEM((2, hg, tq, tq), F32), pltpu.VMEM((2, hg, 1, tq), F32)],
        compiler_params=_cparams("parallel", "parallel", "arbitrary"),
        name="mla_attn",
    )(qf, kf, vt)


def _tile_for(n, pref):
    t = pref
    while n % t:
        t //= 2
    return t


def _even_mixer(x_bf, wt_all, j, wg2, bg, ng, *, batch, seq):
    m = x_bf.shape[0]
    gq_gk_gv = 2 * GLA_HEADS * GLA_DK + GLA_HEADS * GLA_DV
    rest0 = gq_gk_gv + GLA_GATE_RANK
    tm = _tile_for(m, 2048)
    mm = functools.partial(_matmul_t, x_bf, wt_all, layer=j, tm=tm, out_dtype=F32)
    y1, gts = mm(row0=0, ncols=gq_gk_gv, tn=512, side_rows=(gq_gk_gv,))
    y2 = mm(row0=rest0, ncols=wt_all.shape[1] - rest0, tn=512)
    o_a = _gla(y1, y2, gts, wg2, bg.reshape(1, -1), ng.reshape(1, -1), batch=batch, seq=seq,
               tile=_tile_for(seq, 512))
    o_b = _dilated(y2, GLA_HEADS * GLA_DV, batch=batch, seq=seq)
    return o_a, o_b


def _rope_tables(positions):
    inv_freq = ROPE_THETA ** (-jnp.arange(0, MLA_ROPE, 2, dtype=F32) / MLA_ROPE)
    ang = positions.astype(F32).reshape(-1, 1) * inv_freq
    cos, sin = jnp.cos(ang), jnp.sin(ang)
    z32 = jnp.zeros_like(cos)
    z64 = jnp.zeros((cos.shape[0], LANES - MLA_ROPE), F32)
    c = jnp.concatenate([z64, cos, cos], axis=1)
    s1 = jnp.concatenate([z64, z32, sin], axis=1)
    s2 = jnp.concatenate([z64, -sin, z32], axis=1)
    return c, s1, s2


def _odd_mixer(x_bf, tabs, wt_all, j, conv_w, conv_b, bi, bf, ng, qg, kvg, wuq, wukv, *, batch, seq):
    m = x_bf.shape[0]
    H = MLA_HEADS
    cq_ck_cv = 2 * MLSTM_HEADS * MLSTM_DQK + MLSTM_HEADS * MLSTM_DV
    co0 = cq_ck_cv + 2 * MLSTM_HEADS
    kr0 = co0 + MLSTM_HEADS * MLSTM_DV + MLA_Q_RANK + MLA_KV_RANK
    gate_bias = jnp.concatenate([bi, bf, jnp.zeros((LANES - 2 * MLSTM_HEADS,), F32)]).reshape(1, LANES)
    wq3 = wuq.reshape(-1, H, MLA_NOPE + MLA_ROPE)
    wq = jnp.concatenate([wq3[:, :, :MLA_NOPE], jnp.zeros(wq3.shape[:2] + (LANES - MLA_ROPE,), F32),
                          wq3[:, :, MLA_NOPE:]], axis=2).reshape(-1, H * 256).astype(BF16)
    wkv = wukv.reshape(-1, H, MLA_NOPE + MLA_DV)
    wk = wkv[:, :, :MLA_NOPE].reshape(-1, H * MLA_NOPE).astype(BF16)
    wvt = wkv[:, :, MLA_NOPE:].reshape(-1, H * MLA_DV).T.astype(BF16)
    tm = _tile_for(m, 2048)
    mm = functools.partial(_matmul_t, x_bf, wt_all, layer=j, tm=tm, out_dtype=F32)
    y1, gts, kr = mm(row0=0, ncols=cq_ck_cv, tn=512, side_rows=(cq_ck_cv, kr0 + MLA_ROPE - LANES))
    y2 = mm(row0=co0, ncols=kr0 - co0, tn=512)
    o_c = _mlstm(y1, y2, gts, conv_w, conv_b.reshape(1, -1), gate_bias, ng.reshape(1, -1),
                 batch=batch, seq=seq, tile=_tile_for(seq, 512))
    qf, kf, vt = _mla_proj(y2, MLSTM_HEADS * MLSTM_DV // MLA_Q_RANK, kr, qg.reshape(1, -1), kvg.reshape(1, -1),
                           wq, wk, wvt, tabs, tm=_tile_for(m, 512))
    o_d = _mla_attn(qf, kf, vt, batch=batch, seq=seq, tq=_tile_for(seq, 512))
    return o_c, o_d


def kernel(x, positions, even_w_in, even_gla_wg2, even_gla_bg, even_gla_norm_g, even_w_o, odd_w_in, odd_conv_w, odd_conv_b, odd_mlstm_bi, odd_mlstm_bf, odd_mlstm_norm_g, odd_mla_qnorm_g, odd_mla_kvnorm_g, odd_mla_wuq, odd_mla_wukv, odd_w_o, ln1_g, ln1_b, ffn_wgu, ffn_wd, ln2_g, ln2_b):
    batch, seq, d = x.shape
    m = batch * seq
    xf = x.reshape(m, d)
    xb = xf.astype(BF16)
    tabs = _rope_tables(positions)
    even_wt = jnp.swapaxes(even_w_in, 1, 2)
    odd_wt = jnp.swapaxes(odd_w_in, 1, 2)
    depth = ln1_g.shape[0]
    tm_ln = _tile_for(m, 512)
    for l in range(depth):
        j = l // 2
        if l % 2 == 0:
            parts = _even_mixer(xb, even_wt, j, even_gla_wg2[j], even_gla_bg[j], even_gla_norm_g[j],
                                batch=batch, seq=seq)
            w_o = even_w_o
        else:
            parts = _odd_mixer(xb, tabs, odd_wt, j, odd_conv_w[j], odd_conv_b[j], odd_mlstm_bi[j],
                               odd_mlstm_bf[j], odd_mlstm_norm_g[j], odd_mla_qnorm_g[j], odd_mla_kvnorm_g[j],
                               odd_mla_wuq[j], odd_mla_wukv[j], batch=batch, seq=seq)
            w_o = odd_w_o
        xf, xb = _mm_ln(parts, w_o, xf, ln1_g[l].reshape(1, -1), ln1_b[l].reshape(1, -1), tm=tm_ln, layer=j)
        hid, wd_bf = _ffn_up(xb, ffn_wgu, ffn_wd, tm=_tile_for(m, 2048), tn=512, layer=l)
        xf, xb = _mm_ln((hid,), wd_bf, xf, ln2_g[l].reshape(1, -1), ln2_b[l].reshape(1, -1),
                        tm=_tile_for(m, 256))
    return xf.reshape(batch, seq, d)
```

```python
import functools

import jax
import jax.numpy as jnp
from jax import lax
from jax.experimental import pallas as pl
from jax.experimental.pallas import tpu as pltpu

F32 = jnp.float32
BF16 = jnp.bfloat16

D_MODEL = 2048
DEPTH = 4
GLA_HEADS, GLA_DK, GLA_DV, GLA_GATE_RANK, GLA_TAU, GLA_CHUNK = 4, 128, 256, 16, 16.0, 64
DIL_PAIRS = ((128, 1), (512, 4), (2048, 16))
DIL_HEADS, DIL_HEAD_DIM, DIL_BLOCK = 4, 128, 128
MLSTM_HEADS, MLSTM_DQK, MLSTM_DV, MLSTM_CONV, MLSTM_CHUNK = 4, 128, 256, 4, 64
MLA_HEADS, MLA_Q_RANK, MLA_KV_RANK, MLA_NOPE, MLA_ROPE, MLA_DV = 8, 512, 512, 128, 64, 128
ROPE_THETA = 10000.0
FFN_HIDDEN = 5632
DEEPNORM_ALPHA = (2.0 * DEPTH) ** 0.25

LANES = 128
VMEM_LIMIT = 56 * 1024 * 1024
NEG_INF = float("-inf")

_NT = (((1,), (1,)), ((), ()))


def _cparams(*sem):
    return pltpu.CompilerParams(dimension_semantics=sem, vmem_limit_bytes=VMEM_LIMIT)


def _bdot(a, b):
    return jnp.dot(a.astype(BF16), b.astype(BF16), preferred_element_type=F32)


def _bdot_nt(a, b):
    return lax.dot_general(a.astype(BF16), b.astype(BF16), _NT, preferred_element_type=F32)


def _bdot_tn(a, b):
    return jnp.dot(a.astype(F32).T.astype(BF16), b.astype(BF16), preferred_element_type=F32)


def _split2(a):
    hi = a.astype(BF16)
    lo = (a - hi.astype(F32)).astype(BF16)
    return hi, lo


def _dot3(a, b):
    ah, al = _split2(a)
    bh, bl = _split2(b)
    f = functools.partial(jnp.dot, preferred_element_type=F32)
    return f(ah, bh) + (f(ah, bl) + f(al, bh))


def _chunk_cumsum(tril_b, x):
    hi, lo = _split2(x)
    lo2 = (x - hi.astype(F32) - lo.astype(F32)).astype(BF16)
    f = functools.partial(jnp.dot, preferred_element_type=F32)
    return f(tril_b, hi) + (f(tril_b, lo) + f(tril_b, lo2))


def _tril(n):
    r = lax.broadcasted_iota(jnp.int32, (n, n), 0)
    c = lax.broadcasted_iota(jnp.int32, (n, n), 1)
    return c <= r


def _log_sigmoid(x):
    return jnp.minimum(x, 0.0) - jnp.log1p(jnp.exp(-jnp.abs(x)))


def _sigmoid(x):
    return 1.0 / (1.0 + jnp.exp(-x))


def _silu(x):
    return x * _sigmoid(x)


def _wspec(w, layer, rows, cols, index_map):
    if w.ndim == 2:
        return pl.BlockSpec((rows, cols), index_map)
    return pl.BlockSpec((None, rows, cols), lambda *ids: (layer,) + tuple(index_map(*ids)))


def _mm_nt_kernel(*refs, nside, ncast):
    a_ref, wt_ref, side_w, cast_in = refs[0], refs[1], refs[2:2 + nside], refs[2 + nside:2 + nside + ncast]
    outs = refs[2 + nside + ncast:]
    o_ref, side_o, cast_out = outs[0], outs[1:1 + nside], outs[1 + nside:]

    def nt(w_ref):
        return lax.dot_general(a_ref[...], w_ref[0].astype(BF16), _NT, preferred_element_type=F32)

    o_ref[...] = nt(wt_ref).astype(o_ref.dtype)
    for c_in, c_out in zip(cast_in, cast_out):
        c_out[...] = c_in[...].astype(c_out.dtype)

    @pl.when(pl.program_id(1) == 0)
    def _():
        for w_ref, s_ref in zip(side_w, side_o):
            s_ref[...] = nt(w_ref).astype(s_ref.dtype)


def _matmul_t(a, wt, *, layer, row0, ncols, tm, tn, out_dtype, side_rows=(), cast=None):
    m, k = a.shape
    nj = ncols // tn
    wspec = lambda rows, imap: pl.BlockSpec((pl.Element(1), pl.Element(rows), pl.Element(k)), imap)
    in_specs = ([pl.BlockSpec((tm, k), lambda i, j: (i, 0)),
                 wspec(tn, lambda i, j: (layer, pl.multiple_of(row0 + j * tn, 8), 0))]
                + [wspec(LANES, lambda i, j, r=r: (layer, r, 0)) for r in side_rows])
    out_specs = ([pl.BlockSpec((tm, tn), lambda i, j: (i, j))]
                 + [pl.BlockSpec((tm, LANES), lambda i, j: (i, 0)) for _ in side_rows])
    out_shape = ([jax.ShapeDtypeStruct((m, ncols), out_dtype)]
                 + [jax.ShapeDtypeStruct((m, LANES), out_dtype) for _ in side_rows])
    args = [a, wt] + [wt] * len(side_rows)
    if cast is not None:
        cw, cl = cast
        rows, cols = cw.shape[1:]
        slab = rows // ((m // tm) * nj)
        assert slab * (m // tm) * nj == rows and slab % 16 == 0
        in_specs.append(pl.BlockSpec((None, slab, cols), lambda i, j: (cl, i * nj + j, 0)))
        out_specs.append(pl.BlockSpec((slab, cols), lambda i, j: (i * nj + j, 0)))
        out_shape.append(jax.ShapeDtypeStruct((rows, cols), BF16))
        args.append(cw)
    outs = pl.pallas_call(
        functools.partial(_mm_nt_kernel, nside=len(side_rows), ncast=int(cast is not None)),
        grid=(m // tm, nj),
        in_specs=in_specs, out_specs=out_specs, out_shape=out_shape,
        compiler_params=_cparams("parallel", "arbitrary"),
        name="mm_nt",
    )(*args)
    return outs[0] if len(outs) == 1 else tuple(outs)


def _ffn_up_kernel(x_ref, wg_ref, wu_ref, wd_ref, o_ref, wdb_ref):
    x = x_ref[...]
    g = jnp.dot(x, wg_ref[...].astype(BF16), preferred_element_type=F32)
    u = jnp.dot(x, wu_ref[...].astype(BF16), preferred_element_type=F32)
    o_ref[...] = (_silu(g) * u).astype(o_ref.dtype)
    wdb_ref[...] = wd_ref[...].astype(wdb_ref.dtype)


def _ffn_up(x_bf, w_gu, w_d, *, tm, tn, layer):
    m, k = x_bf.shape
    hidden = w_gu.shape[-1] // 2
    nb = hidden // tn
    steps = (m // tm) * nb
    slab = hidden // steps
    assert slab * steps == hidden and slab % 16 == 0
    return pl.pallas_call(
        _ffn_up_kernel,
        grid=(m // tm, nb),
        in_specs=[pl.BlockSpec((tm, k), lambda i, j: (i, 0)),
                  _wspec(w_gu, layer, k, tn, lambda i, j: (0, j)),
                  _wspec(w_gu, layer, k, tn, lambda i, j: (0, j + nb)),
                  _wspec(w_d, layer, slab, w_d.shape[-1], lambda i, j: (i * nb + j, 0))],
        out_specs=[pl.BlockSpec((tm, tn), lambda i, j: (i, j)),
                   pl.BlockSpec((slab, w_d.shape[-1]), lambda i, j: (i * nb + j, 0))],
        out_shape=[jax.ShapeDtypeStruct((m, hidden), BF16),
                   jax.ShapeDtypeStruct((hidden, w_d.shape[-1]), BF16)],
        compiler_params=_cparams("parallel", "arbitrary"),
        name="ffn_up",
    )(x_bf, w_gu, w_gu, w_d)


def _mm_ln_kernel(*refs):
    a_refs, (w_ref, res_ref, g_ref, b_ref, of_ref, ob_ref) = refs[:-6], refs[-6:]
    a = a_refs[0][...] if len(a_refs) == 1 else jnp.concatenate([r[...] for r in a_refs], axis=1)
    y = DEEPNORM_ALPHA * res_ref[...] + jnp.dot(a, w_ref[...], preferred_element_type=F32)
    mu = jnp.mean(y, axis=-1, keepdims=True)
    yc = y - mu
    var = jnp.mean(yc * yc, axis=-1, keepdims=True)
    out = yc * lax.rsqrt(var + 1e-5) * g_ref[...] + b_ref[...]
    of_ref[...] = out
    ob_ref[...] = out.astype(BF16)


def _mm_ln(parts, w, res, g, b, *, tm):
    m = res.shape[0]
    kdim, n = w.shape
    widths = tuple(p.shape[1] for p in parts)
    assert sum(widths) == kdim
    return pl.pallas_call(
        _mm_ln_kernel,
        grid=(m // tm,),
        in_specs=[pl.BlockSpec((tm, wd), lambda i: (i, 0)) for wd in widths] + [
            pl.BlockSpec((kdim, n), lambda i: (0, 0), pipeline_mode=pl.Buffered(1)),
            pl.BlockSpec((tm, n), lambda i: (i, 0)),
            pl.BlockSpec((1, n), lambda i: (0, 0)),
            pl.BlockSpec((1, n), lambda i: (0, 0))],
        out_specs=[pl.BlockSpec((tm, n), lambda i: (i, 0)),
                   pl.BlockSpec((tm, n), lambda i: (i, 0))],
        out_shape=[jax.ShapeDtypeStruct((m, n), F32), jax.ShapeDtypeStruct((m, n), BF16)],
        compiler_params=_cparams("parallel"),
        name="mm_ln",
    )(*parts, w, res, g, b)


def _gla_kernel(q_ref, k_ref, v_ref, r_ref, g_ref, wg2_ref, bg_ref, ng_ref, o_ref, st_ref, *, tile):
    L, H, DK, DV = GLA_CHUNK, GLA_HEADS, GLA_DK, GLA_DV

    @pl.when(pl.program_id(1) == 0)
    def _():
        st_ref[...] = jnp.zeros_like(st_ref)

    tril = _tril(L)
    tril_b = tril.astype(BF16)
    g16 = g_ref[:, :GLA_GATE_RANK]
    nchunk = tile // L
    hc = [(h, c) for c in range(nchunk) for h in range(H)]
    ksl = lambda h: slice(h * DK, (h + 1) * DK)
    vsl = lambda h: slice(h * DV, (h + 1) * DV)
    rsl = lambda c: slice(c * L, (c + 1) * L)
    log_a = [_log_sigmoid(_dot3(g16, wg2_ref[:, ksl(h)]) + bg_ref[:, ksl(h)]) / GLA_TAU for h in range(H)]
    b = {(h, c): _chunk_cumsum(tril_b, log_a[h][rsl(c)]) for h, c in hc}
    q_dec, k_inv, k_end, decay, vb = {}, {}, {}, {}, {}
    for h, c in hc:
        bb = b[h, c]
        b_last = bb[L - 1:L]
        kk = k_ref[rsl(c), ksl(h)]
        q_dec[h, c] = (q_ref[rsl(c), ksl(h)] * DK ** -0.5 * jnp.exp(bb)).astype(BF16)
        k_inv[h, c] = (kk * jnp.exp(-bb)).astype(BF16)
        k_end[h, c] = (kk * jnp.exp(b_last - bb)).astype(BF16)
        decay[h, c] = jnp.exp(b_last)
        vb[h, c] = v_ref[rsl(c), vsl(h)]
    scores = {p: _bdot_nt(q_dec[p], k_inv[p]) for p in hc}
    scores = {p: jnp.where(tril, scores[p], 0.0).astype(BF16) for p in hc}
    intra = {p: _bdot(scores[p], vb[p]) for p in hc}
    s_loc = {p: _bdot_tn(vb[p], k_end[p]) for p in hc}
    st = [st_ref[h] for h in range(H)]
    outs = {}
    for h, c in hc:
        outs[h, c] = intra[h, c] + _bdot_nt(q_dec[h, c], st[h])
        st[h] = st[h] * decay[h, c] + s_loc[h, c]
    for h in range(H):
        st_ref[h] = st[h]
        o = jnp.concatenate([outs[h, c] for c in range(nchunk)], axis=0)
        o = o * lax.rsqrt(jnp.mean(o * o, axis=-1, keepdims=True) + 1e-6) * ng_ref[...]
        o_ref[:, vsl(h)] = (o * _silu(r_ref[:, vsl(h)])).astype(o_ref.dtype)


def _gla(y1, y2, gates, wg2, bg, ng, *, batch, seq, tile):
    nt = seq // tile
    qk_w, v_w = GLA_HEADS * GLA_DK, GLA_HEADS * GLA_DV
    row = lambda b, t: b * nt + t
    return pl.pallas_call(
        functools.partial(_gla_kernel, tile=tile),
        grid=(batch, nt),
        in_specs=[pl.BlockSpec((tile, qk_w), lambda b, t: (row(b, t), 0)),
                  pl.BlockSpec((tile, qk_w), lambda b, t: (row(b, t), 1)),
                  pl.BlockSpec((tile, v_w), lambda b, t: (row(b, t), 1)),
                  pl.BlockSpec((tile, v_w), lambda b, t: (row(b, t), 0)),
                  pl.BlockSpec((tile, LANES), lambda b, t: (row(b, t), 0)),
                  pl.BlockSpec((GLA_GATE_RANK, qk_w), lambda b, t: (0, 0)),
                  pl.BlockSpec((1, qk_w), lambda b, t: (0, 0)),
                  pl.BlockSpec((1, GLA_DV), lambda b, t: (0, 0))],
        out_specs=pl.BlockSpec((tile, v_w), lambda b, t: (row(b, t), 0)),
        out_shape=jax.ShapeDtypeStruct((batch * seq, v_w), BF16),
        scratch_shapes=[pltpu.VMEM((GLA_HEADS, GLA_DV, GLA_DK), F32)],
        compiler_params=_cparams("parallel", "arbitrary"),
        name="gla",
    )(y1, y1, y1, y2, gates, wg2, bg, ng)


LOG2E = 1.4426950408889634
LN2 = 0.6931471805599453
DIL_STRIDE1 = 4


def _dil_kernel(*refs, unit):
    ngrp = len(DIL_PAIRS)
    in_refs, (o_ref, og_ref, lg_ref, rg_ref) = refs[:5 * ngrp], refs[5 * ngrp:]
    u = pl.program_id(1)
    n = DIL_BLOCK
    qscale = DIL_HEAD_DIM ** -0.5 * LOG2E
    row = lax.broadcasted_iota(jnp.int32, (n, n), 0)
    col = lax.broadcasted_iota(jnp.int32, (n, n), 1)
    prev_band, cur_band = col >= row, col <= row
    ones = jnp.ones((n, LANES), BF16)
    for g, (_, d) in enumerate(DIL_PAIRS):
        q_ref, k_ref, v_ref, kp_ref, vp_ref = in_refs[5 * g:5 * g + 5]
        sub = n * d
        blocks = []
        if d > DIL_STRIDE1:
            d2 = d // DIL_STRIDE1
            assert unit == sub and d2 <= DIL_STRIDE1
            per = unit // DIL_STRIDE1
            for a, src in enumerate((q_ref, k_ref, v_ref, kp_ref, vp_ref)):
                for r1 in range(DIL_STRIDE1):
                    rg_ref[a, r1] = src[pl.ds(r1, per, stride=DIL_STRIDE1), :]
            for r in range(d):
                at = (r % DIL_STRIDE1, pl.ds(r // DIL_STRIDE1, n, stride=d2))
                blocks.append(tuple((rg_ref, (a,) + at) for a in range(5))
                              + (prev_band & (u > 0), ((rg_ref, (5,) + at), (rg_ref, (6,) + at))))
        else:
            for s in range(unit // sub):
                for r in range(d):
                    cur = pl.ds(s * sub + r, n, stride=d) if d > 1 else pl.ds(s * sub, n)
                    outs = ((og_ref, (g, cur)), (lg_ref, (g, cur)))
                    if s == 0:
                        prv = pl.ds(r, n, stride=d) if d > 1 else pl.ds(0, n)
                        blocks.append(((q_ref, (cur,)), (k_ref, (cur,)), (v_ref, (cur,)), (kp_ref, (prv,)),
                                       (vp_ref, (prv,)), prev_band & (u > 0), outs))
                    else:
                        prv = pl.ds((s - 1) * sub + r, n, stride=d) if d > 1 else pl.ds((s - 1) * sub, n)
                        blocks.append(((q_ref, (cur,)), (k_ref, (cur,)), (v_ref, (cur,)), (k_ref, (prv,)),
                                       (v_ref, (prv,)), prev_band, outs))
        get = lambda ra: ra[0][ra[1] + (slice(None),)]
        scores = []
        for qa, kc, _, kp, _, pmask, _ in blocks:
            q = (get(qa) * qscale).astype(BF16)
            scores.append((jnp.where(pmask, _bdot_nt(q, get(kp)), NEG_INF),
                           jnp.where(cur_band, _bdot_nt(q, get(kc)), NEG_INF)))
        probs = []
        for s_p, s_c in scores:
            m = jnp.maximum(jnp.max(s_p, axis=-1, keepdims=True), jnp.max(s_c, axis=-1, keepdims=True))
            probs.append((jnp.exp2(s_p - m).astype(BF16), jnp.exp2(s_c - m).astype(BF16), m))
        for (_, _, vc, _, vp, _, ((o_dst, o_at), (l_dst, l_at))), (p_p, p_c, m) in zip(blocks, probs):
            den = _bdot(p_p, ones) + _bdot(p_c, ones)
            o_dst[o_at + (slice(None),)] = (_bdot(p_p, get(vp)) + _bdot(p_c, get(vc))) / den
            l_dst[l_at + (slice(None),)] = m * LN2 + jnp.log(den)
        if d > DIL_STRIDE1:
            for r1 in range(DIL_STRIDE1):
                og_ref[g, pl.ds(r1, per, stride=DIL_STRIDE1), :] = rg_ref[5, r1]
                lg_ref[g, pl.ds(r1, per, stride=DIL_STRIDE1), :] = rg_ref[6, r1]
    lses = [lg_ref[g] for g in range(ngrp)]
    m = functools.reduce(jnp.maximum, lses)
    ws = [jnp.exp(l - m) for l in lses]
    tot = functools.reduce(lambda a, b: a + b, ws)
    acc = sum(w * og_ref[g] for g, w in enumerate(ws))
    o_ref[...] = (acc / tot).astype(o_ref.dtype)


def _dilated(y, col0, *, batch, seq):
    ngrp, H, n = len(DIL_PAIRS), DIL_HEADS, DIL_BLOCK
    unit = max(d for _, d in DIL_PAIRS) * n
    assert seq % unit == 0
    nu = seq // unit
    cb0 = col0 // LANES
    in_specs, args = [], []
    for g, (_, d) in enumerate(DIL_PAIRS):
        sub = n * d
        per = unit // sub
        for part in range(3):
            cb = cb0 + part * ngrp * H + g * H
            in_specs.append(pl.BlockSpec((unit, LANES), lambda b, u, h, cb=cb: (b * nu + u, cb + h)))
            args.append(y)
        for part in (1, 2):
            cb = cb0 + part * ngrp * H + g * H
            in_specs.append(pl.BlockSpec(
                (sub, LANES),
                lambda b, u, h, cb=cb, per=per: (jnp.maximum(b * nu * per + u * per - 1, 0), cb + h)))
            args.append(y)
    return pl.pallas_call(
        functools.partial(_dil_kernel, unit=unit),
        grid=(batch, nu, H),
        in_specs=in_specs,
        out_specs=pl.BlockSpec((unit, LANES), lambda b, u, h: (b * nu + u, h)),
        out_shape=jax.ShapeDtypeStruct((batch * seq, H * LANES), BF16),
        scratch_shapes=[pltpu.VMEM((ngrp, unit, LANES), F32), pltpu.VMEM((ngrp, unit, LANES), F32),
                        pltpu.VMEM((7, DIL_STRIDE1, unit // DIL_STRIDE1, LANES), F32)],
        compiler_params=_cparams("parallel", "parallel", "arbitrary"),
        name="dilated",
    )(*args)


I_LANE, F_LANE = 0, MLSTM_HEADS


def _mlstm_kernel(qk_ref, v_ref, co_ref, gt_ref, cw_ref, cb_ref, gb_ref, ng_ref, o_ref,
                  c_ref, m_ref, tail_ref, xbuf_ref, *, tile):
    L, H, DK, DV = MLSTM_CHUNK, MLSTM_HEADS, MLSTM_DQK, MLSTM_DV
    KC = MLSTM_CONV
    PAD = 8

    @pl.when(pl.program_id(1) == 0)
    def _():
        c_ref[...] = jnp.zeros_like(c_ref)
        m_ref[...] = jnp.zeros_like(m_ref)
        tail_ref[...] = jnp.zeros_like(tail_ref)

    xbuf_ref[0:PAD, :] = tail_ref[...]
    xbuf_ref[PAD:PAD + tile, :] = qk_ref[...]
    tail_ref[...] = qk_ref[tile - PAD:tile, :]
    acc = cb_ref[...] + cw_ref[KC - 1:KC, :] * xbuf_ref[PAD:PAD + tile, :]
    for j in range(KC - 1):
        off = PAD - (KC - 1) + j
        acc = acc + cw_ref[j:j + 1, :] * xbuf_ref[off:off + tile, :]
    qk = _silu(acc)

    gates = gt_ref[...] + gb_ref[...]
    lane = lax.broadcasted_iota(jnp.int32, gates.shape, 1)
    z = jnp.where(lane >= F_LANE, _log_sigmoid(gates), gates)
    tril = _tril(L)
    tril_b = tril.astype(BF16)
    nchunk = tile // L
    lane_c = lax.broadcasted_iota(jnp.int32, (L, LANES), 1)
    zs, bs, wts = [], [], []
    for c in range(nchunk):
        zc = z[c * L:(c + 1) * L]
        bc = _chunk_cumsum(tril_b, zc)
        zs.append(zc)
        bs.append(bc)
        wts.append(jnp.where(lane_c >= F_LANE, bc, zc).T)
    ones_col = (lane_c == 0).astype(F32)

    hc = [(h, c) for c in range(nchunk) for h in range(H)]
    rsl = lambda c: slice(c * L, (c + 1) * L)
    b_col, inter_log, kwt, s_old, s_new, d_log = {}, {}, {}, {}, {}, {}
    m = [m_ref[h:h + 1, 0:1] for h in range(H)]
    for h, c in hc:
        bc = bs[c][:, F_LANE + h:F_LANE + h + 1]
        li_col = zs[c][:, I_LANE + h:I_LANE + h + 1]
        b_row = wts[c][F_LANE + h:F_LANE + h + 1, :]
        li_row = wts[c][I_LANE + h:I_LANE + h + 1, :]
        b_last = bs[c][L - 1:L, F_LANE + h:F_LANE + h + 1]
        d_log[h, c] = jnp.where(tril, bc - b_row + li_row, NEG_INF)
        inter_log[h, c] = bc + m[h]
        a_col = b_last - bc + li_col
        m_loc = jnp.max(a_col, axis=0, keepdims=True)
        kwt[h, c] = jnp.exp(a_col - m_loc)
        m_new = jnp.maximum(b_last + m[h], m_loc)
        s_old[h, c] = jnp.exp(b_last + m[h] - m_new)
        s_new[h, c] = jnp.exp(m_loc - m_new)
        m[h] = m_new
    qb, kb, vb, qkm, m_t = {}, {}, {}, {}, {}
    for h, c in hc:
        qb[h, c] = qk[rsl(c), h * DK:(h + 1) * DK].astype(BF16)
        kb[h, c] = qk[rsl(c), (H + h) * DK:(H + h + 1) * DK] * DK ** -0.5
        vb[h, c] = v_ref[rsl(c), h * DV:(h + 1) * DV]
    raw = {p: _bdot_nt(qb[p], kb[p]) for p in hc}
    for p in hc:
        m_t[p] = jnp.maximum(inter_log[p], jnp.max(d_log[p], axis=-1, keepdims=True))
        qkm[p] = raw[p] * jnp.exp(d_log[p] - m_t[p])
    v_aug = {p: jnp.concatenate([vb[p], ones_col], axis=1).astype(BF16) for p in hc}
    intra = {p: _bdot(qkm[p], v_aug[p]) for p in hc}
    c_loc = {p: _bdot_tn(kb[p] * kwt[p], v_aug[p]) for p in hc}
    ct = [c_ref[h] for h in range(H)]
    outs = {}
    for h, c in hc:
        p = (h, c)
        inter_scale = jnp.exp(inter_log[p] - m_t[p])
        tot = intra[p] + inter_scale * _bdot(qb[p], ct[h])
        outs[p] = tot[:, :DV] / jnp.maximum(jnp.abs(tot[:, DV:DV + 1]), jnp.exp(-m_t[p]))
        ct[h] = ct[h] * s_old[p] + c_loc[p] * s_new[p]
    for h in range(H):
        c_ref[h] = ct[h]
        m_ref[h:h + 1, :] = jnp.broadcast_to(m[h], (1, LANES))
        hcat = jnp.concatenate([outs[h, c] for c in range(nchunk)], axis=0)
        mu = jnp.mean(hcat, axis=-1, keepdims=True)
        hcen = hcat - mu
        var = jnp.mean(hcen * hcen, axis=-1, keepdims=True)
        hn = hcen * lax.rsqrt(var + 1e-5) * ng_ref[...]
        o_ref[:, h * DV:(h + 1) * DV] = (_sigmoid(co_ref[:, h * DV:(h + 1) * DV]) * hn).astype(o_ref.dtype)


def _mlstm(y1, y2, gates, conv_w, conv_b, gate_bias, ng, *, batch, seq, tile):
    nt = seq // tile
    H, DK, DV = MLSTM_HEADS, MLSTM_DQK, MLSTM_DV
    qk_w, v_w = 2 * H * DK, H * DV
    row = lambda b, t: b * nt + t
    return pl.pallas_call(
        functools.partial(_mlstm_kernel, tile=tile),
        grid=(batch, nt),
        in_specs=[pl.BlockSpec((tile, qk_w), lambda b, t: (row(b, t), 0)),
                  pl.BlockSpec((tile, v_w), lambda b, t: (row(b, t), 1)),
                  pl.BlockSpec((tile, v_w), lambda b, t: (row(b, t), 0)),
                  pl.BlockSpec((tile, LANES), lambda b, t: (row(b, t), 0)),
                  pl.BlockSpec((MLSTM_CONV, qk_w), lambda b, t: (0, 0)),
                  pl.BlockSpec((1, qk_w), lambda b, t: (0, 0)),
                  pl.BlockSpec((1, LANES), lambda b, t: (0, 0)),
                  pl.BlockSpec((1, DV), lambda b, t: (0, 0))],
        out_specs=pl.BlockSpec((tile, v_w), lambda b, t: (row(b, t), 0)),
        out_shape=jax.ShapeDtypeStruct((batch * seq, v_w), BF16),
        scratch_shapes=[pltpu.VMEM((H, DK, DV + LANES), F32),
                        pltpu.VMEM((8, LANES), F32),
                        pltpu.VMEM((8, qk_w), F32),
                        pltpu.VMEM((tile + 8, qk_w), F32)],
        compiler_params=_cparams("parallel", "arbitrary"),
        name="mlstm",
    )(y1, y1, y2, gates, conv_w, conv_b, gate_bias, ng)


def _rope128(x, c_ref, s1_ref, s2_ref):
    return x * c_ref[...] + pltpu.roll(x, 32, 1) * s1_ref[...] + pltpu.roll(x, LANES - 32, 1) * s2_ref[...]


def _rms(x, g):
    return x * lax.rsqrt(jnp.mean(x * x, axis=-1, keepdims=True) + 1e-6) * g


MLA_SCORE_SCALE = (MLA_NOPE + MLA_ROPE) ** -0.5 * 1.4426950408889634


def _mla_q_kernel(x_ref, g_ref, w_ref, c_ref, s1_ref, s2_ref, o_ref):
    q = _bdot(_rms(x_ref[...], g_ref[...]), w_ref[...]) * MLA_SCORE_SCALE
    for h in range(MLA_HEADS):
        o_ref[:, 256 * h:256 * h + 128] = q[:, 256 * h:256 * h + 128].astype(o_ref.dtype)
        o_ref[:, 256 * h + 128:256 * h + 256] = _rope128(
            q[:, 256 * h + 128:256 * h + 256], c_ref, s1_ref, s2_ref).astype(o_ref.dtype)


def _mla_kv_kernel(x_ref, g_ref, wk_ref, wvt_ref, kr_ref, c_ref, s1_ref, s2_ref, k_ref, vt_ref):
    xn = _rms(x_ref[...], g_ref[...]).astype(BF16)
    kn = jnp.dot(xn, wk_ref[...], preferred_element_type=F32)
    vt_ref[...] = lax.dot_general(wvt_ref[...], xn, _NT, preferred_element_type=F32).astype(vt_ref.dtype)
    kr = _rope128(kr_ref[...], c_ref, s1_ref, s2_ref).astype(k_ref.dtype)
    for h in range(MLA_HEADS):
        k_ref[:, 256 * h:256 * h + 128] = kn[:, 128 * h:128 * h + 128].astype(k_ref.dtype)
        k_ref[:, 256 * h + 128:256 * h + 256] = kr


def _mla_proj(y, qcb, gates, qg, kvg, wq, wk, wvt, tabs, *, tm):
    m = y.shape[0]
    H = MLA_HEADS
    tab_specs = [pl.BlockSpec((tm, LANES), lambda i: (i, 0))] * 3
    full = lambda a: pl.BlockSpec(a.shape, lambda i: (0, 0))
    qf = pl.pallas_call(
        _mla_q_kernel,
        grid=(m // tm,),
        in_specs=[pl.BlockSpec((tm, MLA_Q_RANK), lambda i: (i, qcb)), full(qg), full(wq)] + tab_specs,
        out_specs=pl.BlockSpec((tm, H * 256), lambda i: (i, 0)),
        out_shape=jax.ShapeDtypeStruct((m, H * 256), BF16),
        compiler_params=_cparams("parallel"),
        name="mla_q",
    )(y, qg, wq, *tabs)
    kf, vt = pl.pallas_call(
        _mla_kv_kernel,
        grid=(m // tm,),
        in_specs=[pl.BlockSpec((tm, MLA_KV_RANK), lambda i: (i, qcb + 1)), full(kvg), full(wk), full(wvt),
                  pl.BlockSpec((tm, LANES), lambda i: (i, 0))] + tab_specs,
        out_specs=[pl.BlockSpec((tm, H * 256), lambda i: (i, 0)),
                   pl.BlockSpec((H * MLA_DV, tm), lambda i: (0, i))],
        out_shape=[jax.ShapeDtypeStruct((m, H * 256), BF16), jax.ShapeDtypeStruct((H * MLA_DV, m), BF16)],
        compiler_params=_cparams("parallel"),
        name="mla_kv",
    )(y, kvg, wk, wvt, gates, *tabs)
    return qf, kf, vt


def _mla_attn_kernel(q_ref, k_ref, vt_ref, o_ref, s_ref, mx_ref, *, tq, hg):
    i = pl.program_id(2)
    DV = MLA_DV

    def scores(j, slot):
        start = pl.multiple_of(j * tq, tq)
        for g in range(hg):
            st = lax.dot_general(k_ref[pl.ds(start, tq), 256 * g:256 * (g + 1)],
                                 q_ref[:, 256 * g:256 * (g + 1)], _NT, preferred_element_type=F32)
            s_ref[slot, g] = st
            mx_ref[slot, g] = jnp.max(st, axis=0, keepdims=True)

    def step(j, slot, carry, masked):
        start = pl.multiple_of(j * tq, tq)
        out = []
        for g in range(hg):
            m, l, acc = carry[g]
            vtb = vt_ref[DV * g:DV * (g + 1), pl.ds(start, tq)]
            st = s_ref[slot, g]
            if masked:
                kk = lax.broadcasted_iota(jnp.int32, (tq, tq), 0)
                qq = lax.broadcasted_iota(jnp.int32, (tq, tq), 1)
                st = jnp.where(kk <= qq, st, NEG_INF)
                m_new = jnp.maximum(m, jnp.max(st, axis=0, keepdims=True))
            else:
                m_new = jnp.maximum(m, mx_ref[slot, g])
            pt = jnp.exp2(st - m_new)
            a = jnp.exp2(m - m_new)
            l = a * l + jnp.sum(pt, axis=0, keepdims=True)
            acc = a * acc + jnp.dot(vtb, pt.astype(BF16), preferred_element_type=F32)
            out.append((m_new, l, acc))
        return tuple(out)

    init = tuple((jnp.full((1, tq), NEG_INF, F32), jnp.zeros((1, tq), F32), jnp.zeros((DV, tq), F32))
                 for _ in range(hg))

    def pair(jj, carry):
        scores(2 * jj + 1, 1)
        carry = step(2 * jj, 0, carry, False)
        scores(2 * jj + 2, 0)
        return step(2 * jj + 1, 1, carry, False)

    def finish(carry):
        for g in range(hg):
            _, l, acc = carry[g]
            o_ref[:, DV * g:DV * (g + 1)] = (acc / l).T.astype(o_ref.dtype)

    scores(0, 0)
    carry = lax.fori_loop(0, i // 2, pair, init)

    @pl.when(i % 2 == 0)
    def _():
        finish(step(i, 0, carry, True))

    @pl.when(i % 2 == 1)
    def _():
        scores(i, 1)
        finish(step(i, 1, step(i - 1, 0, carry, False), True))


def _mla_attn(qf, kf, vt, *, batch, seq, tq, hg=4):
    H = MLA_HEADS
    nq = seq // tq
    return pl.pallas_call(
        functools.partial(_mla_attn_kernel, tq=tq, hg=hg),
        grid=(batch, H // hg, nq),
        in_specs=[pl.BlockSpec((tq, 256 * hg), lambda b, h, i: (b * nq + i, h)),
                  pl.BlockSpec((seq, 256 * hg), lambda b, h, i: (b, h)),
                  pl.BlockSpec((MLA_DV * hg, seq), lambda b, h, i: (h, b))],
        out_specs=pl.BlockSpec((tq, MLA_DV * hg), lambda b, h, i: (b * nq + i, h)),
        out_shape=jax.ShapeDtypeStruct((batch * seq, H * MLA_DV), BF16),
        scratch_shapes=[pltpu.VMEM((2, hg, tq, tq), F32), pltpu.VMEM((2, hg, 1, tq), F32)],
        compiler_params=_cparams("parallel", "parallel", "arbitrary"),
        name="mla_attn",
    )(qf, kf, vt)


def _tile_for(n, pref):
    t = pref
    while n % t:
        t //= 2
    return t


def _even_mixer(x_bf, wt_all, w_o_all, j, wg2, bg, ng, *, batch, seq):
    m = x_bf.shape[0]
    gq_gk_gv = 2 * GLA_HEADS * GLA_DK + GLA_HEADS * GLA_DV
    rest0 = gq_gk_gv + GLA_GATE_RANK
    tm = _tile_for(m, 2048)
    mm = functools.partial(_matmul_t, x_bf, wt_all, layer=j, tm=tm, out_dtype=F32)
    y1, gts, wo_bf = mm(row0=0, ncols=gq_gk_gv, tn=512, side_rows=(gq_gk_gv,), cast=(w_o_all, j))
    y2 = mm(row0=rest0, ncols=wt_all.shape[1] - rest0, tn=512)
    o_a = _gla(y1, y2, gts, wg2, bg.reshape(1, -1), ng.reshape(1, -1), batch=batch, seq=seq,
               tile=_tile_for(seq, 512))
    o_b = _dilated(y2, GLA_HEADS * GLA_DV, batch=batch, seq=seq)
    return (o_a, o_b), wo_bf


def _rope_tables(positions):
    inv_freq = ROPE_THETA ** (-jnp.arange(0, MLA_ROPE, 2, dtype=F32) / MLA_ROPE)
    ang = positions.astype(F32).reshape(-1, 1) * inv_freq
    cos, sin = jnp.cos(ang), jnp.sin(ang)
    z32 = jnp.zeros_like(cos)
    z64 = jnp.zeros((cos.shape[0], LANES - MLA_ROPE), F32)
    c = jnp.concatenate([z64, cos, cos], axis=1)
    s1 = jnp.concatenate([z64, z32, sin], axis=1)
    s2 = jnp.concatenate([z64, -sin, z32], axis=1)
    return c, s1, s2


def _odd_mixer(x_bf, tabs, wt_all, w_o_all, j, conv_w, conv_b, bi, bf, ng, qg, kvg, wuq, wukv, *, batch, seq):
    m = x_bf.shape[0]
    H = MLA_HEADS
    cq_ck_cv = 2 * MLSTM_HEADS * MLSTM_DQK + MLSTM_HEADS * MLSTM_DV
    co0 = cq_ck_cv + 2 * MLSTM_HEADS
    kr0 = co0 + MLSTM_HEADS * MLSTM_DV + MLA_Q_RANK + MLA_KV_RANK
    gate_bias = jnp.concatenate([bi, bf, jnp.zeros((LANES - 2 * MLSTM_HEADS,), F32)]).reshape(1, LANES)
    wq3 = wuq.reshape(-1, H, MLA_NOPE + MLA_ROPE)
    wq = jnp.concatenate([wq3[:, :, :MLA_NOPE], jnp.zeros(wq3.shape[:2] + (LANES - MLA_ROPE,), F32),
                          wq3[:, :, MLA_NOPE:]], axis=2).reshape(-1, H * 256).astype(BF16)
    wkv = wukv.reshape(-1, H, MLA_NOPE + MLA_DV)
    wk = wkv[:, :, :MLA_NOPE].reshape(-1, H * MLA_NOPE).astype(BF16)
    wvt = wkv[:, :, MLA_NOPE:].reshape(-1, H * MLA_DV).T.astype(BF16)
    tm = _tile_for(m, 2048)
    mm = functools.partial(_matmul_t, x_bf, wt_all, layer=j, tm=tm, out_dtype=F32)
    y1, gts, kr, wo_bf = mm(row0=0, ncols=cq_ck_cv, tn=512, side_rows=(cq_ck_cv, kr0 + MLA_ROPE - LANES),
                            cast=(w_o_all, j))
    y2 = mm(row0=co0, ncols=kr0 - co0, tn=512)
    o_c = _mlstm(y1, y2, gts, conv_w, conv_b.reshape(1, -1), gate_bias, ng.reshape(1, -1),
                 batch=batch, seq=seq, tile=_tile_for(seq, 512))
    qf, kf, vt = _mla_proj(y2, MLSTM_HEADS * MLSTM_DV // MLA_Q_RANK, kr, qg.reshape(1, -1), kvg.reshape(1, -1),
                           wq, wk, wvt, tabs, tm=_tile_for(m, 512))
    o_d = _mla_attn(qf, kf, vt, batch=batch, seq=seq, tq=_tile_for(seq, 512))
    return (o_c, o_d), wo_bf


def kernel(x, positions, even_w_in, even_gla_wg2, even_gla_bg, even_gla_norm_g, even_w_o, odd_w_in, odd_conv_w, odd_conv_b, odd_mlstm_bi, odd_mlstm_bf, odd_mlstm_norm_g, odd_mla_qnorm_g, odd_mla_kvnorm_g, odd_mla_wuq, odd_mla_wukv, odd_w_o, ln1_g, ln1_b, ffn_wgu, ffn_wd, ln2_g, ln2_b):
    batch, seq, d = x.shape
    m = batch * seq
    xf = x.reshape(m, d)
    xb = xf.astype(BF16)
    tabs = _rope_tables(positions)
    even_wt = jnp.swapaxes(even_w_in, 1, 2)
    odd_wt = jnp.swapaxes(odd_w_in, 1, 2)
    depth = ln1_g.shape[0]
    tm_ln = _tile_for(m, 512)
    for l in range(depth):
        j = l // 2
        if l % 2 == 0:
            parts, wo_bf = _even_mixer(xb, even_wt, even_w_o, j, even_gla_wg2[j], even_gla_bg[j],
                                       even_gla_norm_g[j], batch=batch, seq=seq)
        else:
            parts, wo_bf = _odd_mixer(xb, tabs, odd_wt, odd_w_o, j, odd_conv_w[j], odd_conv_b[j],
                                      odd_mlstm_bi[j], odd_mlstm_bf[j], odd_mlstm_norm_g[j], odd_mla_qnorm_g[j],
                                      odd_mla_kvnorm_g[j], odd_mla_wuq[j], odd_mla_wukv[j], batch=batch, seq=seq)
        xf, xb = _mm_ln(parts, wo_bf, xf, ln1_g[l].reshape(1, -1), ln1_b[l].reshape(1, -1), tm=tm_ln)
        hid, wd_bf = _ffn_up(xb, ffn_wgu, ffn_wd, tm=_tile_for(m, 2048), tn=512, layer=l)
        xf, xb = _mm_ln((hid,), wd_bf, xf, ln2_g[l].reshape(1, -1), ln2_b[l].reshape(1, -1),
                        tm=_tile_for(m, 256))
    return xf.reshape(batch, seq, d)
```

```python
import functools

import jax
import jax.numpy as jnp
from jax import lax
from jax.experimental import pallas as pl
from jax.experimental.pallas import tpu as pltpu

F32 = jnp.float32
BF16 = jnp.bfloat16

D_MODEL = 2048
DEPTH = 4
GLA_HEADS, GLA_DK, GLA_DV, GLA_GATE_RANK, GLA_TAU, GLA_CHUNK = 4, 128, 256, 16, 16.0, 64
DIL_PAIRS = ((128, 1), (512, 4), (2048, 16))
DIL_HEADS, DIL_HEAD_DIM, DIL_BLOCK = 4, 128, 128
MLSTM_HEADS, MLSTM_DQK, MLSTM_DV, MLSTM_CONV, MLSTM_CHUNK = 4, 128, 256, 4, 64
MLA_HEADS, MLA_Q_RANK, MLA_KV_RANK, MLA_NOPE, MLA_ROPE, MLA_DV = 8, 512, 512, 128, 64, 128
ROPE_THETA = 10000.0
FFN_HIDDEN = 5632
DEEPNORM_ALPHA = (2.0 * DEPTH) ** 0.25

LANES = 128
VMEM_LIMIT = 56 * 1024 * 1024
NEG_INF = float("-inf")

_NT = (((1,), (1,)), ((), ()))


def _cparams(*sem):
    return pltpu.CompilerParams(dimension_semantics=sem, vmem_limit_bytes=VMEM_LIMIT)


def _bdot(a, b):
    return jnp.dot(a.astype(BF16), b.astype(BF16), preferred_element_type=F32)


def _bdot_nt(a, b):
    return lax.dot_general(a.astype(BF16), b.astype(BF16), _NT, preferred_element_type=F32)


def _bdot_tn(a, b):
    return jnp.dot(a.astype(F32).T.astype(BF16), b.astype(BF16), preferred_element_type=F32)


def _split2(a):
    hi = a.astype(BF16)
    lo = (a - hi.astype(F32)).astype(BF16)
    return hi, lo


def _dot3(a, b):
    ah, al = _split2(a)
    bh, bl = _split2(b)
    f = functools.partial(jnp.dot, preferred_element_type=F32)
    return f(ah, bh) + (f(ah, bl) + f(al, bh))


def _chunk_cumsum(tril_b, x):
    hi, lo = _split2(x)
    lo2 = (x - hi.astype(F32) - lo.astype(F32)).astype(BF16)
    f = functools.partial(jnp.dot, preferred_element_type=F32)
    return f(tril_b, hi) + (f(tril_b, lo) + f(tril_b, lo2))


def _tril(n):
    r = lax.broadcasted_iota(jnp.int32, (n, n), 0)
    c = lax.broadcasted_iota(jnp.int32, (n, n), 1)
    return c <= r


def _log_sigmoid(x):
    return jnp.minimum(x, 0.0) - jnp.log1p(jnp.exp(-jnp.abs(x)))


def _sigmoid(x):
    return 1.0 / (1.0 + jnp.exp(-x))


def _silu(x):
    return x * _sigmoid(x)


def _wspec(w, layer, rows, cols, index_map):
    if w.ndim == 2:
        return pl.BlockSpec((rows, cols), index_map)
    return pl.BlockSpec((None, rows, cols), lambda *ids: (layer,) + tuple(index_map(*ids)))


def _mm_nt_kernel(*refs, nside, ncast):
    a_ref, wt_ref, side_w, cast_in = refs[0], refs[1], refs[2:2 + nside], refs[2 + nside:2 + nside + ncast]
    outs = refs[2 + nside + ncast:]
    o_ref, side_o, cast_out = outs[0], outs[1:1 + nside], outs[1 + nside:]

    o_ref[...] = lax.dot_general(a_ref[...], wt_ref[0].astype(BF16), _NT,
                                 preferred_element_type=F32).astype(o_ref.dtype)
    for c_in, c_out in zip(cast_in, cast_out):
        c_out[...] = c_in[...].astype(c_out.dtype)

    if not side_w:
        return

    @pl.when(pl.program_id(1) == 0)
    def _():
        w = jnp.concatenate([w_ref[0] for w_ref in side_w], axis=0).astype(BF16)
        prod = lax.dot_general(a_ref[...], w, _NT, preferred_element_type=F32)
        for n, s_ref in enumerate(side_o):
            s_ref[...] = prod[:, n * LANES:(n + 1) * LANES].astype(s_ref.dtype)


def _matmul_t(a, wt, *, layer, row0, ncols, tm, tn, out_dtype, side_rows=(), cast=None):
    m, k = a.shape
    nj = ncols // tn
    wspec = lambda rows, imap: pl.BlockSpec((pl.Element(1), pl.Element(rows), pl.Element(k)), imap)
    in_specs = ([pl.BlockSpec((tm, k), lambda i, j: (i, 0)),
                 wspec(tn, lambda i, j: (layer, pl.multiple_of(row0 + j * tn, 8), 0))]
                + [wspec(LANES, lambda i, j, r=r: (layer, r, 0)) for r in side_rows])
    out_specs = ([pl.BlockSpec((tm, tn), lambda i, j: (i, j))]
                 + [pl.BlockSpec((tm, LANES), lambda i, j: (i, 0)) for _ in side_rows])
    out_shape = ([jax.ShapeDtypeStruct((m, ncols), out_dtype)]
                 + [jax.ShapeDtypeStruct((m, LANES), out_dtype) for _ in side_rows])
    args = [a, wt] + [wt] * len(side_rows)
    if cast is not None:
        cw, cl = cast
        rows, cols = cw.shape[1:]
        slab = rows // ((m // tm) * nj)
        assert slab * (m // tm) * nj == rows and slab % 16 == 0
        in_specs.append(pl.BlockSpec((None, slab, cols), lambda i, j: (cl, i * nj + j, 0)))
        out_specs.append(pl.BlockSpec((slab, cols), lambda i, j: (i * nj + j, 0)))
        out_shape.append(jax.ShapeDtypeStruct((rows, cols), BF16))
        args.append(cw)
    outs = pl.pallas_call(
        functools.partial(_mm_nt_kernel, nside=len(side_rows), ncast=int(cast is not None)),
        grid=(m // tm, nj),
        in_specs=in_specs, out_specs=out_specs, out_shape=out_shape,
        compiler_params=_cparams("parallel", "arbitrary"),
        name="mm_nt",
    )(*args)
    return outs[0] if len(outs) == 1 else tuple(outs)


def _ffn_up_kernel(x_ref, wg_ref, wu_ref, wd_ref, o_ref, wdb_ref):
    x = x_ref[...]
    g = jnp.dot(x, wg_ref[...].astype(BF16), preferred_element_type=F32)
    u = jnp.dot(x, wu_ref[...].astype(BF16), preferred_element_type=F32)
    o_ref[...] = (_silu(g) * u).astype(o_ref.dtype)
    wdb_ref[...] = wd_ref[...].astype(wdb_ref.dtype)


def _ffn_up(x_bf, w_gu, w_d, *, tm, tn, layer):
    m, k = x_bf.shape
    hidden = w_gu.shape[-1] // 2
    nb = hidden // tn
    steps = (m // tm) * nb
    slab = hidden // steps
    assert slab * steps == hidden and slab % 16 == 0
    return pl.pallas_call(
        _ffn_up_kernel,
        grid=(m // tm, nb),
        in_specs=[pl.BlockSpec((tm, k), lambda i, j: (i, 0)),
                  _wspec(w_gu, layer, k, tn, lambda i, j: (0, j)),
                  _wspec(w_gu, layer, k, tn, lambda i, j: (0, j + nb)),
                  _wspec(w_d, layer, slab, w_d.shape[-1], lambda i, j: (i * nb + j, 0))],
        out_specs=[pl.BlockSpec((tm, tn), lambda i, j: (i, j)),
                   pl.BlockSpec((slab, w_d.shape[-1]), lambda i, j: (i * nb + j, 0))],
        out_shape=[jax.ShapeDtypeStruct((m, hidden), BF16),
                   jax.ShapeDtypeStruct((hidden, w_d.shape[-1]), BF16)],
        compiler_params=_cparams("parallel", "arbitrary"),
        name="ffn_up",
    )(x_bf, w_gu, w_gu, w_d)


def _mm_ln_kernel(*refs):
    a_refs, (w_ref, res_ref, g_ref, b_ref, of_ref, ob_ref) = refs[:-6], refs[-6:]
    a = a_refs[0][...] if len(a_refs) == 1 else jnp.concatenate([r[...] for r in a_refs], axis=1)
    y = DEEPNORM_ALPHA * res_ref[...] + jnp.dot(a, w_ref[...], preferred_element_type=F32)
    mu = jnp.mean(y, axis=-1, keepdims=True)
    yc = y - mu
    var = jnp.mean(yc * yc, axis=-1, keepdims=True)
    out = yc * lax.rsqrt(var + 1e-5) * g_ref[...] + b_ref[...]
    of_ref[...] = out
    ob_ref[...] = out.astype(BF16)


def _mm_ln(parts, w, res, g, b, *, tm):
    m = res.shape[0]
    kdim, n = w.shape
    widths = tuple(p.shape[1] for p in parts)
    assert sum(widths) == kdim
    return pl.pallas_call(
        _mm_ln_kernel,
        grid=(m // tm,),
        in_specs=[pl.BlockSpec((tm, wd), lambda i: (i, 0)) for wd in widths] + [
            pl.BlockSpec((kdim, n), lambda i: (0, 0), pipeline_mode=pl.Buffered(1)),
            pl.BlockSpec((tm, n), lambda i: (i, 0)),
            pl.BlockSpec((1, n), lambda i: (0, 0)),
            pl.BlockSpec((1, n), lambda i: (0, 0))],
        out_specs=[pl.BlockSpec((tm, n), lambda i: (i, 0)),
                   pl.BlockSpec((tm, n), lambda i: (i, 0))],
        out_shape=[jax.ShapeDtypeStruct((m, n), F32), jax.ShapeDtypeStruct((m, n), BF16)],
        compiler_params=_cparams("parallel"),
        name="mm_ln",
    )(*parts, w, res, g, b)


def _gla_kernel(q_ref, k_ref, v_ref, r_ref, g_ref, wg2_ref, bg_ref, ng_ref, o_ref, st_ref, *, tile):
    L, H, DK, DV = GLA_CHUNK, GLA_HEADS, GLA_DK, GLA_DV

    @pl.when(pl.program_id(1) == 0)
    def _():
        st_ref[...] = jnp.zeros_like(st_ref)

    tril = _tril(L)
    tril_b = tril.astype(BF16)
    g16 = g_ref[:, :GLA_GATE_RANK]
    nchunk = tile // L
    hc = [(h, c) for c in range(nchunk) for h in range(H)]
    ksl = lambda h: slice(h * DK, (h + 1) * DK)
    vsl = lambda h: slice(h * DV, (h + 1) * DV)
    rsl = lambda c: slice(c * L, (c + 1) * L)
    log_a = [_log_sigmoid(_dot3(g16, wg2_ref[:, ksl(h)]) + bg_ref[:, ksl(h)]) / GLA_TAU for h in range(H)]
    b = {(h, c): _chunk_cumsum(tril_b, log_a[h][rsl(c)]) for h, c in hc}
    q_dec, k_inv, k_end, decay, vb = {}, {}, {}, {}, {}
    for h, c in hc:
        bb = b[h, c]
        b_last = bb[L - 1:L]
        kk = k_ref[rsl(c), ksl(h)]
        q_dec[h, c] = (q_ref[rsl(c), ksl(h)] * DK ** -0.5 * jnp.exp(bb)).astype(BF16)
        k_inv[h, c] = (kk * jnp.exp(-bb)).astype(BF16)
        k_end[h, c] = (kk * jnp.exp(b_last - bb)).astype(BF16)
        decay[h, c] = jnp.exp(b_last)
        vb[h, c] = v_ref[rsl(c), vsl(h)]
    scores = {p: _bdot_nt(q_dec[p], k_inv[p]) for p in hc}
    scores = {p: jnp.where(tril, scores[p], 0.0).astype(BF16) for p in hc}
    intra = {p: _bdot(scores[p], vb[p]) for p in hc}
    s_loc = {p: _bdot_tn(vb[p], k_end[p]) for p in hc}
    st = [st_ref[h] for h in range(H)]
    outs = {}
    for h, c in hc:
        outs[h, c] = intra[h, c] + _bdot_nt(q_dec[h, c], st[h])
        st[h] = st[h] * decay[h, c] + s_loc[h, c]
    for h in range(H):
        st_ref[h] = st[h]
        o = jnp.concatenate([outs[h, c] for c in range(nchunk)], axis=0)
        o = o * lax.rsqrt(jnp.mean(o * o, axis=-1, keepdims=True) + 1e-6) * ng_ref[...]
        o_ref[:, vsl(h)] = (o * _silu(r_ref[:, vsl(h)])).astype(o_ref.dtype)


def _gla(y1, y2, gates, wg2, bg, ng, *, batch, seq, tile):
    nt = seq // tile
    qk_w, v_w = GLA_HEADS * GLA_DK, GLA_HEADS * GLA_DV
    row = lambda b, t: b * nt + t
    return pl.pallas_call(
        functools.partial(_gla_kernel, tile=tile),
        grid=(batch, nt),
        in_specs=[pl.BlockSpec((tile, qk_w), lambda b, t: (row(b, t), 0)),
                  pl.BlockSpec((tile, qk_w), lambda b, t: (row(b, t), 1)),
                  pl.BlockSpec((tile, v_w), lambda b, t: (row(b, t), 1)),
                  pl.BlockSpec((tile, v_w), lambda b, t: (row(b, t), 0)),
                  pl.BlockSpec((tile, LANES), lambda b, t: (row(b, t), 0)),
                  pl.BlockSpec((GLA_GATE_RANK, qk_w), lambda b, t: (0, 0)),
                  pl.BlockSpec((1, qk_w), lambda b, t: (0, 0)),
                  pl.BlockSpec((1, GLA_DV), lambda b, t: (0, 0))],
        out_specs=pl.BlockSpec((tile, v_w), lambda b, t: (row(b, t), 0)),
        out_shape=jax.ShapeDtypeStruct((batch * seq, v_w), BF16),
        scratch_shapes=[pltpu.VMEM((GLA_HEADS, GLA_DV, GLA_DK), F32)],
        compiler_params=_cparams("parallel", "arbitrary"),
        name="gla",
    )(y1, y1, y1, y2, gates, wg2, bg, ng)


LOG2E = 1.4426950408889634
LN2 = 0.6931471805599453
DIL_STRIDE1 = 4


def _dil_kernel(*refs, unit):
    ngrp = len(DIL_PAIRS)
    in_refs, (o_ref, og_ref, lg_ref, rg_ref) = refs[:5 * ngrp], refs[5 * ngrp:]
    u = pl.program_id(1)
    n = DIL_BLOCK
    qscale = DIL_HEAD_DIM ** -0.5 * LOG2E
    row = lax.broadcasted_iota(jnp.int32, (n, n), 0)
    col = lax.broadcasted_iota(jnp.int32, (n, n), 1)
    prev_band, cur_band = col >= row, col <= row
    ones = jnp.ones((n, LANES), BF16)
    for g, (_, d) in enumerate(DIL_PAIRS):
        q_ref, k_ref, v_ref, kp_ref, vp_ref = in_refs[5 * g:5 * g + 5]
        sub = n * d
        blocks = []
        if d > DIL_STRIDE1:
            d2 = d // DIL_STRIDE1
            assert unit == sub and d2 <= DIL_STRIDE1
            per = unit // DIL_STRIDE1
            for a, src in enumerate((q_ref, k_ref, v_ref, kp_ref, vp_ref)):
                for r1 in range(DIL_STRIDE1):
                    rg_ref[a, r1] = src[pl.ds(r1, per, stride=DIL_STRIDE1), :]
            for r in range(d):
                at = (r % DIL_STRIDE1, pl.ds(r // DIL_STRIDE1, n, stride=d2))
                blocks.append(tuple((rg_ref, (a,) + at) for a in range(5))
                              + (prev_band & (u > 0), ((rg_ref, (5,) + at), (rg_ref, (6,) + at))))
        else:
            for s in range(unit // sub):
                for r in range(d):
                    cur = pl.ds(s * sub + r, n, stride=d) if d > 1 else pl.ds(s * sub, n)
                    outs = ((og_ref, (g, cur)), (lg_ref, (g, cur)))
                    if s == 0:
                        prv = pl.ds(r, n, stride=d) if d > 1 else pl.ds(0, n)
                        blocks.append(((q_ref, (cur,)), (k_ref, (cur,)), (v_ref, (cur,)), (kp_ref, (prv,)),
                                       (vp_ref, (prv,)), prev_band & (u > 0), outs))
                    else:
                        prv = pl.ds((s - 1) * sub + r, n, stride=d) if d > 1 else pl.ds((s - 1) * sub, n)
                        blocks.append(((q_ref, (cur,)), (k_ref, (cur,)), (v_ref, (cur,)), (k_ref, (prv,)),
                                       (v_ref, (prv,)), prev_band, outs))
        get = lambda ra: ra[0][ra[1] + (slice(None),)]
        scores = []
        for qa, kc, _, kp, _, pmask, _ in blocks:
            q = (get(qa) * qscale).astype(BF16)
            scores.append((jnp.where(pmask, _bdot_nt(q, get(kp)), NEG_INF),
                           jnp.where(cur_band, _bdot_nt(q, get(kc)), NEG_INF)))
        probs = []
        for s_p, s_c in scores:
            m = jnp.maximum(jnp.max(s_p, axis=-1, keepdims=True), jnp.max(s_c, axis=-1, keepdims=True))
            probs.append((jnp.exp2(s_p - m).astype(BF16), jnp.exp2(s_c - m).astype(BF16), m))
        for (_, _, vc, _, vp, _, ((o_dst, o_at), (l_dst, l_at))), (p_p, p_c, m) in zip(blocks, probs):
            den = _bdot(p_p, ones) + _bdot(p_c, ones)
            o_dst[o_at + (slice(None),)] = (_bdot(p_p, get(vp)) + _bdot(p_c, get(vc))) / den
            l_dst[l_at + (slice(None),)] = m * LN2 + jnp.log(den)
        if d > DIL_STRIDE1:
            for r1 in range(DIL_STRIDE1):
                og_ref[g, pl.ds(r1, per, stride=DIL_STRIDE1), :] = rg_ref[5, r1]
                lg_ref[g, pl.ds(r1, per, stride=DIL_STRIDE1), :] = rg_ref[6, r1]
    lses = [lg_ref[g] for g in range(ngrp)]
    m = functools.reduce(jnp.maximum, lses)
    ws = [jnp.exp(l - m) for l in lses]
    tot = functools.reduce(lambda a, b: a + b, ws)
    acc = sum(w * og_ref[g] for g, w in enumerate(ws))
    o_ref[...] = (acc / tot).astype(o_ref.dtype)


def _dilated(y, col0, *, batch, seq):
    ngrp, H, n = len(DIL_PAIRS), DIL_HEADS, DIL_BLOCK
    unit = max(d for _, d in DIL_PAIRS) * n
    assert seq % unit == 0
    nu = seq // unit
    cb0 = col0 // LANES
    in_specs, args = [], []
    for g, (_, d) in enumerate(DIL_PAIRS):
        sub = n * d
        per = unit // sub
        for part in range(3):
            cb = cb0 + part * ngrp * H + g * H
            in_specs.append(pl.BlockSpec((unit, LANES), lambda b, u, h, cb=cb: (b * nu + u, cb + h)))
            args.append(y)
        for part in (1, 2):
            cb = cb0 + part * ngrp * H + g * H
            in_specs.append(pl.BlockSpec(
                (sub, LANES),
                lambda b, u, h, cb=cb, per=per: (jnp.maximum(b * nu * per + u * per - 1, 0), cb + h)))
            args.append(y)
    return pl.pallas_call(
        functools.partial(_dil_kernel, unit=unit),
        grid=(batch, nu, H),
        in_specs=in_specs,
        out_specs=pl.BlockSpec((unit, LANES), lambda b, u, h: (b * nu + u, h)),
        out_shape=jax.ShapeDtypeStruct((batch * seq, H * LANES), BF16),
        scratch_shapes=[pltpu.VMEM((ngrp, unit, LANES), F32), pltpu.VMEM((ngrp, unit, LANES), F32),
                        pltpu.VMEM((7, DIL_STRIDE1, unit // DIL_STRIDE1, LANES), F32)],
        compiler_params=_cparams("parallel", "parallel", "arbitrary"),
        name="dilated",
    )(*args)


I_LANE, F_LANE = 0, MLSTM_HEADS


def _mlstm_kernel(qk_ref, v_ref, co_ref, gt_ref, cw_ref, cb_ref, gb_ref, ng_ref, o_ref,
                  c_ref, m_ref, tail_ref, xbuf_ref, *, tile):
    L, H, DK, DV = MLSTM_CHUNK, MLSTM_HEADS, MLSTM_DQK, MLSTM_DV
    KC = MLSTM_CONV
    PAD = 8

    @pl.when(pl.program_id(1) == 0)
    def _():
        c_ref[...] = jnp.zeros_like(c_ref)
        m_ref[...] = jnp.zeros_like(m_ref)
        tail_ref[...] = jnp.zeros_like(tail_ref)

    xbuf_ref[0:PAD, :] = tail_ref[...]
    xbuf_ref[PAD:PAD + tile, :] = qk_ref[...]
    tail_ref[...] = qk_ref[tile - PAD:tile, :]
    acc = cb_ref[...] + cw_ref[KC - 1:KC, :] * xbuf_ref[PAD:PAD + tile, :]
    for j in range(KC - 1):
        off = PAD - (KC - 1) + j
        acc = acc + cw_ref[j:j + 1, :] * xbuf_ref[off:off + tile, :]
    qk = _silu(acc)

    gates = gt_ref[...] + gb_ref[...]
    lane = lax.broadcasted_iota(jnp.int32, gates.shape, 1)
    z = jnp.where(lane >= F_LANE, _log_sigmoid(gates), gates)
    tril = _tril(L)
    tril_b = tril.astype(BF16)
    nchunk = tile // L
    lane_c = lax.broadcasted_iota(jnp.int32, (L, LANES), 1)
    zs, bs, wts = [], [], []
    for c in range(nchunk):
        zc = z[c * L:(c + 1) * L]
        bc = _chunk_cumsum(tril_b, zc)
        zs.append(zc)
        bs.append(bc)
        wts.append(jnp.where(lane_c >= F_LANE, bc, zc).T)
    ones_col = (lane_c == 0).astype(F32)

    hc = [(h, c) for c in range(nchunk) for h in range(H)]
    rsl = lambda c: slice(c * L, (c + 1) * L)
    row_c = lax.broadcasted_iota(jnp.int32, (L, LANES), 0)
    m_run = m_ref[0:1, :]
    kw_all, mt_all, isc_all, floor_all, so_all, sn_all = [], [], [], [], [], []
    for c in range(nchunk):
        bc = bs[c]
        li_a = pltpu.roll(zs[c], F_LANE - I_LANE, 1)
        b_last = bc[L - 1:L]
        a = b_last - bc + li_a
        m_loc = jnp.max(a, axis=0, keepdims=True)
        kw_all.append(jnp.exp(a - m_loc))
        x = li_a - bc
        for s in (1, 2, 4, 8, 16, 32):
            x = jnp.maximum(x, jnp.where(row_c >= s, pltpu.roll(x, s, 0), NEG_INF))
        inter_log = bc + m_run
        m_t = jnp.maximum(inter_log, bc + x)
        mt_all.append(m_t)
        isc_all.append(jnp.exp(inter_log - m_t))
        floor_all.append(jnp.exp(-m_t))
        m_new = jnp.maximum(b_last + m_run, m_loc)
        so_all.append(jnp.exp(b_last + m_run - m_new))
        sn_all.append(jnp.exp(m_loc - m_new))
        m_run = m_new
    m_ref[0:1, :] = m_run
    col = lambda x, h: x[:, F_LANE + h:F_LANE + h + 1]
    qb, kb, vb, qkm = {}, {}, {}, {}
    for h, c in hc:
        qb[h, c] = qk[rsl(c), h * DK:(h + 1) * DK].astype(BF16)
        kb[h, c] = qk[rsl(c), (H + h) * DK:(H + h + 1) * DK] * DK ** -0.5
        vb[h, c] = v_ref[rsl(c), h * DV:(h + 1) * DV]
    raw = {p: _bdot_nt(qb[p], kb[p]) for p in hc}
    for h, c in hc:
        b_row = wts[c][F_LANE + h:F_LANE + h + 1, :]
        li_row = wts[c][I_LANE + h:I_LANE + h + 1, :]
        d_log = jnp.where(tril, col(bs[c], h) - b_row + li_row, NEG_INF)
        qkm[h, c] = raw[h, c] * jnp.exp(d_log - col(mt_all[c], h))
    v_aug = {p: jnp.concatenate([vb[p], ones_col], axis=1).astype(BF16) for p in hc}
    intra = {p: _bdot(qkm[p], v_aug[p]) for p in hc}
    c_loc = {(h, c): _bdot_tn(kb[h, c] * col(kw_all[c], h), v_aug[h, c]) for h, c in hc}
    ct = [c_ref[h] for h in range(H)]
    outs = {}
    for h, c in hc:
        p = (h, c)
        tot = intra[p] + col(isc_all[c], h) * _bdot(qb[p], ct[h])
        outs[p] = tot[:, :DV] / jnp.maximum(jnp.abs(tot[:, DV:DV + 1]), col(floor_all[c], h))
        ct[h] = ct[h] * col(so_all[c], h) + c_loc[p] * col(sn_all[c], h)
    for h in range(H):
        c_ref[h] = ct[h]
        hcat = jnp.concatenate([outs[h, c] for c in range(nchunk)], axis=0)
        mu = jnp.mean(hcat, axis=-1, keepdims=True)
        hcen = hcat - mu
        var = jnp.mean(hcen * hcen, axis=-1, keepdims=True)
        hn = hcen * lax.rsqrt(var + 1e-5) * ng_ref[...]
        o_ref[:, h * DV:(h + 1) * DV] = (_sigmoid(co_ref[:, h * DV:(h + 1) * DV]) * hn).astype(o_ref.dtype)


def _mlstm(y1, y2, gates, conv_w, conv_b, gate_bias, ng, *, batch, seq, tile):
    nt = seq // tile
    H, DK, DV = MLSTM_HEADS, MLSTM_DQK, MLSTM_DV
    qk_w, v_w = 2 * H * DK, H * DV
    row = lambda b, t: b * nt + t
    return pl.pallas_call(
        functools.partial(_mlstm_kernel, tile=tile),
        grid=(batch, nt),
        in_specs=[pl.BlockSpec((tile, qk_w), lambda b, t: (row(b, t), 0)),
                  pl.BlockSpec((tile, v_w), lambda b, t: (row(b, t), 1)),
                  pl.BlockSpec((tile, v_w), lambda b, t: (row(b, t), 0)),
                  pl.BlockSpec((tile, LANES), lambda b, t: (row(b, t), 0)),
                  pl.BlockSpec((MLSTM_CONV, qk_w), lambda b, t: (0, 0)),
                  pl.BlockSpec((1, qk_w), lambda b, t: (0, 0)),
                  pl.BlockSpec((1, LANES), lambda b, t: (0, 0)),
                  pl.BlockSpec((1, DV), lambda b, t: (0, 0))],
        out_specs=pl.BlockSpec((tile, v_w), lambda b, t: (row(b, t), 0)),
        out_shape=jax.ShapeDtypeStruct((batch * seq, v_w), BF16),
        scratch_shapes=[pltpu.VMEM((H, DK, DV + LANES), F32),
                        pltpu.VMEM((8, LANES), F32),
                        pltpu.VMEM((8, qk_w), F32),
                        pltpu.VMEM((tile + 8, qk_w), F32)],
        compiler_params=_cparams("parallel", "arbitrary"),
        name="mlstm",
    )(y1, y1, y2, gates, conv_w, conv_b, gate_bias, ng)


def _rope128(x, c_ref, s1_ref, s2_ref):
    return x * c_ref[...] + pltpu.roll(x, 32, 1) * s1_ref[...] + pltpu.roll(x, LANES - 32, 1) * s2_ref[...]


def _rms(x, g):
    return x * lax.rsqrt(jnp.mean(x * x, axis=-1, keepdims=True) + 1e-6) * g


MLA_SCORE_SCALE = (MLA_NOPE + MLA_ROPE) ** -0.5 * 1.4426950408889634


def _mla_q_kernel(x_ref, g_ref, w_ref, c_ref, s1_ref, s2_ref, o_ref):
    q = _bdot(_rms(x_ref[...], g_ref[...]), w_ref[...]) * MLA_SCORE_SCALE
    for h in range(MLA_HEADS):
        o_ref[:, 256 * h:256 * h + 128] = q[:, 256 * h:256 * h + 128].astype(o_ref.dtype)
        o_ref[:, 256 * h + 128:256 * h + 256] = _rope128(
            q[:, 256 * h + 128:256 * h + 256], c_ref, s1_ref, s2_ref).astype(o_ref.dtype)


def _mla_kv_kernel(x_ref, g_ref, wk_ref, wvt_ref, kr_ref, c_ref, s1_ref, s2_ref, k_ref, vt_ref):
    xn = _rms(x_ref[...], g_ref[...]).astype(BF16)
    kn = jnp.dot(xn, wk_ref[...], preferred_element_type=F32)
    vt_ref[...] = lax.dot_general(wvt_ref[...], xn, _NT, preferred_element_type=F32).astype(vt_ref.dtype)
    kr = _rope128(kr_ref[...], c_ref, s1_ref, s2_ref).astype(k_ref.dtype)
    for h in range(MLA_HEADS):
        k_ref[:, 256 * h:256 * h + 128] = kn[:, 128 * h:128 * h + 128].astype(k_ref.dtype)
        k_ref[:, 256 * h + 128:256 * h + 256] = kr


def _mla_proj(y, qcb, gates, qg, kvg, wq, wk, wvt, tabs, *, tm):
    m = y.shape[0]
    H = MLA_HEADS
    tab_specs = [pl.BlockSpec((tm, LANES), lambda i: (i, 0))] * 3
    full = lambda a: pl.BlockSpec(a.shape, lambda i: (0, 0))
    qf = pl.pallas_call(
        _mla_q_kernel,
        grid=(m // tm,),
        in_specs=[pl.BlockSpec((tm, MLA_Q_RANK), lambda i: (i, qcb)), full(qg), full(wq)] + tab_specs,
        out_specs=pl.BlockSpec((tm, H * 256), lambda i: (i, 0)),
        out_shape=jax.ShapeDtypeStruct((m, H * 256), BF16),
        compiler_params=_cparams("parallel"),
        name="mla_q",
    )(y, qg, wq, *tabs)
    kf, vt = pl.pallas_call(
        _mla_kv_kernel,
        grid=(m // tm,),
        in_specs=[pl.BlockSpec((tm, MLA_KV_RANK), lambda i: (i, qcb + 1)), full(kvg), full(wk), full(wvt),
                  pl.BlockSpec((tm, LANES), lambda i: (i, 0))] + tab_specs,
        out_specs=[pl.BlockSpec((tm, H * 256), lambda i: (i, 0)),
                   pl.BlockSpec((H * MLA_DV, tm), lambda i: (0, i))],
        out_shape=[jax.ShapeDtypeStruct((m, H * 256), BF16), jax.ShapeDtypeStruct((H * MLA_DV, m), BF16)],
        compiler_params=_cparams("parallel"),
        name="mla_kv",
    )(y, kvg, wk, wvt, gates, *tabs)
    return qf, kf, vt


def _mla_attn_kernel(q_ref, k_ref, vt_ref, o_ref, s_ref, mx_ref, *, tq, hg):
    i = pl.program_id(2)
    DV = MLA_DV

    def scores(j, slot):
        start = pl.multiple_of(j * tq, tq)
        for g in range(hg):
            st = lax.dot_general(k_ref[pl.ds(start, tq), 256 * g:256 * (g + 1)],
                                 q_ref[:, 256 * g:256 * (g + 1)], _NT, preferred_element_type=F32)
            s_ref[slot, g] = st
            mx_ref[slot, g] = jnp.max(st, axis=0, keepdims=True)

    def step(j, slot, carry, masked):
        start = pl.multiple_of(j * tq, tq)
        out = []
        for g in range(hg):
            m, l, acc = carry[g]
            vtb = vt_ref[DV * g:DV * (g + 1), pl.ds(start, tq)]
            st = s_ref[slot, g]
            if masked:
                kk = lax.broadcasted_iota(jnp.int32, (tq, tq), 0)
                qq = lax.broadcasted_iota(jnp.int32, (tq, tq), 1)
                st = jnp.where(kk <= qq, st, NEG_INF)
                m_new = jnp.maximum(m, jnp.max(st, axis=0, keepdims=True))
            else:
                m_new = jnp.maximum(m, mx_ref[slot, g])
            pt = jnp.exp2(st - m_new)
            a = jnp.exp2(m - m_new)
            l = a * l + jnp.sum(pt, axis=0, keepdims=True)
            acc = a * acc + jnp.dot(vtb, pt.astype(BF16), preferred_element_type=F32)
            out.append((m_new, l, acc))
        return tuple(out)

    init = tuple((jnp.full((1, tq), NEG_INF, F32), jnp.zeros((1, tq), F32), jnp.zeros((DV, tq), F32))
                 for _ in range(hg))

    def pair(jj, carry):
        scores(2 * jj + 1, 1)
        carry = step(2 * jj, 0, carry, False)
        scores(2 * jj + 2, 0)
        return step(2 * jj + 1, 1, carry, False)

    def finish(carry):
        for g in range(hg):
            _, l, acc = carry[g]
            o_ref[:, DV * g:DV * (g + 1)] = (acc / l).T.astype(o_ref.dtype)

    scores(0, 0)
    carry = lax.fori_loop(0, i // 2, pair, init)

    @pl.when(i % 2 == 0)
    def _():
        finish(step(i, 0, carry, True))

    @pl.when(i % 2 == 1)
    def _():
        scores(i, 1)
        finish(step(i, 1, step(i - 1, 0, carry, False), True))


def _mla_attn(qf, kf, vt, *, batch, seq, tq, hg=4):
    H = MLA_HEADS
    nq = seq // tq
    return pl.pallas_call(
        functools.partial(_mla_attn_kernel, tq=tq, hg=hg),
        grid=(batch, H // hg, nq),
        in_specs=[pl.BlockSpec((tq, 256 * hg), lambda b, h, i: (b * nq + i, h)),
                  pl.BlockSpec((seq, 256 * hg), lambda b, h, i: (b, h)),
                  pl.BlockSpec((MLA_DV * hg, seq), lambda b, h, i: (h, b))],
        out_specs=pl.BlockSpec((tq, MLA_DV * hg), lambda b, h, i: (b * nq + i, h)),
        out_shape=jax.ShapeDtypeStruct((batch * seq, H * MLA_DV), BF16),
        scratch_shapes=[pltpu.VMEM((2, hg, tq, tq), F32), pltpu.VMEM((2, hg, 1, tq), F32)],
        compiler_params=_cparams("parallel", "parallel", "arbitrary"),
        name="mla_attn",
    )(qf, kf, vt)


def _tile_for(n, pref):
    t = pref
    while n % t:
        t //= 2
    return t


def _even_mixer(x_bf, wt_all, w_o_all, j, wg2, bg, ng, *, batch, seq):
    m = x_bf.shape[0]
    gq_gk_gv = 2 * GLA_HEADS * GLA_DK + GLA_HEADS * GLA_DV
    rest0 = gq_gk_gv + GLA_GATE_RANK
    tm = _tile_for(m, 2048)
    mm = functools.partial(_matmul_t, x_bf, wt_all, layer=j, tm=tm, out_dtype=F32)
    y1, gts, wo_bf = mm(row0=0, ncols=gq_gk_gv, tn=512, side_rows=(gq_gk_gv,), cast=(w_o_all, j))
    y2 = mm(row0=rest0, ncols=wt_all.shape[1] - rest0, tn=512)
    o_a = _gla(y1, y2, gts, wg2, bg.reshape(1, -1), ng.reshape(1, -1), batch=batch, seq=seq,
               tile=_tile_for(seq, 512))
    o_b = _dilated(y2, GLA_HEADS * GLA_DV, batch=batch, seq=seq)
    return (o_a, o_b), wo_bf


def _rope_tables(positions):
    inv_freq = ROPE_THETA ** (-jnp.arange(0, MLA_ROPE, 2, dtype=F32) / MLA_ROPE)
    ang = positions.astype(F32).reshape(-1, 1) * inv_freq
    cos, sin = jnp.cos(ang), jnp.sin(ang)
    z32 = jnp.zeros_like(cos)
    z64 = jnp.zeros((cos.shape[0], LANES - MLA_ROPE), F32)
    c = jnp.concatenate([z64, cos, cos], axis=1)
    s1 = jnp.concatenate([z64, z32, sin], axis=1)
    s2 = jnp.concatenate([z64, -sin, z32], axis=1)
    return c, s1, s2


def _odd_mixer(x_bf, tabs, wt_all, w_o_all, j, conv_w, conv_b, bi, bf, ng, qg, kvg, wuq, wukv, *, batch, seq):
    m = x_bf.shape[0]
    H = MLA_HEADS
    cq_ck_cv = 2 * MLSTM_HEADS * MLSTM_DQK + MLSTM_HEADS * MLSTM_DV
    co0 = cq_ck_cv + 2 * MLSTM_HEADS
    kr0 = co0 + MLSTM_HEADS * MLSTM_DV + MLA_Q_RANK + MLA_KV_RANK
    gate_bias = jnp.concatenate([bi, bf, jnp.zeros((LANES - 2 * MLSTM_HEADS,), F32)]).reshape(1, LANES)
    wq3 = wuq.reshape(-1, H, MLA_NOPE + MLA_ROPE)
    wq = jnp.concatenate([wq3[:, :, :MLA_NOPE], jnp.zeros(wq3.shape[:2] + (LANES - MLA_ROPE,), F32),
                          wq3[:, :, MLA_NOPE:]], axis=2).reshape(-1, H * 256).astype(BF16)
    wkv = wukv.reshape(-1, H, MLA_NOPE + MLA_DV)
    wk = wkv[:, :, :MLA_NOPE].reshape(-1, H * MLA_NOPE).astype(BF16)
    wvt = wkv[:, :, MLA_NOPE:].reshape(-1, H * MLA_DV).T.astype(BF16)
    tm = _tile_for(m, 2048)
    mm = functools.partial(_matmul_t, x_bf, wt_all, layer=j, tm=tm, out_dtype=F32)
    y1, gts, kr, wo_bf = mm(row0=0, ncols=cq_ck_cv, tn=512, side_rows=(cq_ck_cv, kr0 + MLA_ROPE - LANES),
                            cast=(w_o_all, j))
    y2 = mm(row0=co0, ncols=kr0 - co0, tn=512)
    o_c = _mlstm(y1, y2, gts, conv_w, conv_b.reshape(1, -1), gate_bias, ng.reshape(1, -1),
                 batch=batch, seq=seq, tile=_tile_for(seq, 512))
    qf, kf, vt = _mla_proj(y2, MLSTM_HEADS * MLSTM_DV // MLA_Q_RANK, kr, qg.reshape(1, -1), kvg.reshape(1, -1),
                           wq, wk, wvt, tabs, tm=_tile_for(m, 512))
    o_d = _mla_attn(qf, kf, vt, batch=batch, seq=seq, tq=_tile_for(seq, 512))
    return (o_c, o_d), wo_bf


def kernel(x, positions, even_w_in, even_gla_wg2, even_gla_bg, even_gla_norm_g, even_w_o, odd_w_in, odd_conv_w, odd_conv_b, odd_mlstm_bi, odd_mlstm_bf, odd_mlstm_norm_g, odd_mla_qnorm_g, odd_mla_kvnorm_g, odd_mla_wuq, odd_mla_wukv, odd_w_o, ln1_g, ln1_b, ffn_wgu, ffn_wd, ln2_g, ln2_b):
    batch, seq, d = x.shape
    m = batch * seq
    xf = x.reshape(m, d)
    xb = xf.astype(BF16)
    tabs = _rope_tables(positions)
    even_wt = jnp.swapaxes(even_w_in, 1, 2)
    odd_wt = jnp.swapaxes(odd_w_in, 1, 2)
    depth = ln1_g.shape[0]
    tm_ln = _tile_for(m, 512)
    for l in range(depth):
        j = l // 2
        if l % 2 == 0:
            parts, wo_bf = _even_mixer(xb, even_wt, even_w_o, j, even_gla_wg2[j], even_gla_bg[j],
                                       even_gla_norm_g[j], batch=batch, seq=seq)
        else:
            parts, wo_bf = _odd_mixer(xb, tabs, odd_wt, odd_w_o, j, odd_conv_w[j], odd_conv_b[j],
                                      odd_mlstm_bi[j], odd_mlstm_bf[j], odd_mlstm_norm_g[j], odd_mla_qnorm_g[j],
                                      odd_mla_kvnorm_g[j], odd_mla_wuq[j], odd_mla_wukv[j], batch=batch, seq=seq)
        xf, xb = _mm_ln(parts, wo_bf, xf, ln1_g[l].reshape(1, -1), ln1_b[l].reshape(1, -1), tm=tm_ln)
        hid, wd_bf = _ffn_up(xb, ffn_wgu, ffn_wd, tm=_tile_for(m, 2048), tn=512, layer=l)
        xf, xb = _mm_ln((hid,), wd_bf, xf, ln2_g[l].reshape(1, -1), ln2_b[l].reshape(1, -1),
                        tm=_tile_for(m, 256))
    return xf.reshape(batch, seq, d)
```

```python
import functools

import jax
import jax.numpy as jnp
from jax import lax
from jax.experimental import pallas as pl
from jax.experimental.pallas import tpu as pltpu

F32 = jnp.float32
BF16 = jnp.bfloat16

D_MODEL = 2048
DEPTH = 4
GLA_HEADS, GLA_DK, GLA_DV, GLA_GATE_RANK, GLA_TAU, GLA_CHUNK = 4, 128, 256, 16, 16.0, 64
DIL_PAIRS = ((128, 1), (512, 4), (2048, 16))
DIL_HEADS, DIL_HEAD_DIM, DIL_BLOCK = 4, 128, 128
MLSTM_HEADS, MLSTM_DQK, MLSTM_DV, MLSTM_CONV, MLSTM_CHUNK = 4, 128, 256, 4, 64
MLA_HEADS, MLA_Q_RANK, MLA_KV_RANK, MLA_NOPE, MLA_ROPE, MLA_DV = 8, 512, 512, 128, 64, 128
ROPE_THETA = 10000.0
FFN_HIDDEN = 5632
DEEPNORM_ALPHA = (2.0 * DEPTH) ** 0.25

LANES = 128
VMEM_LIMIT = 56 * 1024 * 1024
NEG_INF = float("-inf")

_NT = (((1,), (1,)), ((), ()))


def _cparams(*sem):
    return pltpu.CompilerParams(dimension_semantics=sem, vmem_limit_bytes=VMEM_LIMIT)


def _bdot(a, b):
    return jnp.dot(a.astype(BF16), b.astype(BF16), preferred_element_type=F32)


def _bdot_nt(a, b):
    return lax.dot_general(a.astype(BF16), b.astype(BF16), _NT, preferred_element_type=F32)


def _bdot_tn(a, b):
    return jnp.dot(a.astype(F32).T.astype(BF16), b.astype(BF16), preferred_element_type=F32)


def _split2(a):
    hi = a.astype(BF16)
    lo = (a - hi.astype(F32)).astype(BF16)
    return hi, lo


def _dot3(a, b):
    ah, al = _split2(a)
    bh, bl = _split2(b)
    f = functools.partial(jnp.dot, preferred_element_type=F32)
    return f(ah, bh) + (f(ah, bl) + f(al, bh))


def _chunk_cumsum(tril_b, x):
    hi, lo = _split2(x)
    lo2 = (x - hi.astype(F32) - lo.astype(F32)).astype(BF16)
    f = functools.partial(jnp.dot, preferred_element_type=F32)
    return f(tril_b, hi) + (f(tril_b, lo) + f(tril_b, lo2))


def _tril(n):
    r = lax.broadcasted_iota(jnp.int32, (n, n), 0)
    c = lax.broadcasted_iota(jnp.int32, (n, n), 1)
    return c <= r


def _log_sigmoid(x):
    return jnp.minimum(x, 0.0) - jnp.log1p(jnp.exp(-jnp.abs(x)))


def _sigmoid(x):
    return 1.0 / (1.0 + jnp.exp(-x))


def _silu(x):
    return x * _sigmoid(x)


def _wspec(w, layer, rows, cols, index_map):
    if w.ndim == 2:
        return pl.BlockSpec((rows, cols), index_map)
    return pl.BlockSpec((None, rows, cols), lambda *ids: (layer,) + tuple(index_map(*ids)))


def _mm_nt_kernel(*refs, nside, ncast):
    a_ref, wt_ref, side_w, cast_in = refs[0], refs[1], refs[2:2 + nside], refs[2 + nside:2 + nside + ncast]
    outs = refs[2 + nside + ncast:]
    o_ref, side_o, cast_out = outs[0], outs[1:1 + nside], outs[1 + nside:]

    o_ref[...] = lax.dot_general(a_ref[...], wt_ref[0].astype(BF16), _NT,
                                 preferred_element_type=F32).astype(o_ref.dtype)
    for c_in, c_out in zip(cast_in, cast_out):
        c_out[...] = c_in[...].astype(c_out.dtype)

    if not side_w:
        return

    @pl.when(pl.program_id(1) == 0)
    def _():
        w = jnp.concatenate([w_ref[0] for w_ref in side_w], axis=0).astype(BF16)
        prod = lax.dot_general(a_ref[...], w, _NT, preferred_element_type=F32)
        for n, s_ref in enumerate(side_o):
            s_ref[...] = prod[:, n * LANES:(n + 1) * LANES].astype(s_ref.dtype)


def _matmul_t(a, wt, *, layer, row0, ncols, tm, tn, out_dtype, side_rows=(), cast=None):
    m, k = a.shape
    nj = ncols // tn
    wspec = lambda rows, imap: pl.BlockSpec((pl.Element(1), pl.Element(rows), pl.Element(k)), imap)
    in_specs = ([pl.BlockSpec((tm, k), lambda i, j: (i, 0)),
                 wspec(tn, lambda i, j: (layer, pl.multiple_of(row0 + j * tn, 8), 0))]
                + [wspec(LANES, lambda i, j, r=r: (layer, r, 0)) for r in side_rows])
    out_specs = ([pl.BlockSpec((tm, tn), lambda i, j: (i, j))]
                 + [pl.BlockSpec((tm, LANES), lambda i, j: (i, 0)) for _ in side_rows])
    out_shape = ([jax.ShapeDtypeStruct((m, ncols), out_dtype)]
                 + [jax.ShapeDtypeStruct((m, LANES), out_dtype) for _ in side_rows])
    args = [a, wt] + [wt] * len(side_rows)
    if cast is not None:
        cw, cl = cast
        rows, cols = cw.shape[1:]
        slab = rows // ((m // tm) * nj)
        assert slab * (m // tm) * nj == rows and slab % 16 == 0
        in_specs.append(pl.BlockSpec((None, slab, cols), lambda i, j: (cl, i * nj + j, 0)))
        out_specs.append(pl.BlockSpec((slab, cols), lambda i, j: (i * nj + j, 0)))
        out_shape.append(jax.ShapeDtypeStruct((rows, cols), BF16))
        args.append(cw)
    outs = pl.pallas_call(
        functools.partial(_mm_nt_kernel, nside=len(side_rows), ncast=int(cast is not None)),
        grid=(m // tm, nj),
        in_specs=in_specs, out_specs=out_specs, out_shape=out_shape,
        compiler_params=_cparams("parallel", "arbitrary"),
        name="mm_nt",
    )(*args)
    return outs[0] if len(outs) == 1 else tuple(outs)


def _ffn_up_kernel(x_ref, wg_ref, wu_ref, wd_ref, o_ref, wdb_ref):
    x = x_ref[...]
    g = jnp.dot(x, wg_ref[...].astype(BF16), preferred_element_type=F32)
    u = jnp.dot(x, wu_ref[...].astype(BF16), preferred_element_type=F32)
    o_ref[...] = (_silu(g) * u).astype(o_ref.dtype)
    wdb_ref[...] = wd_ref[...].astype(wdb_ref.dtype)


def _ffn_up(x_bf, w_gu, w_d, *, tm, tn, layer):
    m, k = x_bf.shape
    hidden = w_gu.shape[-1] // 2
    nb = hidden // tn
    steps = (m // tm) * nb
    slab = hidden // steps
    assert slab * steps == hidden and slab % 16 == 0
    return pl.pallas_call(
        _ffn_up_kernel,
        grid=(m // tm, nb),
        in_specs=[pl.BlockSpec((tm, k), lambda i, j: (i, 0)),
                  _wspec(w_gu, layer, k, tn, lambda i, j: (0, j)),
                  _wspec(w_gu, layer, k, tn, lambda i, j: (0, j + nb)),
                  _wspec(w_d, layer, slab, w_d.shape[-1], lambda i, j: (i * nb + j, 0))],
        out_specs=[pl.BlockSpec((tm, tn), lambda i, j: (i, j)),
                   pl.BlockSpec((slab, w_d.shape[-1]), lambda i, j: (i * nb + j, 0))],
        out_shape=[jax.ShapeDtypeStruct((m, hidden), BF16),
                   jax.ShapeDtypeStruct((hidden, w_d.shape[-1]), BF16)],
        compiler_params=_cparams("parallel", "arbitrary"),
        name="ffn_up",
    )(x_bf, w_gu, w_gu, w_d)


def _mm_ln_kernel(*refs):
    a_refs, (w_ref, res_ref, g_ref, b_ref, of_ref, ob_ref) = refs[:-6], refs[-6:]
    a = a_refs[0][...] if len(a_refs) == 1 else jnp.concatenate([r[...] for r in a_refs], axis=1)
    y = DEEPNORM_ALPHA * res_ref[...] + jnp.dot(a, w_ref[...], preferred_element_type=F32)
    mu = jnp.mean(y, axis=-1, keepdims=True)
    yc = y - mu
    var = jnp.mean(yc * yc, axis=-1, keepdims=True)
    out = yc * lax.rsqrt(var + 1e-5) * g_ref[...] + b_ref[...]
    of_ref[...] = out
    ob_ref[...] = out.astype(BF16)


def _mm_ln(parts, w, res, g, b, *, tm):
    m = res.shape[0]
    kdim, n = w.shape
    widths = tuple(p.shape[1] for p in parts)
    assert sum(widths) == kdim
    return pl.pallas_call(
        _mm_ln_kernel,
        grid=(m // tm,),
        in_specs=[pl.BlockSpec((tm, wd), lambda i: (i, 0)) for wd in widths] + [
            pl.BlockSpec((kdim, n), lambda i: (0, 0), pipeline_mode=pl.Buffered(1)),
            pl.BlockSpec((tm, n), lambda i: (i, 0)),
            pl.BlockSpec((1, n), lambda i: (0, 0)),
            pl.BlockSpec((1, n), lambda i: (0, 0))],
        out_specs=[pl.BlockSpec((tm, n), lambda i: (i, 0)),
                   pl.BlockSpec((tm, n), lambda i: (i, 0))],
        out_shape=[jax.ShapeDtypeStruct((m, n), F32), jax.ShapeDtypeStruct((m, n), BF16)],
        compiler_params=_cparams("parallel"),
        name="mm_ln",
    )(*parts, w, res, g, b)


def _gla_kernel(q_ref, k_ref, v_ref, r_ref, g_ref, wg2_ref, bg_ref, ng_ref, o_ref, st_ref, *, tile):
    L, H, DK, DV = GLA_CHUNK, GLA_HEADS, GLA_DK, GLA_DV

    @pl.when(pl.program_id(1) == 0)
    def _():
        st_ref[...] = jnp.zeros_like(st_ref)

    tril = _tril(L)
    tril_b = tril.astype(BF16)
    g16 = g_ref[:, :GLA_GATE_RANK]
    nchunk = tile // L
    hc = [(h, c) for c in range(nchunk) for h in range(H)]
    ksl = lambda h: slice(h * DK, (h + 1) * DK)
    vsl = lambda h: slice(h * DV, (h + 1) * DV)
    rsl = lambda c: slice(c * L, (c + 1) * L)
    log_a = [_log_sigmoid(_dot3(g16, wg2_ref[:, ksl(h)]) + bg_ref[:, ksl(h)]) / GLA_TAU for h in range(H)]
    b = {(h, c): _chunk_cumsum(tril_b, log_a[h][rsl(c)]) for h, c in hc}
    q_dec, k_inv, k_end, decay, vb = {}, {}, {}, {}, {}
    for h, c in hc:
        bb = b[h, c]
        b_last = bb[L - 1:L]
        kk = k_ref[rsl(c), ksl(h)]
        q_dec[h, c] = (q_ref[rsl(c), ksl(h)] * DK ** -0.5 * jnp.exp(bb)).astype(BF16)
        k_inv[h, c] = (kk * jnp.exp(-bb)).astype(BF16)
        k_end[h, c] = (kk * jnp.exp(b_last - bb)).astype(BF16)
        decay[h, c] = jnp.exp(b_last)
        vb[h, c] = v_ref[rsl(c), vsl(h)]
    scores = {p: _bdot_nt(q_dec[p], k_inv[p]) for p in hc}
    scores = {p: jnp.where(tril, scores[p], 0.0).astype(BF16) for p in hc}
    intra = {p: _bdot(scores[p], vb[p]) for p in hc}
    s_loc = {p: _bdot_tn(vb[p], k_end[p]) for p in hc}
    st = [st_ref[h] for h in range(H)]
    outs = {}
    for h, c in hc:
        outs[h, c] = intra[h, c] + _bdot_nt(q_dec[h, c], st[h])
        st[h] = st[h] * decay[h, c] + s_loc[h, c]
    for h in range(H):
        st_ref[h] = st[h]
        o = jnp.concatenate([outs[h, c] for c in range(nchunk)], axis=0)
        o = o * lax.rsqrt(jnp.mean(o * o, axis=-1, keepdims=True) + 1e-6) * ng_ref[...]
        o_ref[:, vsl(h)] = (o * _silu(r_ref[:, vsl(h)])).astype(o_ref.dtype)


def _gla(y1, y2, gates, wg2, bg, ng, *, batch, seq, tile):
    nt = seq // tile
    qk_w, v_w = GLA_HEADS * GLA_DK, GLA_HEADS * GLA_DV
    row = lambda b, t: b * nt + t
    return pl.pallas_call(
        functools.partial(_gla_kernel, tile=tile),
        grid=(batch, nt),
        in_specs=[pl.BlockSpec((tile, qk_w), lambda b, t: (row(b, t), 0)),
                  pl.BlockSpec((tile, qk_w), lambda b, t: (row(b, t), 1)),
                  pl.BlockSpec((tile, v_w), lambda b, t: (row(b, t), 1)),
                  pl.BlockSpec((tile, v_w), lambda b, t: (row(b, t), 0)),
                  pl.BlockSpec((tile, LANES), lambda b, t: (row(b, t), 0)),
                  pl.BlockSpec((GLA_GATE_RANK, qk_w), lambda b, t: (0, 0)),
                  pl.BlockSpec((1, qk_w), lambda b, t: (0, 0)),
                  pl.BlockSpec((1, GLA_DV), lambda b, t: (0, 0))],
        out_specs=pl.BlockSpec((tile, v_w), lambda b, t: (row(b, t), 0)),
        out_shape=jax.ShapeDtypeStruct((batch * seq, v_w), BF16),
        scratch_shapes=[pltpu.VMEM((GLA_HEADS, GLA_DV, GLA_DK), F32)],
        compiler_params=_cparams("parallel", "arbitrary"),
        name="gla",
    )(y1, y1, y1, y2, gates, wg2, bg, ng)


LOG2E = 1.4426950408889634
LN2 = 0.6931471805599453
DIL_STRIDE1 = 4


def _dil_kernel(*refs, unit):
    ngrp = len(DIL_PAIRS)
    in_refs, (o_ref, og_ref, lg_ref, rg_ref) = refs[:5 * ngrp], refs[5 * ngrp:]
    u = pl.program_id(1)
    n = DIL_BLOCK
    qscale = DIL_HEAD_DIM ** -0.5 * LOG2E
    row = lax.broadcasted_iota(jnp.int32, (n, n), 0)
    col = lax.broadcasted_iota(jnp.int32, (n, n), 1)
    prev_band, cur_band = col >= row, col <= row
    ones = jnp.ones((n, LANES), BF16)
    for g, (_, d) in enumerate(DIL_PAIRS):
        q_ref, k_ref, v_ref, kp_ref, vp_ref = in_refs[5 * g:5 * g + 5]
        sub = n * d
        blocks = []
        if d > DIL_STRIDE1:
            d2 = d // DIL_STRIDE1
            assert unit == sub and d2 <= DIL_STRIDE1
            per = unit // DIL_STRIDE1
            for a, src in enumerate((q_ref, k_ref, v_ref, kp_ref, vp_ref)):
                for r1 in range(DIL_STRIDE1):
                    rg_ref[a, r1] = src[pl.ds(r1, per, stride=DIL_STRIDE1), :]
            for r in range(d):
                at = (r % DIL_STRIDE1, pl.ds(r // DIL_STRIDE1, n, stride=d2))
                blocks.append(tuple((rg_ref, (a,) + at) for a in range(5))
                              + (prev_band & (u > 0), ((rg_ref, (5,) + at), (rg_ref, (6,) + at))))
        else:
            for s in range(unit // sub):
                for r in range(d):
                    cur = pl.ds(s * sub + r, n, stride=d) if d > 1 else pl.ds(s * sub, n)
                    outs = ((og_ref, (g, cur)), (lg_ref, (g, cur)))
                    if s == 0:
                        prv = pl.ds(r, n, stride=d) if d > 1 else pl.ds(0, n)
                        blocks.append(((q_ref, (cur,)), (k_ref, (cur,)), (v_ref, (cur,)), (kp_ref, (prv,)),
                                       (vp_ref, (prv,)), prev_band & (u > 0), outs))
                    else:
                        prv = pl.ds((s - 1) * sub + r, n, stride=d) if d > 1 else pl.ds((s - 1) * sub, n)
                        blocks.append(((q_ref, (cur,)), (k_ref, (cur,)), (v_ref, (cur,)), (k_ref, (prv,)),
                                       (v_ref, (prv,)), prev_band, outs))
        get = lambda ra: ra[0][ra[1] + (slice(None),)]
        scores = []
        for qa, kc, _, kp, _, pmask, _ in blocks:
            q = (get(qa) * qscale).astype(BF16)
            scores.append((jnp.where(pmask, _bdot_nt(q, get(kp)), NEG_INF),
                           jnp.where(cur_band, _bdot_nt(q, get(kc)), NEG_INF)))
        probs = []
        for s_p, s_c in scores:
            m = jnp.maximum(jnp.max(s_p, axis=-1, keepdims=True), jnp.max(s_c, axis=-1, keepdims=True))
            probs.append((jnp.exp2(s_p - m).astype(BF16), jnp.exp2(s_c - m).astype(BF16), m))
        for (_, _, vc, _, vp, _, ((o_dst, o_at), (l_dst, l_at))), (p_p, p_c, m) in zip(blocks, probs):
            den = _bdot(p_p, ones) + _bdot(p_c, ones)
            o_dst[o_at + (slice(None),)] = (_bdot(p_p, get(vp)) + _bdot(p_c, get(vc))) / den
            l_dst[l_at + (slice(None),)] = m * LN2 + jnp.log(den)
        if d > DIL_STRIDE1:
            for r1 in range(DIL_STRIDE1):
                og_ref[g, pl.ds(r1, per, stride=DIL_STRIDE1), :] = rg_ref[5, r1]
                lg_ref[g, pl.ds(r1, per, stride=DIL_STRIDE1), :] = rg_ref[6, r1]
    lses = [lg_ref[g] for g in range(ngrp)]
    m = functools.reduce(jnp.maximum, lses)
    ws = [jnp.exp(l - m) for l in lses]
    tot = functools.reduce(lambda a, b: a + b, ws)
    acc = sum(w * og_ref[g] for g, w in enumerate(ws))
    o_ref[...] = (acc / tot).astype(o_ref.dtype)


def _dilated(y, col0, *, batch, seq):
    ngrp, H, n = len(DIL_PAIRS), DIL_HEADS, DIL_BLOCK
    unit = max(d for _, d in DIL_PAIRS) * n
    assert seq % unit == 0
    nu = seq // unit
    cb0 = col0 // LANES
    in_specs, args = [], []
    for g, (_, d) in enumerate(DIL_PAIRS):
        sub = n * d
        per = unit // sub
        for part in range(3):
            cb = cb0 + part * ngrp * H + g * H
            in_specs.append(pl.BlockSpec((unit, LANES), lambda b, u, h, cb=cb: (b * nu + u, cb + h)))
            args.append(y)
        for part in (1, 2):
            cb = cb0 + part * ngrp * H + g * H
            in_specs.append(pl.BlockSpec(
                (sub, LANES),
                lambda b, u, h, cb=cb, per=per: (jnp.maximum(b * nu * per + u * per - 1, 0), cb + h)))
            args.append(y)
    return pl.pallas_call(
        functools.partial(_dil_kernel, unit=unit),
        grid=(batch, nu, H),
        in_specs=in_specs,
        out_specs=pl.BlockSpec((unit, LANES), lambda b, u, h: (b * nu + u, h)),
        out_shape=jax.ShapeDtypeStruct((batch * seq, H * LANES), BF16),
        scratch_shapes=[pltpu.VMEM((ngrp, unit, LANES), F32), pltpu.VMEM((ngrp, unit, LANES), F32),
                        pltpu.VMEM((7, DIL_STRIDE1, unit // DIL_STRIDE1, LANES), F32)],
        compiler_params=_cparams("parallel", "parallel", "arbitrary"),
        name="dilated",
    )(*args)


I_LANE, F_LANE = 0, MLSTM_HEADS


def _mlstm_kernel(qk_ref, v_ref, co_ref, gt_ref, cw_ref, cb_ref, gb_ref, ng_ref, o_ref,
                  c_ref, m_ref, tail_ref, xbuf_ref, *, tile):
    L, H, DK, DV = MLSTM_CHUNK, MLSTM_HEADS, MLSTM_DQK, MLSTM_DV
    KC = MLSTM_CONV
    PAD = 8

    @pl.when(pl.program_id(1) == 0)
    def _():
        c_ref[...] = jnp.zeros_like(c_ref)
        m_ref[...] = jnp.zeros_like(m_ref)
        tail_ref[...] = jnp.zeros_like(tail_ref)

    xbuf_ref[0:PAD, :] = tail_ref[...]
    xbuf_ref[PAD:PAD + tile, :] = qk_ref[...]
    tail_ref[...] = qk_ref[tile - PAD:tile, :]
    acc = cb_ref[...] + cw_ref[KC - 1:KC, :] * xbuf_ref[PAD:PAD + tile, :]
    for j in range(KC - 1):
        off = PAD - (KC - 1) + j
        acc = acc + cw_ref[j:j + 1, :] * xbuf_ref[off:off + tile, :]
    qk = _silu(acc)

    gates = gt_ref[...] + gb_ref[...]
    lane = lax.broadcasted_iota(jnp.int32, gates.shape, 1)
    z = jnp.where(lane >= F_LANE, _log_sigmoid(gates), gates)
    tril = _tril(L)
    tril_b = tril.astype(BF16)
    nchunk = tile // L
    lane_c = lax.broadcasted_iota(jnp.int32, (L, LANES), 1)
    zs, bs, wts = [], [], []
    for c in range(nchunk):
        zc = z[c * L:(c + 1) * L]
        bc = _chunk_cumsum(tril_b, zc)
        zs.append(zc)
        bs.append(bc)
        wts.append(jnp.where(lane_c >= F_LANE, bc, zc).T)
    ones_col = (lane_c == 0).astype(F32)

    hc = [(h, c) for c in range(nchunk) for h in range(H)]
    rsl = lambda c: slice(c * L, (c + 1) * L)
    row_c = lax.broadcasted_iota(jnp.int32, (L, LANES), 0)
    m_run = m_ref[0:1, :]
    kw_all, mt_all, isc_all, floor_all, so_all, sn_all = [], [], [], [], [], []
    for c in range(nchunk):
        bc = bs[c]
        li_a = pltpu.roll(zs[c], F_LANE - I_LANE, 1)
        b_last = bc[L - 1:L]
        a = b_last - bc + li_a
        m_loc = jnp.max(a, axis=0, keepdims=True)
        kw_all.append(jnp.exp(a - m_loc))
        x = li_a - bc
        for s in (1, 2, 4, 8, 16, 32):
            x = jnp.maximum(x, jnp.where(row_c >= s, pltpu.roll(x, s, 0), NEG_INF))
        inter_log = bc + m_run
        m_t = jnp.maximum(inter_log, bc + x)
        mt_all.append(m_t)
        isc_all.append(jnp.exp(inter_log - m_t))
        floor_all.append(jnp.exp(-m_t))
        m_new = jnp.maximum(b_last + m_run, m_loc)
        so_all.append(jnp.exp(b_last + m_run - m_new))
        sn_all.append(jnp.exp(m_loc - m_new))
        m_run = m_new
    m_ref[0:1, :] = m_run
    col = lambda x, h: x[:, F_LANE + h:F_LANE + h + 1]
    qb, kb, vb, qkm = {}, {}, {}, {}
    for h, c in hc:
        qb[h, c] = qk[rsl(c), h * DK:(h + 1) * DK].astype(BF16)
        kb[h, c] = qk[rsl(c), (H + h) * DK:(H + h + 1) * DK] * DK ** -0.5
        vb[h, c] = v_ref[rsl(c), h * DV:(h + 1) * DV]
    raw = {p: _bdot_nt(qb[p], kb[p]) for p in hc}
    for h, c in hc:
        b_row = wts[c][F_LANE + h:F_LANE + h + 1, :]
        li_row = wts[c][I_LANE + h:I_LANE + h + 1, :]
        d_log = jnp.where(tril, col(bs[c], h) - b_row + li_row, NEG_INF)
        qkm[h, c] = raw[h, c] * jnp.exp(d_log - col(mt_all[c], h))
    v_aug = {p: jnp.concatenate([vb[p], ones_col], axis=1).astype(BF16) for p in hc}
    intra = {p: _bdot(qkm[p], v_aug[p]) for p in hc}
    c_loc = {(h, c): _bdot_tn(kb[h, c] * col(kw_all[c], h), v_aug[h, c]) for h, c in hc}
    ct = [c_ref[h] for h in range(H)]
    outs = {}
    for h, c in hc:
        p = (h, c)
        tot = intra[p] + col(isc_all[c], h) * _bdot(qb[p], ct[h])
        outs[p] = tot[:, :DV] / jnp.maximum(jnp.abs(tot[:, DV:DV + 1]), col(floor_all[c], h))
        ct[h] = ct[h] * col(so_all[c], h) + c_loc[p] * col(sn_all[c], h)
    for h in range(H):
        c_ref[h] = ct[h]
        hcat = jnp.concatenate([outs[h, c] for c in range(nchunk)], axis=0)
        mu = jnp.mean(hcat, axis=-1, keepdims=True)
        hcen = hcat - mu
        var = jnp.mean(hcen * hcen, axis=-1, keepdims=True)
        hn = hcen * lax.rsqrt(var + 1e-5) * ng_ref[...]
        o_ref[:, h * DV:(h + 1) * DV] = (_sigmoid(co_ref[:, h * DV:(h + 1) * DV]) * hn).astype(o_ref.dtype)


def _mlstm(y1, y2, gates, conv_w, conv_b, gate_bias, ng, *, batch, seq, tile):
    nt = seq // tile
    H, DK, DV = MLSTM_HEADS, MLSTM_DQK, MLSTM_DV
    qk_w, v_w = 2 * H * DK, H * DV
    row = lambda b, t: b * nt + t
    return pl.pallas_call(
        functools.partial(_mlstm_kernel, tile=tile),
        grid=(batch, nt),
        in_specs=[pl.BlockSpec((tile, qk_w), lambda b, t: (row(b, t), 0)),
                  pl.BlockSpec((tile, v_w), lambda b, t: (row(b, t), 1)),
                  pl.BlockSpec((tile, v_w), lambda b, t: (row(b, t), 0)),
                  pl.BlockSpec((tile, LANES), lambda b, t: (row(b, t), 0)),
                  pl.BlockSpec((MLSTM_CONV, qk_w), lambda b, t: (0, 0)),
                  pl.BlockSpec((1, qk_w), lambda b, t: (0, 0)),
                  pl.BlockSpec((1, LANES), lambda b, t: (0, 0)),
                  pl.BlockSpec((1, DV), lambda b, t: (0, 0))],
        out_specs=pl.BlockSpec((tile, v_w), lambda b, t: (row(b, t), 0)),
        out_shape=jax.ShapeDtypeStruct((batch * seq, v_w), BF16),
        scratch_shapes=[pltpu.VMEM((H, DK, DV + LANES), F32),
                        pltpu.VMEM((8, LANES), F32),
                        pltpu.VMEM((8, qk_w), F32),
                        pltpu.VMEM((tile + 8, qk_w), F32)],
        compiler_params=_cparams("parallel", "arbitrary"),
        name="mlstm",
    )(y1, y1, y2, gates, conv_w, conv_b, gate_bias, ng)


def _rope128(x, c_ref, s1_ref, s2_ref):
    return x * c_ref[...] + pltpu.roll(x, 32, 1) * s1_ref[...] + pltpu.roll(x, LANES - 32, 1) * s2_ref[...]


def _rms(x, g):
    return x * lax.rsqrt(jnp.mean(x * x, axis=-1, keepdims=True) + 1e-6) * g


MLA_SCORE_SCALE = (MLA_NOPE + MLA_ROPE) ** -0.5 * 1.4426950408889634


def _mla_proj_kernel(xq_ref, xkv_ref, qg_ref, kvg_ref, wq_ref, wk_ref, wvt_ref, kr_ref, c_ref, s1_ref, s2_ref,
                     q_ref, k_ref, vt_ref):
    q = _bdot(_rms(xq_ref[...], qg_ref[...]), wq_ref[...]) * MLA_SCORE_SCALE
    xn = _rms(xkv_ref[...], kvg_ref[...]).astype(BF16)
    kn = jnp.dot(xn, wk_ref[...], preferred_element_type=F32)
    vt_ref[...] = lax.dot_general(wvt_ref[...], xn, _NT, preferred_element_type=F32).astype(vt_ref.dtype)
    kr = _rope128(kr_ref[...], c_ref, s1_ref, s2_ref).astype(k_ref.dtype)
    for h in range(MLA_HEADS):
        q_ref[:, 256 * h:256 * h + 128] = q[:, 256 * h:256 * h + 128].astype(q_ref.dtype)
        q_ref[:, 256 * h + 128:256 * h + 256] = _rope128(
            q[:, 256 * h + 128:256 * h + 256], c_ref, s1_ref, s2_ref).astype(q_ref.dtype)
        k_ref[:, 256 * h:256 * h + 128] = kn[:, 128 * h:128 * h + 128].astype(k_ref.dtype)
        k_ref[:, 256 * h + 128:256 * h + 256] = kr


def _mla_proj(y, qcb, gates, qg, kvg, wq, wk, wvt, tabs, *, tm):
    m = y.shape[0]
    H = MLA_HEADS
    row_block = lambda width, col: pl.BlockSpec((tm, width), lambda i: (i, col))
    full = lambda a: pl.BlockSpec(a.shape, lambda i: (0, 0))
    return pl.pallas_call(
        _mla_proj_kernel,
        grid=(m // tm,),
        in_specs=[row_block(MLA_Q_RANK, qcb), row_block(MLA_KV_RANK, qcb + 1), full(qg), full(kvg),
                  full(wq), full(wk), full(wvt)] + [row_block(LANES, 0)] * 4,
        out_specs=[row_block(H * 256, 0), row_block(H * 256, 0),
                   pl.BlockSpec((H * MLA_DV, tm), lambda i: (0, i))],
        out_shape=[jax.ShapeDtypeStruct((m, H * 256), BF16), jax.ShapeDtypeStruct((m, H * 256), BF16),
                   jax.ShapeDtypeStruct((H * MLA_DV, m), BF16)],
        compiler_params=_cparams("parallel"),
        name="mla_proj",
    )(y, y, qg, kvg, wq, wk, wvt, gates, *tabs)


def _mla_attn_kernel(q_ref, k_ref, vt_ref, o_ref, s_ref, mx_ref, *, tq, hg):
    i = pl.program_id(2)
    DV = MLA_DV

    def scores(j, slot):
        start = pl.multiple_of(j * tq, tq)
        for g in range(hg):
            st = lax.dot_general(k_ref[pl.ds(start, tq), 256 * g:256 * (g + 1)],
                                 q_ref[:, 256 * g:256 * (g + 1)], _NT, preferred_element_type=F32)
            s_ref[slot, g] = st
            mx_ref[slot, g] = jnp.max(st, axis=0, keepdims=True)

    def step(j, slot, carry, masked):
        start = pl.multiple_of(j * tq, tq)
        out = []
        for g in range(hg):
            m, l, acc = carry[g]
            vtb = vt_ref[DV * g:DV * (g + 1), pl.ds(start, tq)]
            st = s_ref[slot, g]
            if masked:
                kk = lax.broadcasted_iota(jnp.int32, (tq, tq), 0)
                qq = lax.broadcasted_iota(jnp.int32, (tq, tq), 1)
                st = jnp.where(kk <= qq, st, NEG_INF)
                m_new = jnp.maximum(m, jnp.max(st, axis=0, keepdims=True))
            else:
                m_new = jnp.maximum(m, mx_ref[slot, g])
            pt = jnp.exp2(st - m_new)
            a = jnp.exp2(m - m_new)
            l = a * l + jnp.sum(pt, axis=0, keepdims=True)
            acc = a * acc + jnp.dot(vtb, pt.astype(BF16), preferred_element_type=F32)
            out.append((m_new, l, acc))
        return tuple(out)

    init = tuple((jnp.full((1, tq), NEG_INF, F32), jnp.zeros((1, tq), F32), jnp.zeros((DV, tq), F32))
                 for _ in range(hg))

    def pair(jj, carry):
        scores(2 * jj + 1, 1)
        carry = step(2 * jj, 0, carry, False)
        scores(2 * jj + 2, 0)
        return step(2 * jj + 1, 1, carry, False)

    def finish(carry):
        for g in range(hg):
            _, l, acc = carry[g]
            o_ref[:, DV * g:DV * (g + 1)] = (acc / l).T.astype(o_ref.dtype)

    scores(0, 0)
    carry = lax.fori_loop(0, i // 2, pair, init)

    @pl.when(i % 2 == 0)
    def _():
        finish(step(i, 0, carry, True))

    @pl.when(i % 2 == 1)
    def _():
        scores(i, 1)
        finish(step(i, 1, step(i - 1, 0, carry, False), True))


def _mla_attn(qf, kf, vt, *, batch, seq, tq, hg=4):
    H = MLA_HEADS
    nq = seq // tq
    return pl.pallas_call(
        functools.partial(_mla_attn_kernel, tq=tq, hg=hg),
        grid=(batch, H // hg, nq),
        in_specs=[pl.BlockSpec((tq, 256 * hg), lambda b, h, i: (b * nq + i, h)),
                  pl.BlockSpec((seq, 256 * hg), lambda b, h, i: (b, h)),
                  pl.BlockSpec((MLA_DV * hg, seq), lambda b, h, i: (h, b))],
        out_specs=pl.BlockSpec((tq, MLA_DV * hg), lambda b, h, i: (b * nq + i, h)),
        out_shape=jax.ShapeDtypeStruct((batch * seq, H * MLA_DV), BF16),
        scratch_shapes=[pltpu.VMEM((2, hg, tq, tq), F32), pltpu.VMEM((2, hg, 1, tq), F32)],
        compiler_params=_cparams("parallel", "parallel", "arbitrary"),
        name="mla_attn",
    )(qf, kf, vt)


def _tile_for(n, pref):
    t = pref
    while n % t:
        t //= 2
    return t


def _even_mixer(x_bf, wt_all, w_o_all, j, wg2, bg, ng, *, batch, seq):
    m = x_bf.shape[0]
    gq_gk_gv = 2 * GLA_HEADS * GLA_DK + GLA_HEADS * GLA_DV
    rest0 = gq_gk_gv + GLA_GATE_RANK
    tm = _tile_for(m, 2048)
    mm = functools.partial(_matmul_t, x_bf, wt_all, layer=j, tm=tm, out_dtype=F32)
    y1, gts, wo_bf = mm(row0=0, ncols=gq_gk_gv, tn=512, side_rows=(gq_gk_gv,), cast=(w_o_all, j))
    y2 = mm(row0=rest0, ncols=wt_all.shape[1] - rest0, tn=512)
    o_a = _gla(y1, y2, gts, wg2, bg.reshape(1, -1), ng.reshape(1, -1), batch=batch, seq=seq,
               tile=_tile_for(seq, 512))
    o_b = _dilated(y2, GLA_HEADS * GLA_DV, batch=batch, seq=seq)
    return (o_a, o_b), wo_bf


def _rope_tables(positions):
    inv_freq = ROPE_THETA ** (-jnp.arange(0, MLA_ROPE, 2, dtype=F32) / MLA_ROPE)
    ang = positions.astype(F32).reshape(-1, 1) * inv_freq
    cos, sin = jnp.cos(ang), jnp.sin(ang)
    z32 = jnp.zeros_like(cos)
    z64 = jnp.zeros((cos.shape[0], LANES - MLA_ROPE), F32)
    c = jnp.concatenate([z64, cos, cos], axis=1)
    s1 = jnp.concatenate([z64, z32, sin], axis=1)
    s2 = jnp.concatenate([z64, -sin, z32], axis=1)
    return c, s1, s2


def _odd_mixer(x_bf, tabs, wt_all, w_o_all, j, conv_w, conv_b, bi, bf, ng, qg, kvg, wuq, wukv, *, batch, seq):
    m = x_bf.shape[0]
    H = MLA_HEADS
    cq_ck_cv = 2 * MLSTM_HEADS * MLSTM_DQK + MLSTM_HEADS * MLSTM_DV
    co0 = cq_ck_cv + 2 * MLSTM_HEADS
    kr0 = co0 + MLSTM_HEADS * MLSTM_DV + MLA_Q_RANK + MLA_KV_RANK
    gate_bias = jnp.concatenate([bi, bf, jnp.zeros((LANES - 2 * MLSTM_HEADS,), F32)]).reshape(1, LANES)
    wq3 = wuq.reshape(-1, H, MLA_NOPE + MLA_ROPE)
    wq = jnp.concatenate([wq3[:, :, :MLA_NOPE], jnp.zeros(wq3.shape[:2] + (LANES - MLA_ROPE,), F32),
                          wq3[:, :, MLA_NOPE:]], axis=2).reshape(-1, H * 256).astype(BF16)
    wkv = wukv.reshape(-1, H, MLA_NOPE + MLA_DV)
    wk = wkv[:, :, :MLA_NOPE].reshape(-1, H * MLA_NOPE).astype(BF16)
    wvt = wkv[:, :, MLA_NOPE:].reshape(-1, H * MLA_DV).T.astype(BF16)
    tm = _tile_for(m, 2048)
    mm = functools.partial(_matmul_t, x_bf, wt_all, layer=j, tm=tm, out_dtype=F32)
    y1, gts, kr, wo_bf = mm(row0=0, ncols=cq_ck_cv, tn=512, side_rows=(cq_ck_cv, kr0 + MLA_ROPE - LANES),
                            cast=(w_o_all, j))
    y2 = mm(row0=co0, ncols=kr0 - co0, tn=512)
    o_c = _mlstm(y1, y2, gts, conv_w, conv_b.reshape(1, -1), gate_bias, ng.reshape(1, -1),
                 batch=batch, seq=seq, tile=_tile_for(seq, 512))
    qf, kf, vt = _mla_proj(y2, MLSTM_HEADS * MLSTM_DV // MLA_Q_RANK, kr, qg.reshape(1, -1), kvg.reshape(1, -1),
                           wq, wk, wvt, tabs, tm=_tile_for(m, 512))
    o_d = _mla_attn(qf, kf, vt, batch=batch, seq=seq, tq=_tile_for(seq, 512))
    return (o_c, o_d), wo_bf


def kernel(x, positions, even_w_in, even_gla_wg2, even_gla_bg, even_gla_norm_g, even_w_o, odd_w_in, odd_conv_w, odd_conv_b, odd_mlstm_bi, odd_mlstm_bf, odd_mlstm_norm_g, odd_mla_qnorm_g, odd_mla_kvnorm_g, odd_mla_wuq, odd_mla_wukv, odd_w_o, ln1_g, ln1_b, ffn_wgu, ffn_wd, ln2_g, ln2_b):
    batch, seq, d = x.shape
    m = batch * seq
    xf = x.reshape(m, d)
    xb = xf.astype(BF16)
    tabs = _rope_tables(positions)
    even_wt = jnp.swapaxes(even_w_in, 1, 2)
    odd_wt = jnp.swapaxes(odd_w_in, 1, 2)
    depth = ln1_g.shape[0]
    tm_ln = _tile_for(m, 512)
    for l in range(depth):
        j = l // 2
        if l % 2 == 0:
            parts, wo_bf = _even_mixer(xb, even_wt, even_w_o, j, even_gla_wg2[j], even_gla_bg[j],
                                       even_gla_norm_g[j], batch=batch, seq=seq)
        else:
            parts, wo_bf = _odd_mixer(xb, tabs, odd_wt, odd_w_o, j, odd_conv_w[j], odd_conv_b[j],
                                      odd_mlstm_bi[j], odd_mlstm_bf[j], odd_mlstm_norm_g[j], odd_mla_qnorm_g[j],
                                      odd_mla_kvnorm_g[j], odd_mla_wuq[j], odd_mla_wukv[j], batch=batch, seq=seq)
        xf, xb = _mm_ln(parts, wo_bf, xf, ln1_g[l].reshape(1, -1), ln1_b[l].reshape(1, -1), tm=tm_ln)
        hid, wd_bf = _ffn_up(xb, ffn_wgu, ffn_wd, tm=_tile_for(m, 2048), tn=512, layer=l)
        xf, xb = _mm_ln((hid,), wd_bf, xf, ln2_g[l].reshape(1, -1), ln2_b[l].reshape(1, -1),
                        tm=_tile_for(m, 256))
    return xf.reshape(batch, seq, d)
```

```python
import functools

import jax
import jax.numpy as jnp
from jax import lax
from jax.experimental import pallas as pl
from jax.experimental.pallas import tpu as pltpu

F32 = jnp.float32
BF16 = jnp.bfloat16

D_MODEL = 2048
DEPTH = 4
GLA_HEADS, GLA_DK, GLA_DV, GLA_GATE_RANK, GLA_TAU, GLA_CHUNK = 4, 128, 256, 16, 16.0, 64
DIL_PAIRS = ((128, 1), (512, 4), (2048, 16))
DIL_HEADS, DIL_HEAD_DIM, DIL_BLOCK = 4, 128, 128
MLSTM_HEADS, MLSTM_DQK, MLSTM_DV, MLSTM_CONV, MLSTM_CHUNK = 4, 128, 256, 4, 64
MLA_HEADS, MLA_Q_RANK, MLA_KV_RANK, MLA_NOPE, MLA_ROPE, MLA_DV = 8, 512, 512, 128, 64, 128
ROPE_THETA = 10000.0
FFN_HIDDEN = 5632
DEEPNORM_ALPHA = (2.0 * DEPTH) ** 0.25

LANES = 128
VMEM_LIMIT = 56 * 1024 * 1024
NEG_INF = float("-inf")

_NT = (((1,), (1,)), ((), ()))


def _cparams(*sem):
    return pltpu.CompilerParams(dimension_semantics=sem, vmem_limit_bytes=VMEM_LIMIT)


def _bdot(a, b):
    return jnp.dot(a.astype(BF16), b.astype(BF16), preferred_element_type=F32)


def _bdot_nt(a, b):
    return lax.dot_general(a.astype(BF16), b.astype(BF16), _NT, preferred_element_type=F32)


def _bdot_tn(a, b):
    return jnp.dot(a.astype(F32).T.astype(BF16), b.astype(BF16), preferred_element_type=F32)


def _split2(a):
    hi = a.astype(BF16)
    lo = (a - hi.astype(F32)).astype(BF16)
    return hi, lo


def _dot3(a, b):
    ah, al = _split2(a)
    bh, bl = _split2(b)
    f = functools.partial(jnp.dot, preferred_element_type=F32)
    return f(ah, bh) + (f(ah, bl) + f(al, bh))


def _chunk_cumsum(tril_b, x):
    hi, lo = _split2(x)
    lo2 = (x - hi.astype(F32) - lo.astype(F32)).astype(BF16)
    f = functools.partial(jnp.dot, preferred_element_type=F32)
    return f(tril_b, hi) + (f(tril_b, lo) + f(tril_b, lo2))


def _tril(n):
    r = lax.broadcasted_iota(jnp.int32, (n, n), 0)
    c = lax.broadcasted_iota(jnp.int32, (n, n), 1)
    return c <= r


def _log_sigmoid(x):
    return jnp.minimum(x, 0.0) - jnp.log1p(jnp.exp(-jnp.abs(x)))


def _sigmoid(x):
    return 1.0 / (1.0 + jnp.exp(-x))


def _silu(x):
    return x * _sigmoid(x)


def _wspec(w, layer, rows, cols, index_map):
    if w.ndim == 2:
        return pl.BlockSpec((rows, cols), index_map)
    return pl.BlockSpec((None, rows, cols), lambda *ids: (layer,) + tuple(index_map(*ids)))


def _mm_nt_kernel(*refs, nside, ncast):
    a_ref, wt_ref, side_w, cast_in = refs[0], refs[1], refs[2:2 + nside], refs[2 + nside:2 + nside + ncast]
    outs = refs[2 + nside + ncast:]
    o_ref, side_o, cast_out = outs[0], outs[1:1 + nside], outs[1 + nside:]

    o_ref[...] = lax.dot_general(a_ref[...], wt_ref[0].astype(BF16), _NT,
                                 preferred_element_type=F32).astype(o_ref.dtype)
    for c_in, c_out in zip(cast_in, cast_out):
        c_out[...] = c_in[...].astype(c_out.dtype)

    if not side_w:
        return

    @pl.when(pl.program_id(1) == 0)
    def _():
        w = jnp.concatenate([w_ref[0] for w_ref in side_w], axis=0).astype(BF16)
        prod = lax.dot_general(a_ref[...], w, _NT, preferred_element_type=F32)
        for n, s_ref in enumerate(side_o):
            s_ref[...] = prod[:, n * LANES:(n + 1) * LANES].astype(s_ref.dtype)


def _matmul_t(a, wt, *, layer, row0, ncols, tm, tn, out_dtype, side_rows=(), cast=None):
    m, k = a.shape
    nj = ncols // tn
    wspec = lambda rows, imap: pl.BlockSpec((pl.Element(1), pl.Element(rows), pl.Element(k)), imap)
    in_specs = ([pl.BlockSpec((tm, k), lambda i, j: (i, 0)),
                 wspec(tn, lambda i, j: (layer, pl.multiple_of(row0 + j * tn, 8), 0))]
                + [wspec(LANES, lambda i, j, r=r: (layer, r, 0)) for r in side_rows])
    out_specs = ([pl.BlockSpec((tm, tn), lambda i, j: (i, j))]
                 + [pl.BlockSpec((tm, LANES), lambda i, j: (i, 0)) for _ in side_rows])
    out_shape = ([jax.ShapeDtypeStruct((m, ncols), out_dtype)]
                 + [jax.ShapeDtypeStruct((m, LANES), out_dtype) for _ in side_rows])
    args = [a, wt] + [wt] * len(side_rows)
    if cast is not None:
        cw, cl = cast
        rows, cols = cw.shape[1:]
        slab = rows // ((m // tm) * nj)
        assert slab * (m // tm) * nj == rows and slab % 16 == 0
        in_specs.append(pl.BlockSpec((None, slab, cols), lambda i, j: (cl, i * nj + j, 0)))
        out_specs.append(pl.BlockSpec((slab, cols), lambda i, j: (i * nj + j, 0)))
        out_shape.append(jax.ShapeDtypeStruct((rows, cols), BF16))
        args.append(cw)
    outs = pl.pallas_call(
        functools.partial(_mm_nt_kernel, nside=len(side_rows), ncast=int(cast is not None)),
        grid=(m // tm, nj),
        in_specs=in_specs, out_specs=out_specs, out_shape=out_shape,
        compiler_params=_cparams("parallel", "arbitrary"),
        name="mm_nt",
    )(*args)
    return outs[0] if len(outs) == 1 else tuple(outs)


FFN_COL_GROUP = 256


def _ffn_up_kernel(x_ref, wg_ref, wu_ref, wd_ref, o_ref, wdb_ref):
    x = x_ref[...]
    tn = o_ref.shape[1]
    for c0 in range(0, tn, FFN_COL_GROUP):
        cs = slice(c0, c0 + FFN_COL_GROUP)
        g = jnp.dot(x, wg_ref[:, cs].astype(BF16), preferred_element_type=F32)
        u = jnp.dot(x, wu_ref[:, cs].astype(BF16), preferred_element_type=F32)
        o_ref[:, cs] = (_silu(g) * u).astype(o_ref.dtype)
    wdb_ref[...] = wd_ref[...].astype(wdb_ref.dtype)


def _ffn_up(x_bf, w_gu, w_d, *, tm, tn, layer):
    m, k = x_bf.shape
    hidden = w_gu.shape[-1] // 2
    nb = hidden // tn
    steps = (m // tm) * nb
    slab = hidden // steps
    assert slab * steps == hidden and slab % 16 == 0
    return pl.pallas_call(
        _ffn_up_kernel,
        grid=(m // tm, nb),
        in_specs=[pl.BlockSpec((tm, k), lambda i, j: (i, 0)),
                  _wspec(w_gu, layer, k, tn, lambda i, j: (0, j)),
                  _wspec(w_gu, layer, k, tn, lambda i, j: (0, j + nb)),
                  _wspec(w_d, layer, slab, w_d.shape[-1], lambda i, j: (i * nb + j, 0))],
        out_specs=[pl.BlockSpec((tm, tn), lambda i, j: (i, j)),
                   pl.BlockSpec((slab, w_d.shape[-1]), lambda i, j: (i * nb + j, 0))],
        out_shape=[jax.ShapeDtypeStruct((m, hidden), BF16),
                   jax.ShapeDtypeStruct((hidden, w_d.shape[-1]), BF16)],
        compiler_params=_cparams("parallel", "arbitrary"),
        name="ffn_up",
    )(x_bf, w_gu, w_gu, w_d)


def _mm_ln_kernel(*refs):
    a_refs, (w_ref, res_ref, g_ref, b_ref, of_ref, ob_ref) = refs[:-6], refs[-6:]
    a = a_refs[0][...] if len(a_refs) == 1 else jnp.concatenate([r[...] for r in a_refs], axis=1)
    y = DEEPNORM_ALPHA * res_ref[...] + jnp.dot(a, w_ref[...], preferred_element_type=F32)
    mu = jnp.mean(y, axis=-1, keepdims=True)
    yc = y - mu
    var = jnp.mean(yc * yc, axis=-1, keepdims=True)
    out = yc * lax.rsqrt(var + 1e-5) * g_ref[...] + b_ref[...]
    of_ref[...] = out
    ob_ref[...] = out.astype(BF16)


def _mm_ln(parts, w, res, g, b, *, tm):
    m = res.shape[0]
    kdim, n = w.shape
    widths = tuple(p.shape[1] for p in parts)
    assert sum(widths) == kdim
    return pl.pallas_call(
        _mm_ln_kernel,
        grid=(m // tm,),
        in_specs=[pl.BlockSpec((tm, wd), lambda i: (i, 0)) for wd in widths] + [
            pl.BlockSpec((kdim, n), lambda i: (0, 0), pipeline_mode=pl.Buffered(1)),
            pl.BlockSpec((tm, n), lambda i: (i, 0)),
            pl.BlockSpec((1, n), lambda i: (0, 0)),
            pl.BlockSpec((1, n), lambda i: (0, 0))],
        out_specs=[pl.BlockSpec((tm, n), lambda i: (i, 0)),
                   pl.BlockSpec((tm, n), lambda i: (i, 0))],
        out_shape=[jax.ShapeDtypeStruct((m, n), F32), jax.ShapeDtypeStruct((m, n), BF16)],
        compiler_params=_cparams("parallel"),
        name="mm_ln",
    )(*parts, w, res, g, b)


def _gla_kernel(q_ref, k_ref, v_ref, r_ref, g_ref, wg2_ref, bg_ref, ng_ref, o_ref, st_ref, *, tile):
    L, H, DK, DV = GLA_CHUNK, GLA_HEADS, GLA_DK, GLA_DV

    @pl.when(pl.program_id(1) == 0)
    def _():
        st_ref[...] = jnp.zeros_like(st_ref)

    tril = _tril(L)
    tril_b = tril.astype(BF16)
    g16 = g_ref[:, :GLA_GATE_RANK]
    nchunk = tile // L
    hc = [(h, c) for c in range(nchunk) for h in range(H)]
    ksl = lambda h: slice(h * DK, (h + 1) * DK)
    vsl = lambda h: slice(h * DV, (h + 1) * DV)
    rsl = lambda c: slice(c * L, (c + 1) * L)
    log_a = [_log_sigmoid(_dot3(g16, wg2_ref[:, ksl(h)]) + bg_ref[:, ksl(h)]) / GLA_TAU for h in range(H)]
    b = {(h, c): _chunk_cumsum(tril_b, log_a[h][rsl(c)]) for h, c in hc}
    q_dec, k_inv, k_end, decay, vb = {}, {}, {}, {}, {}
    for h, c in hc:
        bb = b[h, c]
        b_last = bb[L - 1:L]
        kk = k_ref[rsl(c), ksl(h)]
        q_dec[h, c] = (q_ref[rsl(c), ksl(h)] * DK ** -0.5 * jnp.exp(bb)).astype(BF16)
        k_inv[h, c] = (kk * jnp.exp(-bb)).astype(BF16)
        k_end[h, c] = (kk * jnp.exp(b_last - bb)).astype(BF16)
        decay[h, c] = jnp.exp(b_last)
        vb[h, c] = v_ref[rsl(c), vsl(h)]
    scores = {p: _bdot_nt(q_dec[p], k_inv[p]) for p in hc}
    scores = {p: jnp.where(tril, scores[p], 0.0).astype(BF16) for p in hc}
    intra = {p: _bdot(scores[p], vb[p]) for p in hc}
    s_loc = {p: _bdot_tn(vb[p], k_end[p]) for p in hc}
    st = [st_ref[h] for h in range(H)]
    outs = {}
    for h, c in hc:
        outs[h, c] = intra[h, c] + _bdot_nt(q_dec[h, c], st[h])
        st[h] = st[h] * decay[h, c] + s_loc[h, c]
    for h in range(H):
        st_ref[h] = st[h]
        o = jnp.concatenate([outs[h, c] for c in range(nchunk)], axis=0)
        o = o * lax.rsqrt(jnp.mean(o * o, axis=-1, keepdims=True) + 1e-6) * ng_ref[...]
        o_ref[:, vsl(h)] = (o * _silu(r_ref[:, vsl(h)])).astype(o_ref.dtype)


def _gla(y1, y2, gates, wg2, bg, ng, *, batch, seq, tile):
    nt = seq // tile
    qk_w, v_w = GLA_HEADS * GLA_DK, GLA_HEADS * GLA_DV
    row = lambda b, t: b * nt + t
    return pl.pallas_call(
        functools.partial(_gla_kernel, tile=tile),
        grid=(batch, nt),
        in_specs=[pl.BlockSpec((tile, qk_w), lambda b, t: (row(b, t), 0)),
                  pl.BlockSpec((tile, qk_w), lambda b, t: (row(b, t), 1)),
                  pl.BlockSpec((tile, v_w), lambda b, t: (row(b, t), 1)),
                  pl.BlockSpec((tile, v_w), lambda b, t: (row(b, t), 0)),
                  pl.BlockSpec((tile, LANES), lambda b, t: (row(b, t), 0)),
                  pl.BlockSpec((GLA_GATE_RANK, qk_w), lambda b, t: (0, 0)),
                  pl.BlockSpec((1, qk_w), lambda b, t: (0, 0)),
                  pl.BlockSpec((1, GLA_DV), lambda b, t: (0, 0))],
        out_specs=pl.BlockSpec((tile, v_w), lambda b, t: (row(b, t), 0)),
        out_shape=jax.ShapeDtypeStruct((batch * seq, v_w), BF16),
        scratch_shapes=[pltpu.VMEM((GLA_HEADS, GLA_DV, GLA_DK), F32)],
        compiler_params=_cparams("parallel", "arbitrary"),
        name="gla",
    )(y1, y1, y1, y2, gates, wg2, bg, ng)


LOG2E = 1.4426950408889634
LN2 = 0.6931471805599453
DIL_STRIDE1 = 4


def _dil_kernel(*refs, unit):
    ngrp = len(DIL_PAIRS)
    in_refs, (o_ref, og_ref, lg_ref, rg_ref) = refs[:5 * ngrp], refs[5 * ngrp:]
    u = pl.program_id(1)
    n = DIL_BLOCK
    qscale = DIL_HEAD_DIM ** -0.5 * LOG2E
    row = lax.broadcasted_iota(jnp.int32, (n, n), 0)
    col = lax.broadcasted_iota(jnp.int32, (n, n), 1)
    prev_band, cur_band = col >= row, col <= row
    ones = jnp.ones((n, LANES), BF16)
    for g, (_, d) in enumerate(DIL_PAIRS):
        q_ref, k_ref, v_ref, kp_ref, vp_ref = in_refs[5 * g:5 * g + 5]
        sub = n * d
        blocks = []
        if d > DIL_STRIDE1:
            d2 = d // DIL_STRIDE1
            assert unit == sub and d2 <= DIL_STRIDE1
            per = unit // DIL_STRIDE1
            for a, src in enumerate((q_ref, k_ref, v_ref, kp_ref, vp_ref)):
                for r1 in range(DIL_STRIDE1):
                    rg_ref[a, r1] = src[pl.ds(r1, per, stride=DIL_STRIDE1), :]
            for r in range(d):
                at = (r % DIL_STRIDE1, pl.ds(r // DIL_STRIDE1, n, stride=d2))
                blocks.append(tuple((rg_ref, (a,) + at) for a in range(5))
                              + (prev_band & (u > 0), ((rg_ref, (5,) + at), (rg_ref, (6,) + at))))
        else:
            for s in range(unit // sub):
                for r in range(d):
                    cur = pl.ds(s * sub + r, n, stride=d) if d > 1 else pl.ds(s * sub, n)
                    outs = ((og_ref, (g, cur)), (lg_ref, (g, cur)))
                    if s == 0:
                        prv = pl.ds(r, n, stride=d) if d > 1 else pl.ds(0, n)
                        blocks.append(((q_ref, (cur,)), (k_ref, (cur,)), (v_ref, (cur,)), (kp_ref, (prv,)),
                                       (vp_ref, (prv,)), prev_band & (u > 0), outs))
                    else:
                        prv = pl.ds((s - 1) * sub + r, n, stride=d) if d > 1 else pl.ds((s - 1) * sub, n)
                        blocks.append(((q_ref, (cur,)), (k_ref, (cur,)), (v_ref, (cur,)), (k_ref, (prv,)),
                                       (v_ref, (prv,)), prev_band, outs))
        get = lambda ra: ra[0][ra[1] + (slice(None),)]
        scores = []
        for qa, kc, _, kp, _, pmask, _ in blocks:
            q = (get(qa) * qscale).astype(BF16)
            scores.append((jnp.where(pmask, _bdot_nt(q, get(kp)), NEG_INF),
                           jnp.where(cur_band, _bdot_nt(q, get(kc)), NEG_INF)))
        probs = []
        for s_p, s_c in scores:
            m = jnp.maximum(jnp.max(s_p, axis=-1, keepdims=True), jnp.max(s_c, axis=-1, keepdims=True))
            probs.append((jnp.exp2(s_p - m).astype(BF16), jnp.exp2(s_c - m).astype(BF16), m))
        for (_, _, vc, _, vp, _, ((o_dst, o_at), (l_dst, l_at))), (p_p, p_c, m) in zip(blocks, probs):
            den = _bdot(p_p, ones) + _bdot(p_c, ones)
            o_dst[o_at + (slice(None),)] = (_bdot(p_p, get(vp)) + _bdot(p_c, get(vc))) / den
            l_dst[l_at + (slice(None),)] = m * LN2 + jnp.log(den)
        if d > DIL_STRIDE1:
            for r1 in range(DIL_STRIDE1):
                og_ref[g, pl.ds(r1, per, stride=DIL_STRIDE1), :] = rg_ref[5, r1]
                lg_ref[g, pl.ds(r1, per, stride=DIL_STRIDE1), :] = rg_ref[6, r1]
    lses = [lg_ref[g] for g in range(ngrp)]
    m = functools.reduce(jnp.maximum, lses)
    ws = [jnp.exp(l - m) for l in lses]
    tot = functools.reduce(lambda a, b: a + b, ws)
    acc = sum(w * og_ref[g] for g, w in enumerate(ws))
    o_ref[...] = (acc / tot).astype(o_ref.dtype)


def _dilated(y, col0, *, batch, seq):
    ngrp, H, n = len(DIL_PAIRS), DIL_HEADS, DIL_BLOCK
    unit = max(d for _, d in DIL_PAIRS) * n
    assert seq % unit == 0
    nu = seq // unit
    cb0 = col0 // LANES
    in_specs, args = [], []
    for g, (_, d) in enumerate(DIL_PAIRS):
        sub = n * d
        per = unit // sub
        for part in range(3):
            cb = cb0 + part * ngrp * H + g * H
            in_specs.append(pl.BlockSpec((unit, LANES), lambda b, u, h, cb=cb: (b * nu + u, cb + h)))
            args.append(y)
        for part in (1, 2):
            cb = cb0 + part * ngrp * H + g * H
            in_specs.append(pl.BlockSpec(
                (sub, LANES),
                lambda b, u, h, cb=cb, per=per: (jnp.maximum(b * nu * per + u * per - 1, 0), cb + h)))
            args.append(y)
    return pl.pallas_call(
        functools.partial(_dil_kernel, unit=unit),
        grid=(batch, nu, H),
        in_specs=in_specs,
        out_specs=pl.BlockSpec((unit, LANES), lambda b, u, h: (b * nu + u, h)),
        out_shape=jax.ShapeDtypeStruct((batch * seq, H * LANES), BF16),
        scratch_shapes=[pltpu.VMEM((ngrp, unit, LANES), F32), pltpu.VMEM((ngrp, unit, LANES), F32),
                        pltpu.VMEM((7, DIL_STRIDE1, unit // DIL_STRIDE1, LANES), F32)],
        compiler_params=_cparams("parallel", "parallel", "arbitrary"),
        name="dilated",
    )(*args)


I_LANE, F_LANE = 0, MLSTM_HEADS


def _mlstm_kernel(qk_ref, v_ref, co_ref, gt_ref, cw_ref, cb_ref, gb_ref, ng_ref, o_ref,
                  c_ref, m_ref, tail_ref, xbuf_ref, *, tile):
    L, H, DK, DV = MLSTM_CHUNK, MLSTM_HEADS, MLSTM_DQK, MLSTM_DV
    KC = MLSTM_CONV
    PAD = 8

    @pl.when(pl.program_id(1) == 0)
    def _():
        c_ref[...] = jnp.zeros_like(c_ref)
        m_ref[...] = jnp.zeros_like(m_ref)
        tail_ref[...] = jnp.zeros_like(tail_ref)

    xbuf_ref[0:PAD, :] = tail_ref[...]
    xbuf_ref[PAD:PAD + tile, :] = qk_ref[...]
    tail_ref[...] = qk_ref[tile - PAD:tile, :]
    acc = cb_ref[...] + cw_ref[KC - 1:KC, :] * xbuf_ref[PAD:PAD + tile, :]
    for j in range(KC - 1):
        off = PAD - (KC - 1) + j
        acc = acc + cw_ref[j:j + 1, :] * xbuf_ref[off:off + tile, :]
    qk = _silu(acc)

    gates = gt_ref[...] + gb_ref[...]
    lane = lax.broadcasted_iota(jnp.int32, gates.shape, 1)
    z = jnp.where(lane >= F_LANE, _log_sigmoid(gates), gates)
    tril = _tril(L)
    tril_b = tril.astype(BF16)
    nchunk = tile // L
    lane_c = lax.broadcasted_iota(jnp.int32, (L, LANES), 1)
    zs, bs, wts = [], [], []
    for c in range(nchunk):
        zc = z[c * L:(c + 1) * L]
        bc = _chunk_cumsum(tril_b, zc)
        zs.append(zc)
        bs.append(bc)
        wts.append(jnp.where(lane_c >= F_LANE, bc, zc).T)
    ones_col = (lane_c == 0).astype(F32)

    hc = [(h, c) for c in range(nchunk) for h in range(H)]
    rsl = lambda c: slice(c * L, (c + 1) * L)
    row_c = lax.broadcasted_iota(jnp.int32, (L, LANES), 0)
    m_run = m_ref[0:1, :]
    kw_all, mt_all, isc_all, floor_all, so_all, sn_all = [], [], [], [], [], []
    for c in range(nchunk):
        bc = bs[c]
        li_a = pltpu.roll(zs[c], F_LANE - I_LANE, 1)
        b_last = bc[L - 1:L]
        a = b_last - bc + li_a
        m_loc = jnp.max(a, axis=0, keepdims=True)
        kw_all.append(jnp.exp(a - m_loc))
        x = li_a - bc
        for s in (1, 2, 4, 8, 16, 32):
            x = jnp.maximum(x, jnp.where(row_c >= s, pltpu.roll(x, s, 0), NEG_INF))
        inter_log = bc + m_run
        m_t = jnp.maximum(inter_log, bc + x)
        mt_all.append(m_t)
        isc_all.append(jnp.exp(inter_log - m_t))
        floor_all.append(jnp.exp(-m_t))
        m_new = jnp.maximum(b_last + m_run, m_loc)
        so_all.append(jnp.exp(b_last + m_run - m_new))
        sn_all.append(jnp.exp(m_loc - m_new))
        m_run = m_new
    m_ref[0:1, :] = m_run
    col = lambda x, h: x[:, F_LANE + h:F_LANE + h + 1]
    qb, kb, vb, qkm = {}, {}, {}, {}
    for h, c in hc:
        qb[h, c] = qk[rsl(c), h * DK:(h + 1) * DK].astype(BF16)
        kb[h, c] = qk[rsl(c), (H + h) * DK:(H + h + 1) * DK] * DK ** -0.5
        vb[h, c] = v_ref[rsl(c), h * DV:(h + 1) * DV]
    raw = {p: _bdot_nt(qb[p], kb[p]) for p in hc}
    for h, c in hc:
        b_row = wts[c][F_LANE + h:F_LANE + h + 1, :]
        li_row = wts[c][I_LANE + h:I_LANE + h + 1, :]
        d_log = jnp.where(tril, col(bs[c], h) - b_row + li_row, NEG_INF)
        qkm[h, c] = raw[h, c] * jnp.exp(d_log - col(mt_all[c], h))
    v_aug = {p: jnp.concatenate([vb[p], ones_col], axis=1).astype(BF16) for p in hc}
    intra = {p: _bdot(qkm[p], v_aug[p]) for p in hc}
    c_loc = {(h, c): _bdot_tn(kb[h, c] * col(kw_all[c], h), v_aug[h, c]) for h, c in hc}
    ct = [c_ref[h] for h in range(H)]
    outs = {}
    for h, c in hc:
        p = (h, c)
        tot = intra[p] + col(isc_all[c], h) * _bdot(qb[p], ct[h])
        outs[p] = tot[:, :DV] / jnp.maximum(jnp.abs(tot[:, DV:DV + 1]), col(floor_all[c], h))
        ct[h] = ct[h] * col(so_all[c], h) + c_loc[p] * col(sn_all[c], h)
    for h in range(H):
        c_ref[h] = ct[h]
        hcat = jnp.concatenate([outs[h, c] for c in range(nchunk)], axis=0)
        mu = jnp.mean(hcat, axis=-1, keepdims=True)
        hcen = hcat - mu
        var = jnp.mean(hcen * hcen, axis=-1, keepdims=True)
        hn = hcen * lax.rsqrt(var + 1e-5) * ng_ref[...]
        o_ref[:, h * DV:(h + 1) * DV] = (_sigmoid(co_ref[:, h * DV:(h + 1) * DV]) * hn).astype(o_ref.dtype)


def _mlstm(y1, y2, gates, conv_w, conv_b, gate_bias, ng, *, batch, seq, tile):
    nt = seq // tile
    H, DK, DV = MLSTM_HEADS, MLSTM_DQK, MLSTM_DV
    qk_w, v_w = 2 * H * DK, H * DV
    row = lambda b, t: b * nt + t
    return pl.pallas_call(
        functools.partial(_mlstm_kernel, tile=tile),
        grid=(batch, nt),
        in_specs=[pl.BlockSpec((tile, qk_w), lambda b, t: (row(b, t), 0)),
                  pl.BlockSpec((tile, v_w), lambda b, t: (row(b, t), 1)),
                  pl.BlockSpec((tile, v_w), lambda b, t: (row(b, t), 0)),
                  pl.BlockSpec((tile, LANES), lambda b, t: (row(b, t), 0)),
                  pl.BlockSpec((MLSTM_CONV, qk_w), lambda b, t: (0, 0)),
                  pl.BlockSpec((1, qk_w), lambda b, t: (0, 0)),
                  pl.BlockSpec((1, LANES), lambda b, t: (0, 0)),
                  pl.BlockSpec((1, DV), lambda b, t: (0, 0))],
        out_specs=pl.BlockSpec((tile, v_w), lambda b, t: (row(b, t), 0)),
        out_shape=jax.ShapeDtypeStruct((batch * seq, v_w), BF16),
        scratch_shapes=[pltpu.VMEM((H, DK, DV + LANES), F32),
                        pltpu.VMEM((8, LANES), F32),
                        pltpu.VMEM((8, qk_w), F32),
                        pltpu.VMEM((tile + 8, qk_w), F32)],
        compiler_params=_cparams("parallel", "arbitrary"),
        name="mlstm",
    )(y1, y1, y2, gates, conv_w, conv_b, gate_bias, ng)


def _rope128(x, c_ref, s1_ref, s2_ref):
    return x * c_ref[...] + pltpu.roll(x, 32, 1) * s1_ref[...] + pltpu.roll(x, LANES - 32, 1) * s2_ref[...]


def _rms(x, g):
    return x * lax.rsqrt(jnp.mean(x * x, axis=-1, keepdims=True) + 1e-6) * g


MLA_SCORE_SCALE = (MLA_NOPE + MLA_ROPE) ** -0.5 * 1.4426950408889634


def _mla_proj_kernel(xq_ref, xkv_ref, qg_ref, kvg_ref, wq_ref, wk_ref, wvt_ref, kr_ref, c_ref, s1_ref, s2_ref,
                     q_ref, k_ref, vt_ref):
    q = _bdot(_rms(xq_ref[...], qg_ref[...]), wq_ref[...]) * MLA_SCORE_SCALE
    xn = _rms(xkv_ref[...], kvg_ref[...]).astype(BF16)
    kn = jnp.dot(xn, wk_ref[...], preferred_element_type=F32)
    vt_ref[...] = lax.dot_general(wvt_ref[...], xn, _NT, preferred_element_type=F32).astype(vt_ref.dtype)
    kr = _rope128(kr_ref[...], c_ref, s1_ref, s2_ref).astype(k_ref.dtype)
    for h in range(MLA_HEADS):
        q_ref[:, 256 * h:256 * h + 128] = q[:, 256 * h:256 * h + 128].astype(q_ref.dtype)
        q_ref[:, 256 * h + 128:256 * h + 256] = _rope128(
            q[:, 256 * h + 128:256 * h + 256], c_ref, s1_ref, s2_ref).astype(q_ref.dtype)
        k_ref[:, 256 * h:256 * h + 128] = kn[:, 128 * h:128 * h + 128].astype(k_ref.dtype)
        k_ref[:, 256 * h + 128:256 * h + 256] = kr


def _mla_proj(y, qcb, gates, qg, kvg, wq, wk, wvt, tabs, *, tm):
    m = y.shape[0]
    H = MLA_HEADS
    row_block = lambda width, col: pl.BlockSpec((tm, width), lambda i: (i, col))
    full = lambda a: pl.BlockSpec(a.shape, lambda i: (0, 0))
    return pl.pallas_call(
        _mla_proj_kernel,
        grid=(m // tm,),
        in_specs=[row_block(MLA_Q_RANK, qcb), row_block(MLA_KV_RANK, qcb + 1), full(qg), full(kvg),
                  full(wq), full(wk), full(wvt)] + [row_block(LANES, 0)] * 4,
        out_specs=[row_block(H * 256, 0), row_block(H * 256, 0),
                   pl.BlockSpec((H * MLA_DV, tm), lambda i: (0, i))],
        out_shape=[jax.ShapeDtypeStruct((m, H * 256), BF16), jax.ShapeDtypeStruct((m, H * 256), BF16),
                   jax.ShapeDtypeStruct((H * MLA_DV, m), BF16)],
        compiler_params=_cparams("parallel"),
        name="mla_proj",
    )(y, y, qg, kvg, wq, wk, wvt, gates, *tabs)


def _mla_attn_kernel(q_ref, k_ref, vt_ref, o_ref, s_ref, mx_ref, *, tq, hg):
    i = pl.program_id(2)
    DV = MLA_DV

    def scores(j, slot):
        start = pl.multiple_of(j * tq, tq)
        for g in range(hg):
            st = lax.dot_general(k_ref[pl.ds(start, tq), 256 * g:256 * (g + 1)],
                                 q_ref[:, 256 * g:256 * (g + 1)], _NT, preferred_element_type=F32)
            s_ref[slot, g] = st
            mx_ref[slot, g] = jnp.max(st, axis=0, keepdims=True)

    def step(j, slot, carry, masked):
        start = pl.multiple_of(j * tq, tq)
        out = []
        for g in range(hg):
            m, l, acc = carry[g]
            vtb = vt_ref[DV * g:DV * (g + 1), pl.ds(start, tq)]
            st = s_ref[slot, g]
            if masked:
                kk = lax.broadcasted_iota(jnp.int32, (tq, tq), 0)
                qq = lax.broadcasted_iota(jnp.int32, (tq, tq), 1)
                st = jnp.where(kk <= qq, st, NEG_INF)
                m_new = jnp.maximum(m, jnp.max(st, axis=0, keepdims=True))
            else:
                m_new = jnp.maximum(m, mx_ref[slot, g])
            pt = jnp.exp2(st - m_new)
            a = jnp.exp2(m - m_new)
            l = a * l + jnp.sum(pt, axis=0, keepdims=True)
            acc = a * acc + jnp.dot(vtb, pt.astype(BF16), preferred_element_type=F32)
            out.append((m_new, l, acc))
        return tuple(out)

    init = tuple((jnp.full((1, tq), NEG_INF, F32), jnp.zeros((1, tq), F32), jnp.zeros((DV, tq), F32))
                 for _ in range(hg))

    def pair(jj, carry):
        scores(2 * jj + 1, 1)
        carry = step(2 * jj, 0, carry, False)
        scores(2 * jj + 2, 0)
        return step(2 * jj + 1, 1, carry, False)

    def finish(carry):
        for g in range(hg):
            _, l, acc = carry[g]
            o_ref[:, DV * g:DV * (g + 1)] = (acc / l).T.astype(o_ref.dtype)

    scores(0, 0)
    carry = lax.fori_loop(0, i // 2, pair, init)

    @pl.when(i % 2 == 0)
    def _():
        finish(step(i, 0, carry, True))

    @pl.when(i % 2 == 1)
    def _():
        scores(i, 1)
        finish(step(i, 1, step(i - 1, 0, carry, False), True))


def _mla_attn(qf, kf, vt, *, batch, seq, tq, hg=4):
    H = MLA_HEADS
    nq = seq // tq
    return pl.pallas_call(
        functools.partial(_mla_attn_kernel, tq=tq, hg=hg),
        grid=(batch, H // hg, nq),
        in_specs=[pl.BlockSpec((tq, 256 * hg), lambda b, h, i: (b * nq + i, h)),
                  pl.BlockSpec((seq, 256 * hg), lambda b, h, i: (b, h)),
                  pl.BlockSpec((MLA_DV * hg, seq), lambda b, h, i: (h, b))],
        out_specs=pl.BlockSpec((tq, MLA_DV * hg), lambda b, h, i: (b * nq + i, h)),
        out_shape=jax.ShapeDtypeStruct((batch * seq, H * MLA_DV), BF16),
        scratch_shapes=[pltpu.VMEM((2, hg, tq, tq), F32), pltpu.VMEM((2, hg, 1, tq), F32)],
        compiler_params=_cparams("parallel", "parallel", "arbitrary"),
        name="mla_attn",
    )(qf, kf, vt)


def _tile_for(n, pref):
    t = pref
    while n % t:
        t //= 2
    return t


def _even_mixer(x_bf, wt_all, w_o_all, j, wg2, bg, ng, *, batch, seq):
    m = x_bf.shape[0]
    gq_gk_gv = 2 * GLA_HEADS * GLA_DK + GLA_HEADS * GLA_DV
    rest0 = gq_gk_gv + GLA_GATE_RANK
    tm = _tile_for(m, 2048)
    mm = functools.partial(_matmul_t, x_bf, wt_all, layer=j, tm=tm, out_dtype=F32)
    y1, gts, wo_bf = mm(row0=0, ncols=gq_gk_gv, tn=512, side_rows=(gq_gk_gv,), cast=(w_o_all, j))
    y2 = mm(row0=rest0, ncols=wt_all.shape[1] - rest0, tn=512)
    o_a = _gla(y1, y2, gts, wg2, bg.reshape(1, -1), ng.reshape(1, -1), batch=batch, seq=seq,
               tile=_tile_for(seq, 512))
    o_b = _dilated(y2, GLA_HEADS * GLA_DV, batch=batch, seq=seq)
    return (o_a, o_b), wo_bf


def _rope_tables(positions):
    inv_freq = ROPE_THETA ** (-jnp.arange(0, MLA_ROPE, 2, dtype=F32) / MLA_ROPE)
    ang = positions.astype(F32).reshape(-1, 1) * inv_freq
    cos, sin = jnp.cos(ang), jnp.sin(ang)
    z32 = jnp.zeros_like(cos)
    z64 = jnp.zeros((cos.shape[0], LANES - MLA_ROPE), F32)
    c = jnp.concatenate([z64, cos, cos], axis=1)
    s1 = jnp.concatenate([z64, z32, sin], axis=1)
    s2 = jnp.concatenate([z64, -sin, z32], axis=1)
    return c, s1, s2


def _odd_mixer(x_bf, tabs, wt_all, w_o_all, j, conv_w, conv_b, bi, bf, ng, qg, kvg, wuq, wukv, *, batch, seq):
    m = x_bf.shape[0]
    H = MLA_HEADS
    cq_ck_cv = 2 * MLSTM_HEADS * MLSTM_DQK + MLSTM_HEADS * MLSTM_DV
    co0 = cq_ck_cv + 2 * MLSTM_HEADS
    kr0 = co0 + MLSTM_HEADS * MLSTM_DV + MLA_Q_RANK + MLA_KV_RANK
    gate_bias = jnp.concatenate([bi, bf, jnp.zeros((LANES - 2 * MLSTM_HEADS,), F32)]).reshape(1, LANES)
    wq3 = wuq.reshape(-1, H, MLA_NOPE + MLA_ROPE)
    wq = jnp.concatenate([wq3[:, :, :MLA_NOPE], jnp.zeros(wq3.shape[:2] + (LANES - MLA_ROPE,), F32),
                          wq3[:, :, MLA_NOPE:]], axis=2).reshape(-1, H * 256).astype(BF16)
    wkv = wukv.reshape(-1, H, MLA_NOPE + MLA_DV)
    wk = wkv[:, :, :MLA_NOPE].reshape(-1, H * MLA_NOPE).astype(BF16)
    wvt = wkv[:, :, MLA_NOPE:].reshape(-1, H * MLA_DV).T.astype(BF16)
    tm = _tile_for(m, 2048)
    mm = functools.partial(_matmul_t, x_bf, wt_all, layer=j, tm=tm, out_dtype=F32)
    y1, gts, kr, wo_bf = mm(row0=0, ncols=cq_ck_cv, tn=512, side_rows=(cq_ck_cv, kr0 + MLA_ROPE - LANES),
                            cast=(w_o_all, j))
    y2 = mm(row0=co0, ncols=kr0 - co0, tn=512)
    o_c = _mlstm(y1, y2, gts, conv_w, conv_b.reshape(1, -1), gate_bias, ng.reshape(1, -1),
                 batch=batch, seq=seq, tile=_tile_for(seq, 512))
    qf, kf, vt = _mla_proj(y2, MLSTM_HEADS * MLSTM_DV // MLA_Q_RANK, kr, qg.reshape(1, -1), kvg.reshape(1, -1),
                           wq, wk, wvt, tabs, tm=_tile_for(m, 512))
    o_d = _mla_attn(qf, kf, vt, batch=batch, seq=seq, tq=_tile_for(seq, 512))
    return (o_c, o_d), wo_bf


def kernel(x, positions, even_w_in, even_gla_wg2, even_gla_bg, even_gla_norm_g, even_w_o, odd_w_in, odd_conv_w, odd_conv_b, odd_mlstm_bi, odd_mlstm_bf, odd_mlstm_norm_g, odd_mla_qnorm_g, odd_mla_kvnorm_g, odd_mla_wuq, odd_mla_wukv, odd_w_o, ln1_g, ln1_b, ffn_wgu, ffn_wd, ln2_g, ln2_b):
    batch, seq, d = x.shape
    m = batch * seq
    xf = x.reshape(m, d)
    xb = xf.astype(BF16)
    tabs = _rope_tables(positions)
    even_wt = jnp.swapaxes(even_w_in, 1, 2)
    odd_wt = jnp.swapaxes(odd_w_in, 1, 2)
    depth = ln1_g.shape[0]
    tm_ln = _tile_for(m, 512)
    for l in range(depth):
        j = l // 2
        if l % 2 == 0:
            parts, wo_bf = _even_mixer(xb, even_wt, even_w_o, j, even_gla_wg2[j], even_gla_bg[j],
                                       even_gla_norm_g[j], batch=batch, seq=seq)
        else:
            parts, wo_bf = _odd_mixer(xb, tabs, odd_wt, odd_w_o, j, odd_conv_w[j], odd_conv_b[j],
                                      odd_mlstm_bi[j], odd_mlstm_bf[j], odd_mlstm_norm_g[j], odd_mla_qnorm_g[j],
                                      odd_mla_kvnorm_g[j], odd_mla_wuq[j], odd_mla_wukv[j], batch=batch, seq=seq)
        xf, xb = _mm_ln(parts, wo_bf, xf, ln1_g[l].reshape(1, -1), ln1_b[l].reshape(1, -1), tm=tm_ln)
        hid, wd_bf = _ffn_up(xb, ffn_wgu, ffn_wd, tm=_tile_for(m, 2048), tn=512, layer=l)
        xf, xb = _mm_ln((hid,), wd_bf, xf, ln2_g[l].reshape(1, -1), ln2_b[l].reshape(1, -1),
                        tm=_tile_for(m, 256))
    return xf.reshape(batch, seq, d)
```

```python
import functools

import jax
import jax.numpy as jnp
from jax import lax
from jax.experimental import pallas as pl
from jax.experimental.pallas import tpu as pltpu

F32 = jnp.float32
BF16 = jnp.bfloat16

DEPTH = 4
GLA_HEADS, GLA_DK, GLA_DV, GLA_GATE_RANK, GLA_TAU, GLA_CHUNK = 4, 128, 256, 16, 16.0, 64
DIL_PAIRS = ((128, 1), (512, 4), (2048, 16))
DIL_HEADS, DIL_HEAD_DIM, DIL_BLOCK = 4, 128, 128
MLSTM_HEADS, MLSTM_DQK, MLSTM_DV, MLSTM_CONV, MLSTM_CHUNK = 4, 128, 256, 4, 64
MLA_HEADS, MLA_Q_RANK, MLA_KV_RANK, MLA_NOPE, MLA_ROPE, MLA_DV = 8, 512, 512, 128, 64, 128
ROPE_THETA = 10000.0
DEEPNORM_ALPHA = (2.0 * DEPTH) ** 0.25

LANES = 128
VMEM_LIMIT = 56 * 1024 * 1024
NEG_INF = float("-inf")

_NT = (((1,), (1,)), ((), ()))


def _cparams(*sem):
    return pltpu.CompilerParams(dimension_semantics=sem, vmem_limit_bytes=VMEM_LIMIT)


def _bdot(a, b):
    return jnp.dot(a.astype(BF16), b.astype(BF16), preferred_element_type=F32)


def _bdot_nt(a, b):
    return lax.dot_general(a.astype(BF16), b.astype(BF16), _NT, preferred_element_type=F32)


def _bdot_tn(a, b):
    return jnp.dot(a.astype(F32).T.astype(BF16), b.astype(BF16), preferred_element_type=F32)


def _split2(a):
    hi = a.astype(BF16)
    lo = (a - hi.astype(F32)).astype(BF16)
    return hi, lo


def _dot3(a, b):
    ah, al = _split2(a)
    bh, bl = _split2(b)
    f = functools.partial(jnp.dot, preferred_element_type=F32)
    return f(ah, bh) + (f(ah, bl) + f(al, bh))


def _chunk_cumsum(tril_b, x):
    hi, lo = _split2(x)
    lo2 = (x - hi.astype(F32) - lo.astype(F32)).astype(BF16)
    f = functools.partial(jnp.dot, preferred_element_type=F32)
    return f(tril_b, hi) + (f(tril_b, lo) + f(tril_b, lo2))


def _tril(n):
    r = lax.broadcasted_iota(jnp.int32, (n, n), 0)
    c = lax.broadcasted_iota(jnp.int32, (n, n), 1)
    return c <= r


def _log_sigmoid(x):
    return jnp.minimum(x, 0.0) - jnp.log1p(jnp.exp(-jnp.abs(x)))


def _sigmoid(x):
    return 1.0 / (1.0 + jnp.exp(-x))


def _silu(x):
    return x * _sigmoid(x)


def _wspec(w, layer, rows, cols, index_map):
    if w.ndim == 2:
        return pl.BlockSpec((rows, cols), index_map)
    return pl.BlockSpec((None, rows, cols), lambda *ids: (layer,) + tuple(index_map(*ids)))


def _mm_nt_kernel(*refs, nside, ncast):
    a_ref, wt_ref, side_w, cast_in = refs[0], refs[1], refs[2:2 + nside], refs[2 + nside:2 + nside + ncast]
    outs = refs[2 + nside + ncast:]
    o_ref, side_o, cast_out = outs[0], outs[1:1 + nside], outs[1 + nside:]

    o_ref[...] = lax.dot_general(a_ref[...], wt_ref[0].astype(BF16), _NT,
                                 preferred_element_type=F32).astype(o_ref.dtype)
    for c_in, c_out in zip(cast_in, cast_out):
        c_out[...] = c_in[...].astype(c_out.dtype)

    if not side_w:
        return

    @pl.when(pl.program_id(1) == 0)
    def _():
        w = jnp.concatenate([w_ref[0] for w_ref in side_w], axis=0).astype(BF16)
        prod = lax.dot_general(a_ref[...], w, _NT, preferred_element_type=F32)
        for n, s_ref in enumerate(side_o):
            s_ref[...] = prod[:, n * LANES:(n + 1) * LANES].astype(s_ref.dtype)


def _matmul_t(a, wt, *, layer, row0, ncols, tm, tn, out_dtype, side_rows=(), cast=None, skip=None):
    m, k = a.shape
    nj = ncols // tn
    wspec = lambda rows, imap: pl.BlockSpec((pl.Element(1), pl.Element(rows), pl.Element(k)), imap)
    jump = (lambda j: 0) if skip is None else (lambda j: jnp.where(j >= skip[0], skip[1], 0))
    in_specs = ([pl.BlockSpec((tm, k), lambda i, j: (i, 0)),
                 wspec(tn, lambda i, j: (layer, pl.multiple_of(row0 + j * tn + jump(j), 8), 0))]
                + [wspec(LANES, lambda i, j, r=r: (layer, r, 0)) for r in side_rows])
    out_specs = ([pl.BlockSpec((tm, tn), lambda i, j: (i, j))]
                 + [pl.BlockSpec((tm, LANES), lambda i, j: (i, 0)) for _ in side_rows])
    out_shape = ([jax.ShapeDtypeStruct((m, ncols), out_dtype)]
                 + [jax.ShapeDtypeStruct((m, LANES), out_dtype) for _ in side_rows])
    args = [a, wt] + [wt] * len(side_rows)
    if cast is not None:
        cw, cl = cast
        rows, cols = cw.shape[1:]
        slab = rows // ((m // tm) * nj)
        assert slab * (m // tm) * nj == rows and slab % 16 == 0
        in_specs.append(pl.BlockSpec((None, slab, cols), lambda i, j: (cl, i * nj + j, 0)))
        out_specs.append(pl.BlockSpec((slab, cols), lambda i, j: (i * nj + j, 0)))
        out_shape.append(jax.ShapeDtypeStruct((rows, cols), BF16))
        args.append(cw)
    outs = pl.pallas_call(
        functools.partial(_mm_nt_kernel, nside=len(side_rows), ncast=int(cast is not None)),
        grid=(m // tm, nj),
        in_specs=in_specs, out_specs=out_specs, out_shape=out_shape,
        compiler_params=_cparams("parallel", "arbitrary"),
        name="mm_nt",
    )(*args)
    return outs[0] if len(outs) == 1 else tuple(outs)


FFN_COL_GROUP = 256


def _ffn_up_kernel(x_ref, wg_ref, wu_ref, wd_ref, o_ref, wdb_ref):
    x = x_ref[...]
    tn = o_ref.shape[1]
    for c0 in range(0, tn, FFN_COL_GROUP):
        cs = slice(c0, c0 + FFN_COL_GROUP)
        g = jnp.dot(x, wg_ref[:, cs].astype(BF16), preferred_element_type=F32)
        u = jnp.dot(x, wu_ref[:, cs].astype(BF16), preferred_element_type=F32)
        o_ref[:, cs] = (_silu(g) * u).astype(o_ref.dtype)
    wdb_ref[...] = wd_ref[...].astype(wdb_ref.dtype)


def _ffn_up(x_bf, w_gu, w_d, *, tm, tn, layer):
    m, k = x_bf.shape
    hidden = w_gu.shape[-1] // 2
    nb = hidden // tn
    steps = (m // tm) * nb
    slab = hidden // steps
    assert slab * steps == hidden and slab % 16 == 0
    return pl.pallas_call(
        _ffn_up_kernel,
        grid=(m // tm, nb),
        in_specs=[pl.BlockSpec((tm, k), lambda i, j: (i, 0)),
                  _wspec(w_gu, layer, k, tn, lambda i, j: (0, j)),
                  _wspec(w_gu, layer, k, tn, lambda i, j: (0, j + nb)),
                  _wspec(w_d, layer, slab, w_d.shape[-1], lambda i, j: (i * nb + j, 0))],
        out_specs=[pl.BlockSpec((tm, tn), lambda i, j: (i, j)),
                   pl.BlockSpec((slab, w_d.shape[-1]), lambda i, j: (i * nb + j, 0))],
        out_shape=[jax.ShapeDtypeStruct((m, hidden), BF16),
                   jax.ShapeDtypeStruct((hidden, w_d.shape[-1]), BF16)],
        compiler_params=_cparams("parallel", "arbitrary"),
        name="ffn_up",
    )(x_bf, w_gu, w_gu, w_d)


def _mm_ln_kernel(*refs):
    a_refs, (w_ref, res_ref, g_ref, b_ref, of_ref, ob_ref) = refs[:-6], refs[-6:]
    a = a_refs[0][...] if len(a_refs) == 1 else jnp.concatenate([r[...] for r in a_refs], axis=1)
    y = DEEPNORM_ALPHA * res_ref[...] + jnp.dot(a, w_ref[...], preferred_element_type=F32)
    mu = jnp.mean(y, axis=-1, keepdims=True)
    yc = y - mu
    var = jnp.mean(yc * yc, axis=-1, keepdims=True)
    out = yc * lax.rsqrt(var + 1e-5) * g_ref[...] + b_ref[...]
    of_ref[...] = out
    ob_ref[...] = out.astype(BF16)


def _mm_ln(parts, w, res, g, b, *, tm):
    m = res.shape[0]
    kdim, n = w.shape
    widths = tuple(p.shape[1] for p in parts)
    assert sum(widths) == kdim
    return pl.pallas_call(
        _mm_ln_kernel,
        grid=(m // tm,),
        in_specs=[pl.BlockSpec((tm, wd), lambda i: (i, 0)) for wd in widths] + [
            pl.BlockSpec((kdim, n), lambda i: (0, 0), pipeline_mode=pl.Buffered(1)),
            pl.BlockSpec((tm, n), lambda i: (i, 0)),
            pl.BlockSpec((1, n), lambda i: (0, 0)),
            pl.BlockSpec((1, n), lambda i: (0, 0))],
        out_specs=[pl.BlockSpec((tm, n), lambda i: (i, 0)),
                   pl.BlockSpec((tm, n), lambda i: (i, 0))],
        out_shape=[jax.ShapeDtypeStruct((m, n), F32), jax.ShapeDtypeStruct((m, n), BF16)],
        compiler_params=_cparams("parallel"),
        name="mm_ln",
    )(*parts, w, res, g, b)


def _gla_kernel(q_ref, k_ref, v_ref, r_ref, g_ref, wg2_ref, bg_ref, ng_ref, o_ref, st_ref, *, tile):
    L, H, DK, DV = GLA_CHUNK, GLA_HEADS, GLA_DK, GLA_DV

    @pl.when(pl.program_id(1) == 0)
    def _():
        st_ref[...] = jnp.zeros_like(st_ref)

    tril = _tril(L)
    tril_b = tril.astype(BF16)
    g16 = g_ref[:, :GLA_GATE_RANK]
    nchunk = tile // L
    hc = [(h, c) for c in range(nchunk) for h in range(H)]
    ksl = lambda h: slice(h * DK, (h + 1) * DK)
    vsl = lambda h: slice(h * DV, (h + 1) * DV)
    rsl = lambda c: slice(c * L, (c + 1) * L)
    log_a = [_log_sigmoid(_dot3(g16, wg2_ref[:, ksl(h)]) + bg_ref[:, ksl(h)]) / GLA_TAU for h in range(H)]
    b = {(h, c): _chunk_cumsum(tril_b, log_a[h][rsl(c)]) for h, c in hc}
    q_dec, k_inv, k_end, decay, vb = {}, {}, {}, {}, {}
    for h, c in hc:
        bb = b[h, c]
        b_last = bb[L - 1:L]
        kk = k_ref[rsl(c), ksl(h)]
        q_dec[h, c] = (q_ref[rsl(c), ksl(h)] * DK ** -0.5 * jnp.exp(bb)).astype(BF16)
        k_inv[h, c] = (kk * jnp.exp(-bb)).astype(BF16)
        k_end[h, c] = (kk * jnp.exp(b_last - bb)).astype(BF16)
        decay[h, c] = jnp.exp(b_last)
        vb[h, c] = v_ref[rsl(c), vsl(h)]
    scores = {p: _bdot_nt(q_dec[p], k_inv[p]) for p in hc}
    scores = {p: jnp.where(tril, scores[p], 0.0).astype(BF16) for p in hc}
    intra = {p: _bdot(scores[p], vb[p]) for p in hc}
    s_loc = {p: _bdot_tn(vb[p], k_end[p]) for p in hc}
    st = [st_ref[h] for h in range(H)]
    outs = {}
    for h, c in hc:
        outs[h, c] = intra[h, c] + _bdot_nt(q_dec[h, c], st[h])
        st[h] = st[h] * decay[h, c] + s_loc[h, c]
    for h in range(H):
        st_ref[h] = st[h]
        o = jnp.concatenate([outs[h, c] for c in range(nchunk)], axis=0)
        o = o * lax.rsqrt(jnp.mean(o * o, axis=-1, keepdims=True) + 1e-6) * ng_ref[...]
        o_ref[:, vsl(h)] = (o * _silu(r_ref[:, vsl(h)])).astype(o_ref.dtype)


def _gla(y1, y2, gates, wg2, bg, ng, *, batch, seq, tile):
    nt = seq // tile
    qk_w, v_w = GLA_HEADS * GLA_DK, GLA_HEADS * GLA_DV
    row = lambda b, t: b * nt + t
    return pl.pallas_call(
        functools.partial(_gla_kernel, tile=tile),
        grid=(batch, nt),
        in_specs=[pl.BlockSpec((tile, qk_w), lambda b, t: (row(b, t), 0)),
                  pl.BlockSpec((tile, qk_w), lambda b, t: (row(b, t), 1)),
                  pl.BlockSpec((tile, v_w), lambda b, t: (row(b, t), 1)),
                  pl.BlockSpec((tile, v_w), lambda b, t: (row(b, t), 0)),
                  pl.BlockSpec((tile, LANES), lambda b, t: (row(b, t), 0)),
                  pl.BlockSpec((GLA_GATE_RANK, qk_w), lambda b, t: (0, 0)),
                  pl.BlockSpec((1, qk_w), lambda b, t: (0, 0)),
                  pl.BlockSpec((1, GLA_DV), lambda b, t: (0, 0))],
        out_specs=pl.BlockSpec((tile, v_w), lambda b, t: (row(b, t), 0)),
        out_shape=jax.ShapeDtypeStruct((batch * seq, v_w), BF16),
        scratch_shapes=[pltpu.VMEM((GLA_HEADS, GLA_DV, GLA_DK), F32)],
        compiler_params=_cparams("parallel", "arbitrary"),
        name="gla",
    )(y1, y1, y1, y2, gates, wg2, bg, ng)


LOG2E = 1.4426950408889634
LN2 = 0.6931471805599453
DIL_STRIDE1 = 4


def _dil_kernel(*refs, unit):
    ngrp = len(DIL_PAIRS)
    in_refs, (o_ref, og_ref, lg_ref, rg_ref) = refs[:5 * ngrp], refs[5 * ngrp:]
    u = pl.program_id(1)
    n = DIL_BLOCK
    qscale = DIL_HEAD_DIM ** -0.5 * LOG2E
    row = lax.broadcasted_iota(jnp.int32, (n, n), 0)
    col = lax.broadcasted_iota(jnp.int32, (n, n), 1)
    prev_band, cur_band = col >= row, col <= row
    ones = jnp.ones((n, LANES), BF16)
    for g, (_, d) in enumerate(DIL_PAIRS):
        q_ref, k_ref, v_ref, kp_ref, vp_ref = in_refs[5 * g:5 * g + 5]
        sub = n * d
        blocks = []
        if d > DIL_STRIDE1:
            d2 = d // DIL_STRIDE1
            assert unit == sub and d2 <= DIL_STRIDE1
            per = unit // DIL_STRIDE1
            for a, src in enumerate((q_ref, k_ref, v_ref, kp_ref, vp_ref)):
                for r1 in range(DIL_STRIDE1):
                    rg_ref[a, r1] = src[pl.ds(r1, per, stride=DIL_STRIDE1), :]
            for r in range(d):
                at = (r % DIL_STRIDE1, pl.ds(r // DIL_STRIDE1, n, stride=d2))
                blocks.append(tuple((rg_ref, (a,) + at) for a in range(5))
                              + (prev_band & (u > 0), ((rg_ref, (5,) + at), (rg_ref, (6,) + at))))
        else:
            for s in range(unit // sub):
                for r in range(d):
                    cur = pl.ds(s * sub + r, n, stride=d) if d > 1 else pl.ds(s * sub, n)
                    outs = ((og_ref, (g, cur)), (lg_ref, (g, cur)))
                    if s == 0:
                        prv = pl.ds(r, n, stride=d) if d > 1 else pl.ds(0, n)
                        blocks.append(((q_ref, (cur,)), (k_ref, (cur,)), (v_ref, (cur,)), (kp_ref, (prv,)),
                                       (vp_ref, (prv,)), prev_band & (u > 0), outs))
                    else:
                        prv = pl.ds((s - 1) * sub + r, n, stride=d) if d > 1 else pl.ds((s - 1) * sub, n)
                        blocks.append(((q_ref, (cur,)), (k_ref, (cur,)), (v_ref, (cur,)), (k_ref, (prv,)),
                                       (v_ref, (prv,)), prev_band, outs))
        get = lambda ra: ra[0][ra[1] + (slice(None),)]
        scores = []
        for qa, kc, _, kp, _, pmask, _ in blocks:
            q = (get(qa) * qscale).astype(BF16)
            scores.append((jnp.where(pmask, _bdot_nt(q, get(kp)), NEG_INF),
                           jnp.where(cur_band, _bdot_nt(q, get(kc)), NEG_INF)))
        probs = []
        for s_p, s_c in scores:
            m = jnp.maximum(jnp.max(s_p, axis=-1, keepdims=True), jnp.max(s_c, axis=-1, keepdims=True))
            probs.append((jnp.exp2(s_p - m).astype(BF16), jnp.exp2(s_c - m).astype(BF16), m))
        for (_, _, vc, _, vp, _, ((o_dst, o_at), (l_dst, l_at))), (p_p, p_c, m) in zip(blocks, probs):
            den = _bdot(p_p, ones) + _bdot(p_c, ones)
            o_dst[o_at + (slice(None),)] = (_bdot(p_p, get(vp)) + _bdot(p_c, get(vc))) / den
            l_dst[l_at + (slice(None),)] = m * LN2 + jnp.log(den)
        if d > DIL_STRIDE1:
            for r1 in range(DIL_STRIDE1):
                og_ref[g, pl.ds(r1, per, stride=DIL_STRIDE1), :] = rg_ref[5, r1]
                lg_ref[g, pl.ds(r1, per, stride=DIL_STRIDE1), :] = rg_ref[6, r1]
    lses = [lg_ref[g] for g in range(ngrp)]
    m = functools.reduce(jnp.maximum, lses)
    ws = [jnp.exp(l - m) for l in lses]
    tot = functools.reduce(lambda a, b: a + b, ws)
    acc = sum(w * og_ref[g] for g, w in enumerate(ws))
    o_ref[...] = (acc / tot).astype(o_ref.dtype)


def _dilated(y, col0, *, batch, seq):
    ngrp, H, n = len(DIL_PAIRS), DIL_HEADS, DIL_BLOCK
    unit = max(d for _, d in DIL_PAIRS) * n
    assert seq % unit == 0
    nu = seq // unit
    cb0 = col0 // LANES
    in_specs, args = [], []
    for g, (_, d) in enumerate(DIL_PAIRS):
        sub = n * d
        per = unit // sub
        for part in range(3):
            cb = cb0 + part * ngrp * H + g * H
            in_specs.append(pl.BlockSpec((unit, LANES), lambda b, u, h, cb=cb: (b * nu + u, cb + h)))
            args.append(y)
        for part in (1, 2):
            cb = cb0 + part * ngrp * H + g * H
            in_specs.append(pl.BlockSpec(
                (sub, LANES),
                lambda b, u, h, cb=cb, per=per: (jnp.maximum(b * nu * per + u * per - 1, 0), cb + h)))
            args.append(y)
    return pl.pallas_call(
        functools.partial(_dil_kernel, unit=unit),
        grid=(batch, nu, H),
        in_specs=in_specs,
        out_specs=pl.BlockSpec((unit, LANES), lambda b, u, h: (b * nu + u, h)),
        out_shape=jax.ShapeDtypeStruct((batch * seq, H * LANES), BF16),
        scratch_shapes=[pltpu.VMEM((ngrp, unit, LANES), F32), pltpu.VMEM((ngrp, unit, LANES), F32),
                        pltpu.VMEM((7, DIL_STRIDE1, unit // DIL_STRIDE1, LANES), F32)],
        compiler_params=_cparams("parallel", "parallel", "arbitrary"),
        name="dilated",
    )(*args)


I_LANE, F_LANE = 0, MLSTM_HEADS


def _mlstm_kernel(qk_ref, v_ref, co_ref, gt_ref, cw_ref, cb_ref, gb_ref, ng_ref, o_ref,
                  c_ref, m_ref, tail_ref, xbuf_ref, *, tile):
    L, H, DK, DV = MLSTM_CHUNK, MLSTM_HEADS, MLSTM_DQK, MLSTM_DV
    KC = MLSTM_CONV
    PAD = 8

    @pl.when(pl.program_id(1) == 0)
    def _():
        c_ref[...] = jnp.zeros_like(c_ref)
        m_ref[...] = jnp.zeros_like(m_ref)
        tail_ref[...] = jnp.zeros_like(tail_ref)

    xbuf_ref[0:PAD, :] = tail_ref[...]
    xbuf_ref[PAD:PAD + tile, :] = qk_ref[...]
    tail_ref[...] = qk_ref[tile - PAD:tile, :]
    acc = cb_ref[...] + cw_ref[KC - 1:KC, :] * xbuf_ref[PAD:PAD + tile, :]
    for j in range(KC - 1):
        off = PAD - (KC - 1) + j
        acc = acc + cw_ref[j:j + 1, :] * xbuf_ref[off:off + tile, :]
    qk = _silu(acc)

    gates = gt_ref[...] + gb_ref[...]
    lane = lax.broadcasted_iota(jnp.int32, gates.shape, 1)
    z = jnp.where(lane >= F_LANE, _log_sigmoid(gates), gates)
    tril = _tril(L)
    tril_b = tril.astype(BF16)
    nchunk = tile // L
    lane_c = lax.broadcasted_iota(jnp.int32, (L, LANES), 1)
    zs, bs, wts = [], [], []
    for c in range(nchunk):
        zc = z[c * L:(c + 1) * L]
        bc = _chunk_cumsum(tril_b, zc)
        zs.append(zc)
        bs.append(bc)
        wts.append(jnp.where(lane_c >= F_LANE, bc, zc).T)
    ones_col = (lane_c == 0).astype(F32)

    hc = [(h, c) for c in range(nchunk) for h in range(H)]
    rsl = lambda c: slice(c * L, (c + 1) * L)
    row_c = lax.broadcasted_iota(jnp.int32, (L, LANES), 0)
    m_run = m_ref[0:1, :]
    kw_all, mt_all, isc_all, floor_all, so_all, sn_all = [], [], [], [], [], []
    for c in range(nchunk):
        bc = bs[c]
        li_a = pltpu.roll(zs[c], F_LANE - I_LANE, 1)
        b_last = bc[L - 1:L]
        a = b_last - bc + li_a
        m_loc = jnp.max(a, axis=0, keepdims=True)
        kw_all.append(jnp.exp(a - m_loc))
        x = li_a - bc
        for s in (1, 2, 4, 8, 16, 32):
            x = jnp.maximum(x, jnp.where(row_c >= s, pltpu.roll(x, s, 0), NEG_INF))
        inter_log = bc + m_run
        m_t = jnp.maximum(inter_log, bc + x)
        mt_all.append(m_t)
        isc_all.append(jnp.exp(inter_log - m_t))
        floor_all.append(jnp.exp(-m_t))
        m_new = jnp.maximum(b_last + m_run, m_loc)
        so_all.append(jnp.exp(b_last + m_run - m_new))
        sn_all.append(jnp.exp(m_loc - m_new))
        m_run = m_new
    m_ref[0:1, :] = m_run
    col = lambda x, h: x[:, F_LANE + h:F_LANE + h + 1]
    qb, kb, vb, qkm = {}, {}, {}, {}
    for h, c in hc:
        qb[h, c] = qk[rsl(c), h * DK:(h + 1) * DK].astype(BF16)
        kb[h, c] = qk[rsl(c), (H + h) * DK:(H + h + 1) * DK] * DK ** -0.5
        vb[h, c] = v_ref[rsl(c), h * DV:(h + 1) * DV]
    raw = {p: _bdot_nt(qb[p], kb[p]) for p in hc}
    for h, c in hc:
        b_row = wts[c][F_LANE + h:F_LANE + h + 1, :]
        li_row = wts[c][I_LANE + h:I_LANE + h + 1, :]
        d_log = jnp.where(tril, col(bs[c], h) - b_row + li_row, NEG_INF)
        qkm[h, c] = raw[h, c] * jnp.exp(d_log - col(mt_all[c], h))
    v_aug = {p: jnp.concatenate([vb[p], ones_col], axis=1).astype(BF16) for p in hc}
    intra = {p: _bdot(qkm[p], v_aug[p]) for p in hc}
    c_loc = {(h, c): _bdot_tn(kb[h, c] * col(kw_all[c], h), v_aug[h, c]) for h, c in hc}
    ct = [c_ref[h] for h in range(H)]
    outs = {}
    for h, c in hc:
        p = (h, c)
        tot = intra[p] + col(isc_all[c], h) * _bdot(qb[p], ct[h])
        outs[p] = tot[:, :DV] / jnp.maximum(jnp.abs(tot[:, DV:DV + 1]), col(floor_all[c], h))
        ct[h] = ct[h] * col(so_all[c], h) + c_loc[p] * col(sn_all[c], h)
    for h in range(H):
        c_ref[h] = ct[h]
        hcat = jnp.concatenate([outs[h, c] for c in range(nchunk)], axis=0)
        mu = jnp.mean(hcat, axis=-1, keepdims=True)
        hcen = hcat - mu
        var = jnp.mean(hcen * hcen, axis=-1, keepdims=True)
        hn = hcen * lax.rsqrt(var + 1e-5) * ng_ref[...]
        o_ref[:, h * DV:(h + 1) * DV] = (_sigmoid(co_ref[:, h * DV:(h + 1) * DV]) * hn).astype(o_ref.dtype)


def _mlstm(y1, y2, gates, conv_w, conv_b, gate_bias, ng, *, batch, seq, tile, co_block=0):
    nt = seq // tile
    H, DK, DV = MLSTM_HEADS, MLSTM_DQK, MLSTM_DV
    qk_w, v_w = 2 * H * DK, H * DV
    row = lambda b, t: b * nt + t
    return pl.pallas_call(
        functools.partial(_mlstm_kernel, tile=tile),
        grid=(batch, nt),
        in_specs=[pl.BlockSpec((tile, qk_w), lambda b, t: (row(b, t), 0)),
                  pl.BlockSpec((tile, v_w), lambda b, t: (row(b, t), 1)),
                  pl.BlockSpec((tile, v_w), lambda b, t: (row(b, t), co_block)),
                  pl.BlockSpec((tile, LANES), lambda b, t: (row(b, t), 0)),
                  pl.BlockSpec((MLSTM_CONV, qk_w), lambda b, t: (0, 0)),
                  pl.BlockSpec((1, qk_w), lambda b, t: (0, 0)),
                  pl.BlockSpec((1, LANES), lambda b, t: (0, 0)),
                  pl.BlockSpec((1, DV), lambda b, t: (0, 0))],
        out_specs=pl.BlockSpec((tile, v_w), lambda b, t: (row(b, t), 0)),
        out_shape=jax.ShapeDtypeStruct((batch * seq, v_w), BF16),
        scratch_shapes=[pltpu.VMEM((H, DK, DV + LANES), F32),
                        pltpu.VMEM((8, LANES), F32),
                        pltpu.VMEM((8, qk_w), F32),
                        pltpu.VMEM((tile + 8, qk_w), F32)],
        compiler_params=_cparams("parallel", "arbitrary"),
        name="mlstm",
    )(y1, y1, y2, gates, conv_w, conv_b, gate_bias, ng)


def _rope128(x, c_ref, s1_ref, s2_ref):
    return x * c_ref[...] + pltpu.roll(x, 32, 1) * s1_ref[...] + pltpu.roll(x, LANES - 32, 1) * s2_ref[...]


def _rms(x, g):
    return x * lax.rsqrt(jnp.mean(x * x, axis=-1, keepdims=True) + 1e-6) * g


MLA_SCORE_SCALE = (MLA_NOPE + MLA_ROPE) ** -0.5 * 1.4426950408889634


def _mla_proj_kernel(xq_ref, xkv_ref, qg_ref, kvg_ref, wq_ref, wk_ref, wvt_ref, kr_ref, c_ref, s1_ref, s2_ref,
                     q_ref, k_ref, vt_ref):
    q = _bdot(_rms(xq_ref[...], qg_ref[...]), wq_ref[...]) * MLA_SCORE_SCALE
    xn = _rms(xkv_ref[...], kvg_ref[...]).astype(BF16)
    kn = jnp.dot(xn, wk_ref[...], preferred_element_type=F32)
    vt_ref[...] = lax.dot_general(wvt_ref[...], xn, _NT, preferred_element_type=F32).astype(vt_ref.dtype)
    kr = _rope128(kr_ref[...], c_ref, s1_ref, s2_ref).astype(k_ref.dtype)
    for h in range(MLA_HEADS):
        q_ref[:, 256 * h:256 * h + 128] = q[:, 256 * h:256 * h + 128].astype(q_ref.dtype)
        q_ref[:, 256 * h + 128:256 * h + 256] = _rope128(
            q[:, 256 * h + 128:256 * h + 256], c_ref, s1_ref, s2_ref).astype(q_ref.dtype)
        k_ref[:, 256 * h:256 * h + 128] = kn[:, 128 * h:128 * h + 128].astype(k_ref.dtype)
        k_ref[:, 256 * h + 128:256 * h + 256] = kr


def _mla_proj(y, qcb, gates, qg, kvg, wq, wk, wvt, tabs, *, tm):
    m = y.shape[0]
    H = MLA_HEADS
    row_block = lambda width, col: pl.BlockSpec((tm, width), lambda i: (i, col))
    full = lambda a: pl.BlockSpec(a.shape, lambda i: (0, 0))
    return pl.pallas_call(
        _mla_proj_kernel,
        grid=(m // tm,),
        in_specs=[row_block(MLA_Q_RANK, qcb), row_block(MLA_KV_RANK, qcb + 1), full(qg), full(kvg),
                  full(wq), full(wk), full(wvt)] + [row_block(LANES, 0)] * 4,
        out_specs=[row_block(H * 256, 0), row_block(H * 256, 0),
                   pl.BlockSpec((H * MLA_DV, tm), lambda i: (0, i))],
        out_shape=[jax.ShapeDtypeStruct((m, H * 256), BF16), jax.ShapeDtypeStruct((m, H * 256), BF16),
                   jax.ShapeDtypeStruct((H * MLA_DV, m), BF16)],
        compiler_params=_cparams("parallel"),
        name="mla_proj",
    )(y, y, qg, kvg, wq, wk, wvt, gates, *tabs)


def _mla_attn_kernel(q_ref, k_ref, vt_ref, o_ref, s_ref, mx_ref, *, tq, hg):
    i = pl.program_id(2)
    DV = MLA_DV

    def scores(j, slot):
        start = pl.multiple_of(j * tq, tq)
        for g in range(hg):
            st = lax.dot_general(k_ref[pl.ds(start, tq), 256 * g:256 * (g + 1)],
                                 q_ref[:, 256 * g:256 * (g + 1)], _NT, preferred_element_type=F32)
            s_ref[slot, g] = st
            mx_ref[slot, g] = jnp.max(st, axis=0, keepdims=True)

    def step(j, slot, carry, masked):
        start = pl.multiple_of(j * tq, tq)
        out = []
        for g in range(hg):
            m, l, acc = carry[g]
            vtb = vt_ref[DV * g:DV * (g + 1), pl.ds(start, tq)]
            st = s_ref[slot, g]
            if masked:
                kk = lax.broadcasted_iota(jnp.int32, (tq, tq), 0)
                qq = lax.broadcasted_iota(jnp.int32, (tq, tq), 1)
                st = jnp.where(kk <= qq, st, NEG_INF)
                m_new = jnp.maximum(m, jnp.max(st, axis=0, keepdims=True))
            else:
                m_new = jnp.maximum(m, mx_ref[slot, g])
            pt = jnp.exp2(st - m_new)
            a = jnp.exp2(m - m_new)
            l = a * l + jnp.sum(pt, axis=0, keepdims=True)
            acc = a * acc + jnp.dot(vtb, pt.astype(BF16), preferred_element_type=F32)
            out.append((m_new, l, acc))
        return tuple(out)

    init = tuple((jnp.full((1, tq), NEG_INF, F32), jnp.zeros((1, tq), F32), jnp.zeros((DV, tq), F32))
                 for _ in range(hg))

    def pair(jj, carry):
        scores(2 * jj + 1, 1)
        carry = step(2 * jj, 0, carry, False)
        scores(2 * jj + 2, 0)
        return step(2 * jj + 1, 1, carry, False)

    def finish(carry):
        for g in range(hg):
            _, l, acc = carry[g]
            o_ref[:, DV * g:DV * (g + 1)] = (acc / l).T.astype(o_ref.dtype)

    scores(0, 0)
    carry = lax.fori_loop(0, i // 2, pair, init)

    @pl.when(i % 2 == 0)
    def _():
        finish(step(i, 0, carry, True))

    @pl.when(i % 2 == 1)
    def _():
        scores(i, 1)
        finish(step(i, 1, step(i - 1, 0, carry, False), True))


def _mla_attn(qf, kf, vt, *, batch, seq, tq, hg=4):
    H = MLA_HEADS
    nq = seq // tq
    return pl.pallas_call(
        functools.partial(_mla_attn_kernel, tq=tq, hg=hg),
        grid=(batch, H // hg, nq),
        in_specs=[pl.BlockSpec((tq, 256 * hg), lambda b, h, i: (b * nq + i, h)),
                  pl.BlockSpec((seq, 256 * hg), lambda b, h, i: (b, h)),
                  pl.BlockSpec((MLA_DV * hg, seq), lambda b, h, i: (h, b))],
        out_specs=pl.BlockSpec((tq, MLA_DV * hg), lambda b, h, i: (b * nq + i, h)),
        out_shape=jax.ShapeDtypeStruct((batch * seq, H * MLA_DV), BF16),
        scratch_shapes=[pltpu.VMEM((2, hg, tq, tq), F32), pltpu.VMEM((2, hg, 1, tq), F32)],
        compiler_params=_cparams("parallel", "parallel", "arbitrary"),
        name="mla_attn",
    )(qf, kf, vt)


def _tile_for(n, pref):
    t = pref
    while n % t:
        t //= 2
    return t


def _even_mixer(x_bf, wt_all, w_o_all, j, wg2, bg, ng, *, batch, seq):
    m = x_bf.shape[0]
    gq_gk_gv = 2 * GLA_HEADS * GLA_DK + GLA_HEADS * GLA_DV
    rest0 = gq_gk_gv + GLA_GATE_RANK
    tm = _tile_for(m, 2048)
    mm = functools.partial(_matmul_t, x_bf, wt_all, layer=j, tm=tm, out_dtype=F32)
    y1, gts, wo_bf = mm(row0=0, ncols=gq_gk_gv, tn=512, side_rows=(gq_gk_gv,), cast=(w_o_all, j))
    y2 = mm(row0=rest0, ncols=wt_all.shape[1] - rest0, tn=512)
    o_a = _gla(y1, y2, gts, wg2, bg.reshape(1, -1), ng.reshape(1, -1), batch=batch, seq=seq,
               tile=_tile_for(seq, 512))
    o_b = _dilated(y2, GLA_HEADS * GLA_DV, batch=batch, seq=seq)
    return (o_a, o_b), wo_bf


def _rope_tables(positions):
    inv_freq = ROPE_THETA ** (-jnp.arange(0, MLA_ROPE, 2, dtype=F32) / MLA_ROPE)
    ang = positions.astype(F32).reshape(-1, 1) * inv_freq
    cos, sin = jnp.cos(ang), jnp.sin(ang)
    z32 = jnp.zeros_like(cos)
    z64 = jnp.zeros((cos.shape[0], LANES - MLA_ROPE), F32)
    c = jnp.concatenate([z64, cos, cos], axis=1)
    s1 = jnp.concatenate([z64, z32, sin], axis=1)
    s2 = jnp.concatenate([z64, -sin, z32], axis=1)
    return c, s1, s2


def _odd_mixer(x_bf, tabs, wt_all, w_o_all, j, conv_w, conv_b, bi, bf, ng, qg, kvg, wuq, wukv, *, batch, seq):
    m = x_bf.shape[0]
    H = MLA_HEADS
    cq_ck_cv = 2 * MLSTM_HEADS * MLSTM_DQK + MLSTM_HEADS * MLSTM_DV
    co0 = cq_ck_cv + 2 * MLSTM_HEADS
    kr0 = co0 + MLSTM_HEADS * MLSTM_DV + MLA_Q_RANK + MLA_KV_RANK
    gate_bias = jnp.concatenate([bi, bf, jnp.zeros((LANES - 2 * MLSTM_HEADS,), F32)]).reshape(1, LANES)
    wq3 = wuq.reshape(-1, H, MLA_NOPE + MLA_ROPE)
    wq = jnp.concatenate([wq3[:, :, :MLA_NOPE], jnp.zeros(wq3.shape[:2] + (LANES - MLA_ROPE,), F32),
                          wq3[:, :, MLA_NOPE:]], axis=2).reshape(-1, H * 256).astype(BF16)
    wkv = wukv.reshape(-1, H, MLA_NOPE + MLA_DV)
    wk = wkv[:, :, :MLA_NOPE].reshape(-1, H * MLA_NOPE).astype(BF16)
    wvt = wkv[:, :, MLA_NOPE:].reshape(-1, H * MLA_DV).T.astype(BF16)
    tm = _tile_for(m, 2048)
    mm = functools.partial(_matmul_t, x_bf, wt_all, layer=j, tm=tm, out_dtype=F32)
    tn = 512
    y, gts, kr, wo_bf = mm(row0=0, ncols=kr0 - (co0 - cq_ck_cv), tn=tn, skip=(cq_ck_cv // tn, co0 - cq_ck_cv),
                           side_rows=(cq_ck_cv, kr0 + MLA_ROPE - LANES), cast=(w_o_all, j))
    co_col = cq_ck_cv
    o_c = _mlstm(y, y, gts, conv_w, conv_b.reshape(1, -1), gate_bias, ng.reshape(1, -1),
                 batch=batch, seq=seq, tile=_tile_for(seq, 512), co_block=co_col // (MLSTM_HEADS * MLSTM_DV))
    qf, kf, vt = _mla_proj(y, (co_col + MLSTM_HEADS * MLSTM_DV) // MLA_Q_RANK, kr, qg.reshape(1, -1),
                           kvg.reshape(1, -1), wq, wk, wvt, tabs, tm=_tile_for(m, 512))
    o_d = _mla_attn(qf, kf, vt, batch=batch, seq=seq, tq=_tile_for(seq, 512))
    return (o_c, o_d), wo_bf


def kernel(x, positions, even_w_in, even_gla_wg2, even_gla_bg, even_gla_norm_g, even_w_o, odd_w_in, odd_conv_w, odd_conv_b, odd_mlstm_bi, odd_mlstm_bf, odd_mlstm_norm_g, odd_mla_qnorm_g, odd_mla_kvnorm_g, odd_mla_wuq, odd_mla_wukv, odd_w_o, ln1_g, ln1_b, ffn_wgu, ffn_wd, ln2_g, ln2_b):
    batch, seq, d = x.shape
    m = batch * seq
    xf = x.reshape(m, d)
    xb = xf.astype(BF16)
    tabs = _rope_tables(positions)
    even_wt = jnp.swapaxes(even_w_in, 1, 2)
    odd_wt = jnp.swapaxes(odd_w_in, 1, 2)
    depth = ln1_g.shape[0]
    tm_ln = _tile_for(m, 512)
    for l in range(depth):
        j = l // 2
        if l % 2 == 0:
            parts, wo_bf = _even_mixer(xb, even_wt, even_w_o, j, even_gla_wg2[j], even_gla_bg[j],
                                       even_gla_norm_g[j], batch=batch, seq=seq)
        else:
            parts, wo_bf = _odd_mixer(xb, tabs, odd_wt, odd_w_o, j, odd_conv_w[j], odd_conv_b[j],
                                      odd_mlstm_bi[j], odd_mlstm_bf[j], odd_mlstm_norm_g[j], odd_mla_qnorm_g[j],
                                      odd_mla_kvnorm_g[j], odd_mla_wuq[j], odd_mla_wukv[j], batch=batch, seq=seq)
        xf, xb = _mm_ln(parts, wo_bf, xf, ln1_g[l].reshape(1, -1), ln1_b[l].reshape(1, -1), tm=tm_ln)
        hid, wd_bf = _ffn_up(xb, ffn_wgu, ffn_wd, tm=_tile_for(m, 2048), tn=512, layer=l)
        xf, xb = _mm_ln((hid,), wd_bf, xf, ln2_g[l].reshape(1, -1), ln2_b[l].reshape(1, -1),
                        tm=_tile_for(m, 256))
    return xf.reshape(batch, seq, d)
```

```python
import functools

import jax
import jax.numpy as jnp
from jax import lax
from jax.experimental import pallas as pl
from jax.experimental.pallas import tpu as pltpu

F32 = jnp.float32
BF16 = jnp.bfloat16

DEPTH = 4
GLA_HEADS, GLA_DK, GLA_DV, GLA_GATE_RANK, GLA_TAU, GLA_CHUNK = 4, 128, 256, 16, 16.0, 64
DIL_PAIRS = ((128, 1), (512, 4), (2048, 16))
DIL_HEADS, DIL_HEAD_DIM, DIL_BLOCK = 4, 128, 128
MLSTM_HEADS, MLSTM_DQK, MLSTM_DV, MLSTM_CONV, MLSTM_CHUNK = 4, 128, 256, 4, 64
MLA_HEADS, MLA_Q_RANK, MLA_KV_RANK, MLA_NOPE, MLA_ROPE, MLA_DV = 8, 512, 512, 128, 64, 128
ROPE_THETA = 10000.0
DEEPNORM_ALPHA = (2.0 * DEPTH) ** 0.25

LANES = 128
VMEM_LIMIT = 56 * 1024 * 1024
NEG_INF = float("-inf")

_NT = (((1,), (1,)), ((), ()))


def _cparams(*sem):
    return pltpu.CompilerParams(dimension_semantics=sem, vmem_limit_bytes=VMEM_LIMIT)


def _bdot(a, b):
    return jnp.dot(a.astype(BF16), b.astype(BF16), preferred_element_type=F32)


def _bdot_nt(a, b):
    return lax.dot_general(a.astype(BF16), b.astype(BF16), _NT, preferred_element_type=F32)


def _bdot_tn(a, b):
    return jnp.dot(a.astype(F32).T.astype(BF16), b.astype(BF16), preferred_element_type=F32)


def _split2(a):
    hi = a.astype(BF16)
    lo = (a - hi.astype(F32)).astype(BF16)
    return hi, lo


def _dot3(a, b):
    ah, al = _split2(a)
    bh, bl = _split2(b)
    f = functools.partial(jnp.dot, preferred_element_type=F32)
    return f(ah, bh) + (f(ah, bl) + f(al, bh))


def _chunk_cumsum(tril_b, x):
    hi, lo = _split2(x)
    lo2 = (x - hi.astype(F32) - lo.astype(F32)).astype(BF16)
    f = functools.partial(jnp.dot, preferred_element_type=F32)
    return f(tril_b, hi) + (f(tril_b, lo) + f(tril_b, lo2))


def _tril(n):
    r = lax.broadcasted_iota(jnp.int32, (n, n), 0)
    c = lax.broadcasted_iota(jnp.int32, (n, n), 1)
    return c <= r


def _log_sigmoid(x):
    return jnp.minimum(x, 0.0) - jnp.log1p(jnp.exp(-jnp.abs(x)))


def _sigmoid(x):
    return 1.0 / (1.0 + jnp.exp(-x))


def _silu(x):
    return x * _sigmoid(x)


def _wspec(w, layer, rows, cols, index_map):
    if w.ndim == 2:
        return pl.BlockSpec((rows, cols), index_map)
    return pl.BlockSpec((None, rows, cols), lambda *ids: (layer,) + tuple(index_map(*ids)))


def _mm_nt_kernel(*refs, nside, ncast, cast_steps):
    a_ref, wt_ref, side_w, cast_in = refs[0], refs[1], refs[2:2 + nside], refs[2 + nside:2 + nside + ncast]
    outs = refs[2 + nside + ncast:]
    o_ref, side_o, cast_out = outs[0], outs[1:1 + nside], outs[1 + nside:]

    o_ref[...] = lax.dot_general(a_ref[...], wt_ref[0].astype(BF16), _NT,
                                 preferred_element_type=F32).astype(o_ref.dtype)
    if cast_in:
        step = pl.program_id(0) * pl.num_programs(1) + pl.program_id(1)

        @pl.when(step < cast_steps)
        def _():
            cast_out[0][...] = cast_in[0][...].astype(cast_out[0].dtype)

    if not side_w:
        return

    @pl.when(pl.program_id(1) == 0)
    def _():
        w = jnp.concatenate([w_ref[0] for w_ref in side_w], axis=0).astype(BF16)
        prod = lax.dot_general(a_ref[...], w, _NT, preferred_element_type=F32)
        for n, s_ref in enumerate(side_o):
            s_ref[...] = prod[:, n * LANES:(n + 1) * LANES].astype(s_ref.dtype)


def _matmul_t(a, wt, *, layer, row0, ncols, tm, tn, out_dtype, side_rows=(), cast=None, skip=None):
    m, k = a.shape
    nj = ncols // tn
    wspec = lambda rows, imap: pl.BlockSpec((pl.Element(1), pl.Element(rows), pl.Element(k)), imap)
    jump = (lambda j: 0) if skip is None else (lambda j: jnp.where(j >= skip[0], skip[1], 0))
    in_specs = ([pl.BlockSpec((tm, k), lambda i, j: (i, 0)),
                 wspec(tn, lambda i, j: (layer, pl.multiple_of(row0 + j * tn + jump(j), 8), 0))]
                + [wspec(LANES, lambda i, j, r=r: (layer, r, 0)) for r in side_rows])
    out_specs = ([pl.BlockSpec((tm, tn), lambda i, j: (i, j))]
                 + [pl.BlockSpec((tm, LANES), lambda i, j: (i, 0)) for _ in side_rows])
    out_shape = ([jax.ShapeDtypeStruct((m, ncols), out_dtype)]
                 + [jax.ShapeDtypeStruct((m, LANES), out_dtype) for _ in side_rows])
    args = [a, wt] + [wt] * len(side_rows)
    cast_steps = 0
    if cast is not None:
        cw, cl = cast
        rows, cols = cw.shape[1:]
        steps = (m // tm) * nj
        slab = 16 * -(-rows // (16 * steps))
        cast_steps = rows // slab
        assert cast_steps * slab == rows and cast_steps <= steps
        slab_at = lambda i, j: jnp.minimum(i * nj + j, cast_steps - 1)
        in_specs.append(pl.BlockSpec((None, slab, cols), lambda i, j: (cl, slab_at(i, j), 0)))
        out_specs.append(pl.BlockSpec((slab, cols), lambda i, j: (slab_at(i, j), 0)))
        out_shape.append(jax.ShapeDtypeStruct((rows, cols), BF16))
        args.append(cw)
    outs = pl.pallas_call(
        functools.partial(_mm_nt_kernel, nside=len(side_rows), ncast=int(cast is not None),
                          cast_steps=cast_steps),
        grid=(m // tm, nj),
        in_specs=in_specs, out_specs=out_specs, out_shape=out_shape,
        compiler_params=_cparams("parallel", "arbitrary"),
        name="mm_nt",
    )(*args)
    return outs[0] if len(outs) == 1 else tuple(outs)


FFN_COL_GROUP = 256


def _ffn_up_kernel(x_ref, wg_ref, wu_ref, wd_ref, o_ref, wdb_ref):
    x = x_ref[...]
    tn = o_ref.shape[1]
    for c0 in range(0, tn, FFN_COL_GROUP):
        cs = slice(c0, c0 + FFN_COL_GROUP)
        g = jnp.dot(x, wg_ref[:, cs].astype(BF16), preferred_element_type=F32)
        u = jnp.dot(x, wu_ref[:, cs].astype(BF16), preferred_element_type=F32)
        o_ref[:, cs] = (_silu(g) * u).astype(o_ref.dtype)
    wdb_ref[...] = wd_ref[...].astype(wdb_ref.dtype)


def _ffn_up(x_bf, w_gu, w_d, *, tm, tn, layer):
    m, k = x_bf.shape
    hidden = w_gu.shape[-1] // 2
    nb = hidden // tn
    steps = (m // tm) * nb
    slab = hidden // steps
    assert slab * steps == hidden and slab % 16 == 0
    return pl.pallas_call(
        _ffn_up_kernel,
        grid=(m // tm, nb),
        in_specs=[pl.BlockSpec((tm, k), lambda i, j: (i, 0)),
                  _wspec(w_gu, layer, k, tn, lambda i, j: (0, j)),
                  _wspec(w_gu, layer, k, tn, lambda i, j: (0, j + nb)),
                  _wspec(w_d, layer, slab, w_d.shape[-1], lambda i, j: (i * nb + j, 0))],
        out_specs=[pl.BlockSpec((tm, tn), lambda i, j: (i, j)),
                   pl.BlockSpec((slab, w_d.shape[-1]), lambda i, j: (i * nb + j, 0))],
        out_shape=[jax.ShapeDtypeStruct((m, hidden), BF16),
                   jax.ShapeDtypeStruct((hidden, w_d.shape[-1]), BF16)],
        compiler_params=_cparams("parallel", "arbitrary"),
        name="ffn_up",
    )(x_bf, w_gu, w_gu, w_d)


def _mm_ln_kernel(*refs):
    a_refs, (w_ref, res_ref, g_ref, b_ref, of_ref, ob_ref) = refs[:-6], refs[-6:]
    a = a_refs[0][...] if len(a_refs) == 1 else jnp.concatenate([r[...] for r in a_refs], axis=1)
    y = DEEPNORM_ALPHA * res_ref[...] + jnp.dot(a, w_ref[...], preferred_element_type=F32)
    mu = jnp.mean(y, axis=-1, keepdims=True)
    yc = y - mu
    var = jnp.mean(yc * yc, axis=-1, keepdims=True)
    out = yc * lax.rsqrt(var + 1e-5) * g_ref[...] + b_ref[...]
    of_ref[...] = out
    ob_ref[...] = out.astype(BF16)


def _mm_ln(parts, w, res, g, b, *, tm):
    m = res.shape[0]
    kdim, n = w.shape
    widths = tuple(p.shape[1] for p in parts)
    assert sum(widths) == kdim
    return pl.pallas_call(
        _mm_ln_kernel,
        grid=(m // tm,),
        in_specs=[pl.BlockSpec((tm, wd), lambda i: (i, 0)) for wd in widths] + [
            pl.BlockSpec((kdim, n), lambda i: (0, 0), pipeline_mode=pl.Buffered(1)),
            pl.BlockSpec((tm, n), lambda i: (i, 0)),
            pl.BlockSpec((1, n), lambda i: (0, 0)),
            pl.BlockSpec((1, n), lambda i: (0, 0))],
        out_specs=[pl.BlockSpec((tm, n), lambda i: (i, 0)),
                   pl.BlockSpec((tm, n), lambda i: (i, 0))],
        out_shape=[jax.ShapeDtypeStruct((m, n), F32), jax.ShapeDtypeStruct((m, n), BF16)],
        compiler_params=_cparams("parallel"),
        name="mm_ln",
    )(*parts, w, res, g, b)


def _gla_kernel(q_ref, k_ref, v_ref, r_ref, g_ref, wg2_ref, bg_ref, ng_ref, o_ref, st_ref, *, tile):
    L, H, DK, DV = GLA_CHUNK, GLA_HEADS, GLA_DK, GLA_DV

    @pl.when(pl.program_id(1) == 0)
    def _():
        st_ref[...] = jnp.zeros_like(st_ref)

    tril = _tril(L)
    tril_b = tril.astype(BF16)
    g16 = g_ref[:, :GLA_GATE_RANK]
    nchunk = tile // L
    hc = [(h, c) for c in range(nchunk) for h in range(H)]
    ksl = lambda h: slice(h * DK, (h + 1) * DK)
    vsl = lambda h: slice(h * DV, (h + 1) * DV)
    rsl = lambda c: slice(c * L, (c + 1) * L)
    log_a = [_log_sigmoid(_dot3(g16, wg2_ref[:, ksl(h)]) + bg_ref[:, ksl(h)]) / GLA_TAU for h in range(H)]
    b = {(h, c): _chunk_cumsum(tril_b, log_a[h][rsl(c)]) for h, c in hc}
    q_dec, k_inv, k_end, decay, vb = {}, {}, {}, {}, {}
    for h, c in hc:
        bb = b[h, c]
        b_last = bb[L - 1:L]
        kk = k_ref[rsl(c), ksl(h)]
        q_dec[h, c] = (q_ref[rsl(c), ksl(h)] * DK ** -0.5 * jnp.exp(bb)).astype(BF16)
        k_inv[h, c] = (kk * jnp.exp(-bb)).astype(BF16)
        k_end[h, c] = (kk * jnp.exp(b_last - bb)).astype(BF16)
        decay[h, c] = jnp.exp(b_last)
        vb[h, c] = v_ref[rsl(c), vsl(h)]
    scores = {p: _bdot_nt(q_dec[p], k_inv[p]) for p in hc}
    scores = {p: jnp.where(tril, scores[p], 0.0).astype(BF16) for p in hc}
    intra = {p: _bdot(scores[p], vb[p]) for p in hc}
    s_loc = {p: _bdot_tn(vb[p], k_end[p]) for p in hc}
    st = [st_ref[h] for h in range(H)]
    outs = {}
    for h, c in hc:
        outs[h, c] = intra[h, c] + _bdot_nt(q_dec[h, c], st[h])
        st[h] = st[h] * decay[h, c] + s_loc[h, c]
    for h in range(H):
        st_ref[h] = st[h]
        o = jnp.concatenate([outs[h, c] for c in range(nchunk)], axis=0)
        o = o * lax.rsqrt(jnp.mean(o * o, axis=-1, keepdims=True) + 1e-6) * ng_ref[...]
        o_ref[:, vsl(h)] = (o * _silu(r_ref[:, vsl(h)])).astype(o_ref.dtype)


def _gla(y1, y2, gates, wg2, bg, ng, *, batch, seq, tile, gr_block=0):
    nt = seq // tile
    qk_w, v_w = GLA_HEADS * GLA_DK, GLA_HEADS * GLA_DV
    row = lambda b, t: b * nt + t
    return pl.pallas_call(
        functools.partial(_gla_kernel, tile=tile),
        grid=(batch, nt),
        in_specs=[pl.BlockSpec((tile, qk_w), lambda b, t: (row(b, t), 0)),
                  pl.BlockSpec((tile, qk_w), lambda b, t: (row(b, t), 1)),
                  pl.BlockSpec((tile, v_w), lambda b, t: (row(b, t), 1)),
                  pl.BlockSpec((tile, v_w), lambda b, t: (row(b, t), gr_block)),
                  pl.BlockSpec((tile, LANES), lambda b, t: (row(b, t), 0)),
                  pl.BlockSpec((GLA_GATE_RANK, qk_w), lambda b, t: (0, 0)),
                  pl.BlockSpec((1, qk_w), lambda b, t: (0, 0)),
                  pl.BlockSpec((1, GLA_DV), lambda b, t: (0, 0))],
        out_specs=pl.BlockSpec((tile, v_w), lambda b, t: (row(b, t), 0)),
        out_shape=jax.ShapeDtypeStruct((batch * seq, v_w), BF16),
        scratch_shapes=[pltpu.VMEM((GLA_HEADS, GLA_DV, GLA_DK), F32)],
        compiler_params=_cparams("parallel", "arbitrary"),
        name="gla",
    )(y1, y1, y1, y2, gates, wg2, bg, ng)


LOG2E = 1.4426950408889634
LN2 = 0.6931471805599453
DIL_STRIDE1 = 4


def _dil_kernel(*refs, unit):
    ngrp = len(DIL_PAIRS)
    in_refs, (o_ref, og_ref, lg_ref, rg_ref) = refs[:5 * ngrp], refs[5 * ngrp:]
    u = pl.program_id(1)
    n = DIL_BLOCK
    qscale = DIL_HEAD_DIM ** -0.5 * LOG2E
    row = lax.broadcasted_iota(jnp.int32, (n, n), 0)
    col = lax.broadcasted_iota(jnp.int32, (n, n), 1)
    prev_band, cur_band = col >= row, col <= row
    ones = jnp.ones((n, LANES), BF16)
    for g, (_, d) in enumerate(DIL_PAIRS):
        q_ref, k_ref, v_ref, kp_ref, vp_ref = in_refs[5 * g:5 * g + 5]
        sub = n * d
        blocks = []
        if d > DIL_STRIDE1:
            d2 = d // DIL_STRIDE1
            assert unit == sub and d2 <= DIL_STRIDE1
            per = unit // DIL_STRIDE1
            for a, src in enumerate((q_ref, k_ref, v_ref, kp_ref, vp_ref)):
                for r1 in range(DIL_STRIDE1):
                    rg_ref[a, r1] = src[pl.ds(r1, per, stride=DIL_STRIDE1), :]
            for r in range(d):
                at = (r % DIL_STRIDE1, pl.ds(r // DIL_STRIDE1, n, stride=d2))
                blocks.append(tuple((rg_ref, (a,) + at) for a in range(5))
                              + (prev_band & (u > 0), ((rg_ref, (5,) + at), (rg_ref, (6,) + at))))
        else:
            for s in range(unit // sub):
                for r in range(d):
                    cur = pl.ds(s * sub + r, n, stride=d) if d > 1 else pl.ds(s * sub, n)
                    outs = ((og_ref, (g, cur)), (lg_ref, (g, cur)))
                    if s == 0:
                        prv = pl.ds(r, n, stride=d) if d > 1 else pl.ds(0, n)
                        blocks.append(((q_ref, (cur,)), (k_ref, (cur,)), (v_ref, (cur,)), (kp_ref, (prv,)),
                                       (vp_ref, (prv,)), prev_band & (u > 0), outs))
                    else:
                        prv = pl.ds((s - 1) * sub + r, n, stride=d) if d > 1 else pl.ds((s - 1) * sub, n)
                        blocks.append(((q_ref, (cur,)), (k_ref, (cur,)), (v_ref, (cur,)), (k_ref, (prv,)),
                                       (v_ref, (prv,)), prev_band, outs))
        get = lambda ra: ra[0][ra[1] + (slice(None),)]
        scores = []
        for qa, kc, _, kp, _, pmask, _ in blocks:
            q = (get(qa) * qscale).astype(BF16)
            scores.append((jnp.where(pmask, _bdot_nt(q, get(kp)), NEG_INF),
                           jnp.where(cur_band, _bdot_nt(q, get(kc)), NEG_INF)))
        probs = []
        for s_p, s_c in scores:
            m = jnp.maximum(jnp.max(s_p, axis=-1, keepdims=True), jnp.max(s_c, axis=-1, keepdims=True))
            probs.append((jnp.exp2(s_p - m).astype(BF16), jnp.exp2(s_c - m).astype(BF16), m))
        for (_, _, vc, _, vp, _, ((o_dst, o_at), (l_dst, l_at))), (p_p, p_c, m) in zip(blocks, probs):
            den = _bdot(p_p, ones) + _bdot(p_c, ones)
            o_dst[o_at + (slice(None),)] = (_bdot(p_p, get(vp)) + _bdot(p_c, get(vc))) / den
            l_dst[l_at + (slice(None),)] = m * LN2 + jnp.log(den)
        if d > DIL_STRIDE1:
            for r1 in range(DIL_STRIDE1):
                og_ref[g, pl.ds(r1, per, stride=DIL_STRIDE1), :] = rg_ref[5, r1]
                lg_ref[g, pl.ds(r1, per, stride=DIL_STRIDE1), :] = rg_ref[6, r1]
    lses = [lg_ref[g] for g in range(ngrp)]
    m = functools.reduce(jnp.maximum, lses)
    ws = [jnp.exp(l - m) for l in lses]
    tot = functools.reduce(lambda a, b: a + b, ws)
    acc = sum(w * og_ref[g] for g, w in enumerate(ws))
    o_ref[...] = (acc / tot).astype(o_ref.dtype)


def _dilated(y, col0, *, batch, seq):
    ngrp, H, n = len(DIL_PAIRS), DIL_HEADS, DIL_BLOCK
    unit = max(d for _, d in DIL_PAIRS) * n
    assert seq % unit == 0
    nu = seq // unit
    cb0 = col0 // LANES
    in_specs, args = [], []
    for g, (_, d) in enumerate(DIL_PAIRS):
        sub = n * d
        per = unit // sub
        for part in range(3):
            cb = cb0 + part * ngrp * H + g * H
            in_specs.append(pl.BlockSpec((unit, LANES), lambda b, u, h, cb=cb: (b * nu + u, cb + h)))
            args.append(y)
        for part in (1, 2):
            cb = cb0 + part * ngrp * H + g * H
            in_specs.append(pl.BlockSpec(
                (sub, LANES),
                lambda b, u, h, cb=cb, per=per: (jnp.maximum(b * nu * per + u * per - 1, 0), cb + h)))
            args.append(y)
    return pl.pallas_call(
        functools.partial(_dil_kernel, unit=unit),
        grid=(batch, nu, H),
        in_specs=in_specs,
        out_specs=pl.BlockSpec((unit, LANES), lambda b, u, h: (b * nu + u, h)),
        out_shape=jax.ShapeDtypeStruct((batch * seq, H * LANES), BF16),
        scratch_shapes=[pltpu.VMEM((ngrp, unit, LANES), F32), pltpu.VMEM((ngrp, unit, LANES), F32),
                        pltpu.VMEM((7, DIL_STRIDE1, unit // DIL_STRIDE1, LANES), F32)],
        compiler_params=_cparams("parallel", "parallel", "arbitrary"),
        name="dilated",
    )(*args)


I_LANE, F_LANE = 0, MLSTM_HEADS


def _mlstm_kernel(qk_ref, v_ref, co_ref, gt_ref, cw_ref, cb_ref, gb_ref, ng_ref, o_ref,
                  c_ref, m_ref, tail_ref, xbuf_ref, *, tile):
    L, H, DK, DV = MLSTM_CHUNK, MLSTM_HEADS, MLSTM_DQK, MLSTM_DV
    KC = MLSTM_CONV
    PAD = 8

    @pl.when(pl.program_id(1) == 0)
    def _():
        c_ref[...] = jnp.zeros_like(c_ref)
        m_ref[...] = jnp.zeros_like(m_ref)
        tail_ref[...] = jnp.zeros_like(tail_ref)

    xbuf_ref[0:PAD, :] = tail_ref[...]
    xbuf_ref[PAD:PAD + tile, :] = qk_ref[...]
    tail_ref[...] = qk_ref[tile - PAD:tile, :]
    acc = cb_ref[...] + cw_ref[KC - 1:KC, :] * xbuf_ref[PAD:PAD + tile, :]
    for j in range(KC - 1):
        off = PAD - (KC - 1) + j
        acc = acc + cw_ref[j:j + 1, :] * xbuf_ref[off:off + tile, :]
    qk = _silu(acc)

    gates = gt_ref[...] + gb_ref[...]
    lane = lax.broadcasted_iota(jnp.int32, gates.shape, 1)
    z = jnp.where(lane >= F_LANE, _log_sigmoid(gates), gates)
    tril = _tril(L)
    tril_b = tril.astype(BF16)
    nchunk = tile // L
    lane_c = lax.broadcasted_iota(jnp.int32, (L, LANES), 1)
    zs, bs, wts = [], [], []
    for c in range(nchunk):
        zc = z[c * L:(c + 1) * L]
        bc = _chunk_cumsum(tril_b, zc)
        zs.append(zc)
        bs.append(bc)
        wts.append(jnp.where(lane_c >= F_LANE, bc, zc).T)
    ones_col = (lane_c == 0).astype(F32)

    hc = [(h, c) for c in range(nchunk) for h in range(H)]
    rsl = lambda c: slice(c * L, (c + 1) * L)
    row_c = lax.broadcasted_iota(jnp.int32, (L, LANES), 0)
    m_run = m_ref[0:1, :]
    kw_all, mt_all, isc_all, floor_all, so_all, sn_all = [], [], [], [], [], []
    for c in range(nchunk):
        bc = bs[c]
        li_a = pltpu.roll(zs[c], F_LANE - I_LANE, 1)
        b_last = bc[L - 1:L]
        a = b_last - bc + li_a
        m_loc = jnp.max(a, axis=0, keepdims=True)
        kw_all.append(jnp.exp(a - m_loc))
        x = li_a - bc
        for s in (1, 2, 4, 8, 16, 32):
            x = jnp.maximum(x, jnp.where(row_c >= s, pltpu.roll(x, s, 0), NEG_INF))
        inter_log = bc + m_run
        m_t = jnp.maximum(inter_log, bc + x)
        mt_all.append(m_t)
        isc_all.append(jnp.exp(inter_log - m_t))
        floor_all.append(jnp.exp(-m_t))
        m_new = jnp.maximum(b_last + m_run, m_loc)
        so_all.append(jnp.exp(b_last + m_run - m_new))
        sn_all.append(jnp.exp(m_loc - m_new))
        m_run = m_new
    m_ref[0:1, :] = m_run
    col = lambda x, h: x[:, F_LANE + h:F_LANE + h + 1]
    qb, kb, vb, qkm = {}, {}, {}, {}
    for h, c in hc:
        qb[h, c] = qk[rsl(c), h * DK:(h + 1) * DK].astype(BF16)
        kb[h, c] = qk[rsl(c), (H + h) * DK:(H + h + 1) * DK] * DK ** -0.5
        vb[h, c] = v_ref[rsl(c), h * DV:(h + 1) * DV]
    raw = {p: _bdot_nt(qb[p], kb[p]) for p in hc}
    for h, c in hc:
        b_row = wts[c][F_LANE + h:F_LANE + h + 1, :]
        li_row = wts[c][I_LANE + h:I_LANE + h + 1, :]
        d_log = jnp.where(tril, col(bs[c], h) - b_row + li_row, NEG_INF)
        qkm[h, c] = raw[h, c] * jnp.exp(d_log - col(mt_all[c], h))
    v_aug = {p: jnp.concatenate([vb[p], ones_col], axis=1).astype(BF16) for p in hc}
    intra = {p: _bdot(qkm[p], v_aug[p]) for p in hc}
    c_loc = {(h, c): _bdot_tn(kb[h, c] * col(kw_all[c], h), v_aug[h, c]) for h, c in hc}
    ct = [c_ref[h] for h in range(H)]
    outs = {}
    for h, c in hc:
        p = (h, c)
        tot = intra[p] + col(isc_all[c], h) * _bdot(qb[p], ct[h])
        outs[p] = tot[:, :DV] / jnp.maximum(jnp.abs(tot[:, DV:DV + 1]), col(floor_all[c], h))
        ct[h] = ct[h] * col(so_all[c], h) + c_loc[p] * col(sn_all[c], h)
    for h in range(H):
        c_ref[h] = ct[h]
        hcat = jnp.concatenate([outs[h, c] for c in range(nchunk)], axis=0)
        mu = jnp.mean(hcat, axis=-1, keepdims=True)
        hcen = hcat - mu
        var = jnp.mean(hcen * hcen, axis=-1, keepdims=True)
        hn = hcen * lax.rsqrt(var + 1e-5) * ng_ref[...]
        o_ref[:, h * DV:(h + 1) * DV] = (_sigmoid(co_ref[:, h * DV:(h + 1) * DV]) * hn).astype(o_ref.dtype)


def _mlstm(y1, y2, gates, conv_w, conv_b, gate_bias, ng, *, batch, seq, tile, co_block=0):
    nt = seq // tile
    H, DK, DV = MLSTM_HEADS, MLSTM_DQK, MLSTM_DV
    qk_w, v_w = 2 * H * DK, H * DV
    row = lambda b, t: b * nt + t
    return pl.pallas_call(
        functools.partial(_mlstm_kernel, tile=tile),
        grid=(batch, nt),
        in_specs=[pl.BlockSpec((tile, qk_w), lambda b, t: (row(b, t), 0)),
                  pl.BlockSpec((tile, v_w), lambda b, t: (row(b, t), 1)),
                  pl.BlockSpec((tile, v_w), lambda b, t: (row(b, t), co_block)),
                  pl.BlockSpec((tile, LANES), lambda b, t: (row(b, t), 0)),
                  pl.BlockSpec((MLSTM_CONV, qk_w), lambda b, t: (0, 0)),
                  pl.BlockSpec((1, qk_w), lambda b, t: (0, 0)),
                  pl.BlockSpec((1, LANES), lambda b, t: (0, 0)),
                  pl.BlockSpec((1, DV), lambda b, t: (0, 0))],
        out_specs=pl.BlockSpec((tile, v_w), lambda b, t: (row(b, t), 0)),
        out_shape=jax.ShapeDtypeStruct((batch * seq, v_w), BF16),
        scratch_shapes=[pltpu.VMEM((H, DK, DV + LANES), F32),
                        pltpu.VMEM((8, LANES), F32),
                        pltpu.VMEM((8, qk_w), F32),
                        pltpu.VMEM((tile + 8, qk_w), F32)],
        compiler_params=_cparams("parallel", "arbitrary"),
        name="mlstm",
    )(y1, y1, y2, gates, conv_w, conv_b, gate_bias, ng)


def _rope128(x, c_ref, s1_ref, s2_ref):
    return x * c_ref[...] + pltpu.roll(x, 32, 1) * s1_ref[...] + pltpu.roll(x, LANES - 32, 1) * s2_ref[...]


def _rms(x, g):
    return x * lax.rsqrt(jnp.mean(x * x, axis=-1, keepdims=True) + 1e-6) * g


MLA_SCORE_SCALE = (MLA_NOPE + MLA_ROPE) ** -0.5 * 1.4426950408889634


def _mla_proj_kernel(xq_ref, xkv_ref, qg_ref, kvg_ref, wq_ref, wk_ref, wvt_ref, kr_ref, c_ref, s1_ref, s2_ref,
                     q_ref, k_ref, vt_ref):
    q = _bdot(_rms(xq_ref[...], qg_ref[...]), wq_ref[...]) * MLA_SCORE_SCALE
    xn = _rms(xkv_ref[...], kvg_ref[...]).astype(BF16)
    kn = jnp.dot(xn, wk_ref[...], preferred_element_type=F32)
    vt_ref[...] = lax.dot_general(wvt_ref[...], xn, _NT, preferred_element_type=F32).astype(vt_ref.dtype)
    kr = _rope128(kr_ref[...], c_ref, s1_ref, s2_ref).astype(k_ref.dtype)
    for h in range(MLA_HEADS):
        q_ref[:, 256 * h:256 * h + 128] = q[:, 256 * h:256 * h + 128].astype(q_ref.dtype)
        q_ref[:, 256 * h + 128:256 * h + 256] = _rope128(
            q[:, 256 * h + 128:256 * h + 256], c_ref, s1_ref, s2_ref).astype(q_ref.dtype)
        k_ref[:, 256 * h:256 * h + 128] = kn[:, 128 * h:128 * h + 128].astype(k_ref.dtype)
        k_ref[:, 256 * h + 128:256 * h + 256] = kr


def _mla_proj(y, qcb, gates, qg, kvg, wq, wk, wvt, tabs, *, tm):
    m = y.shape[0]
    H = MLA_HEADS
    row_block = lambda width, col: pl.BlockSpec((tm, width), lambda i: (i, col))
    full = lambda a: pl.BlockSpec(a.shape, lambda i: (0, 0))
    return pl.pallas_call(
        _mla_proj_kernel,
        grid=(m // tm,),
        in_specs=[row_block(MLA_Q_RANK, qcb), row_block(MLA_KV_RANK, qcb + 1), full(qg), full(kvg),
                  full(wq), full(wk), full(wvt)] + [row_block(LANES, 0)] * 4,
        out_specs=[row_block(H * 256, 0), row_block(H * 256, 0),
                   pl.BlockSpec((H * MLA_DV, tm), lambda i: (0, i))],
        out_shape=[jax.ShapeDtypeStruct((m, H * 256), BF16), jax.ShapeDtypeStruct((m, H * 256), BF16),
                   jax.ShapeDtypeStruct((H * MLA_DV, m), BF16)],
        compiler_params=_cparams("parallel"),
        name="mla_proj",
    )(y, y, qg, kvg, wq, wk, wvt, gates, *tabs)


def _mla_attn_kernel(q_ref, k_ref, vt_ref, o_ref, s_ref, mx_ref, *, tq, hg):
    i = pl.program_id(2)
    DV = MLA_DV

    def scores(j, slot):
        start = pl.multiple_of(j * tq, tq)
        for g in range(hg):
            st = lax.dot_general(k_ref[pl.ds(start, tq), 256 * g:256 * (g + 1)],
                                 q_ref[:, 256 * g:256 * (g + 1)], _NT, preferred_element_type=F32)
            s_ref[slot, g] = st
            mx_ref[slot, g] = jnp.max(st, axis=0, keepdims=True)

    def step(j, slot, carry, masked):
        start = pl.multiple_of(j * tq, tq)
        out = []
        for g in range(hg):
            m, l, acc = carry[g]
            vtb = vt_ref[DV * g:DV * (g + 1), pl.ds(start, tq)]
            st = s_ref[slot, g]
            if masked:
                kk = lax.broadcasted_iota(jnp.int32, (tq, tq), 0)
                qq = lax.broadcasted_iota(jnp.int32, (tq, tq), 1)
                st = jnp.where(kk <= qq, st, NEG_INF)
                m_new = jnp.maximum(m, jnp.max(st, axis=0, keepdims=True))
            else:
                m_new = jnp.maximum(m, mx_ref[slot, g])
            pt = jnp.exp2(st - m_new)
            a = jnp.exp2(m - m_new)
            l = a * l + jnp.sum(pt, axis=0, keepdims=True)
            acc = a * acc + jnp.dot(vtb, pt.astype(BF16), preferred_element_type=F32)
            out.append((m_new, l, acc))
        return tuple(out)

    init = tuple((jnp.full((1, tq), NEG_INF, F32), jnp.zeros((1, tq), F32), jnp.zeros((DV, tq), F32))
                 for _ in range(hg))

    def pair(jj, carry):
        scores(2 * jj + 1, 1)
        carry = step(2 * jj, 0, carry, False)
        scores(2 * jj + 2, 0)
        return step(2 * jj + 1, 1, carry, False)

    def finish(carry):
        for g in range(hg):
            _, l, acc = carry[g]
            o_ref[:, DV * g:DV * (g + 1)] = (acc / l).T.astype(o_ref.dtype)

    scores(0, 0)
    carry = lax.fori_loop(0, i // 2, pair, init)

    @pl.when(i % 2 == 0)
    def _():
        finish(step(i, 0, carry, True))

    @pl.when(i % 2 == 1)
    def _():
        scores(i, 1)
        finish(step(i, 1, step(i - 1, 0, carry, False), True))


def _mla_attn(qf, kf, vt, *, batch, seq, tq, hg=4):
    H = MLA_HEADS
    nq = seq // tq
    return pl.pallas_call(
        functools.partial(_mla_attn_kernel, tq=tq, hg=hg),
        grid=(batch, H // hg, nq),
        in_specs=[pl.BlockSpec((tq, 256 * hg), lambda b, h, i: (b * nq + i, h)),
                  pl.BlockSpec((seq, 256 * hg), lambda b, h, i: (b, h)),
                  pl.BlockSpec((MLA_DV * hg, seq), lambda b, h, i: (h, b))],
        out_specs=pl.BlockSpec((tq, MLA_DV * hg), lambda b, h, i: (b * nq + i, h)),
        out_shape=jax.ShapeDtypeStruct((batch * seq, H * MLA_DV), BF16),
        scratch_shapes=[pltpu.VMEM((2, hg, tq, tq), F32), pltpu.VMEM((2, hg, 1, tq), F32)],
        compiler_params=_cparams("parallel", "parallel", "arbitrary"),
        name="mla_attn",
    )(qf, kf, vt)


def _tile_for(n, pref):
    t = pref
    while n % t:
        t //= 2
    return t


def _even_mixer(x_bf, wt_all, w_o_all, j, wg2, bg, ng, *, batch, seq):
    m = x_bf.shape[0]
    gq_gk_gv = 2 * GLA_HEADS * GLA_DK + GLA_HEADS * GLA_DV
    rest0 = gq_gk_gv + GLA_GATE_RANK
    tm = _tile_for(m, 2048)
    mm = functools.partial(_matmul_t, x_bf, wt_all, layer=j, tm=tm, out_dtype=F32)
    tn = 512
    y, gts, wo_bf = mm(row0=0, ncols=wt_all.shape[1] - GLA_GATE_RANK, tn=tn, skip=(gq_gk_gv // tn, GLA_GATE_RANK),
                       side_rows=(gq_gk_gv,), cast=(w_o_all, j))
    o_a = _gla(y, y, gts, wg2, bg.reshape(1, -1), ng.reshape(1, -1), batch=batch, seq=seq,
               tile=_tile_for(seq, 512), gr_block=gq_gk_gv // (GLA_HEADS * GLA_DV))
    o_b = _dilated(y, gq_gk_gv + GLA_HEADS * GLA_DV, batch=batch, seq=seq)
    return (o_a, o_b), wo_bf


def _rope_tables(positions):
    inv_freq = ROPE_THETA ** (-jnp.arange(0, MLA_ROPE, 2, dtype=F32) / MLA_ROPE)
    ang = positions.astype(F32).reshape(-1, 1) * inv_freq
    cos, sin = jnp.cos(ang), jnp.sin(ang)
    z32 = jnp.zeros_like(cos)
    z64 = jnp.zeros((cos.shape[0], LANES - MLA_ROPE), F32)
    c = jnp.concatenate([z64, cos, cos], axis=1)
    s1 = jnp.concatenate([z64, z32, sin], axis=1)
    s2 = jnp.concatenate([z64, -sin, z32], axis=1)
    return c, s1, s2


def _odd_mixer(x_bf, tabs, wt_all, w_o_all, j, conv_w, conv_b, bi, bf, ng, qg, kvg, wuq, wukv, *, batch, seq):
    m = x_bf.shape[0]
    H = MLA_HEADS
    cq_ck_cv = 2 * MLSTM_HEADS * MLSTM_DQK + MLSTM_HEADS * MLSTM_DV
    co0 = cq_ck_cv + 2 * MLSTM_HEADS
    kr0 = co0 + MLSTM_HEADS * MLSTM_DV + MLA_Q_RANK + MLA_KV_RANK
    gate_bias = jnp.concatenate([bi, bf, jnp.zeros((LANES - 2 * MLSTM_HEADS,), F32)]).reshape(1, LANES)
    wq3 = wuq.reshape(-1, H, MLA_NOPE + MLA_ROPE)
    wq = jnp.concatenate([wq3[:, :, :MLA_NOPE], jnp.zeros(wq3.shape[:2] + (LANES - MLA_ROPE,), F32),
                          wq3[:, :, MLA_NOPE:]], axis=2).reshape(-1, H * 256).astype(BF16)
    wkv = wukv.reshape(-1, H, MLA_NOPE + MLA_DV)
    wk = wkv[:, :, :MLA_NOPE].reshape(-1, H * MLA_NOPE).astype(BF16)
    wvt = wkv[:, :, MLA_NOPE:].reshape(-1, H * MLA_DV).T.astype(BF16)
    tm = _tile_for(m, 2048)
    mm = functools.partial(_matmul_t, x_bf, wt_all, layer=j, tm=tm, out_dtype=F32)
    tn = 512
    y, gts, kr, wo_bf = mm(row0=0, ncols=kr0 - (co0 - cq_ck_cv), tn=tn, skip=(cq_ck_cv // tn, co0 - cq_ck_cv),
                           side_rows=(cq_ck_cv, kr0 + MLA_ROPE - LANES), cast=(w_o_all, j))
    co_col = cq_ck_cv
    o_c = _mlstm(y, y, gts, conv_w, conv_b.reshape(1, -1), gate_bias, ng.reshape(1, -1),
                 batch=batch, seq=seq, tile=_tile_for(seq, 512), co_block=co_col // (MLSTM_HEADS * MLSTM_DV))
    qf, kf, vt = _mla_proj(y, (co_col + MLSTM_HEADS * MLSTM_DV) // MLA_Q_RANK, kr, qg.reshape(1, -1),
                           kvg.reshape(1, -1), wq, wk, wvt, tabs, tm=_tile_for(m, 512))
    o_d = _mla_attn(qf, kf, vt, batch=batch, seq=seq, tq=_tile_for(seq, 512))
    return (o_c, o_d), wo_bf


def kernel(x, positions, even_w_in, even_gla_wg2, even_gla_bg, even_gla_norm_g, even_w_o, odd_w_in, odd_conv_w, odd_conv_b, odd_mlstm_bi, odd_mlstm_bf, odd_mlstm_norm_g, odd_mla_qnorm_g, odd_mla_kvnorm_g, odd_mla_wuq, odd_mla_wukv, odd_w_o, ln1_g, ln1_b, ffn_wgu, ffn_wd, ln2_g, ln2_b):
    batch, seq, d = x.shape
    m = batch * seq
    xf = x.reshape(m, d)
    xb = xf.astype(BF16)
    tabs = _rope_tables(positions)
    even_wt = jnp.swapaxes(even_w_in, 1, 2)
    odd_wt = jnp.swapaxes(odd_w_in, 1, 2)
    depth = ln1_g.shape[0]
    tm_ln = _tile_for(m, 512)
    for l in range(depth):
        j = l // 2
        if l % 2 == 0:
            parts, wo_bf = _even_mixer(xb, even_wt, even_w_o, j, even_gla_wg2[j], even_gla_bg[j],
                                       even_gla_norm_g[j], batch=batch, seq=seq)
        else:
            parts, wo_bf = _odd_mixer(xb, tabs, odd_wt, odd_w_o, j, odd_conv_w[j], odd_conv_b[j],
                                      odd_mlstm_bi[j], odd_mlstm_bf[j], odd_mlstm_norm_g[j], odd_mla_qnorm_g[j],
                                      odd_mla_kvnorm_g[j], odd_mla_wuq[j], odd_mla_wukv[j], batch=batch, seq=seq)
        xf, xb = _mm_ln(parts, wo_bf, xf, ln1_g[l].reshape(1, -1), ln1_b[l].reshape(1, -1), tm=tm_ln)
        hid, wd_bf = _ffn_up(xb, ffn_wgu, ffn_wd, tm=_tile_for(m, 2048), tn=512, layer=l)
        xf, xb = _mm_ln((hid,), wd_bf, xf, ln2_g[l].reshape(1, -1), ln2_b[l].reshape(1, -1),
                        tm=_tile_for(m, 256))
    return xf.reshape(batch, seq, d)
```
